```python
import math
import jax
import jax.numpy as jnp
from jax import lax
import numpy as np


D_MODEL = 1024
BATCH = 4
SEQ = 8192
DEPTH = 1

N_META = 16
EPS = 1e-5
SSD_HEAD_DIM = 64
SSD_WIDTH = D_MODEL
SSD_HEADS = SSD_WIDTH // SSD_HEAD_DIM
SSD_GROUPS = 2
SSD_STATE = 128
SSD_CONV = 4
SSD_CHUNK = 128
GLA_HEADS = 4
GLA_K_WIDTH = D_MODEL // 2
GLA_V_WIDTH = D_MODEL
GLA_KEY_DIM = GLA_K_WIDTH // GLA_HEADS
GLA_VAL_DIM = GLA_V_WIDTH // GLA_HEADS
GLA_RANK = 16
GLA_TAU = 16.0
GLA_CHUNK = 16
MIX_WIDTH = SSD_WIDTH + GLA_V_WIDTH
XBC_WIDTH = SSD_WIDTH + 2 * SSD_GROUPS * SSD_STATE
IN_WIDTH = SSD_WIDTH + XBC_WIDTH + SSD_HEADS + 2 * GLA_K_WIDTH + 2 * GLA_V_WIDTH + GLA_RANK
N_EXPERTS = 32
TOP_K = 4
D_FF = D_MODEL
SWIGLU_LIMIT = 7.0
SWIGLU_ALPHA = 1.702
MOE_BLOCK = 256

kernel_name = "hymba_ssd_gla_moe_block"


def rms_norm(t, w):
    tf = t.astype(jnp.float32)
    tf = tf * lax.rsqrt(jnp.mean(tf * tf, axis=-1, keepdims=True) + EPS)
    return (tf * w.astype(jnp.float32)).astype(t.dtype)


def causal_depthwise_conv(u, w, b):
    c = u.shape[-1]
    out = lax.conv_general_dilated(u, w[:, None, :].astype(u.dtype), window_strides=(1,),
                                   padding=[(SSD_CONV - 1, 0)],
                                   dimension_numbers=('NWC', 'WIO', 'NWC'),
                                   feature_group_count=c)
    return out + b


def ssd_chunked(x, dt, a, bm, cm):
    b, lp, g, hg, p = x.shape
    n = bm.shape[-1]
    nc = lp // SSD_CHUNK
    q = SSD_CHUNK
    x = x.reshape(b, nc, q, g, hg, p)
    dt = dt.reshape(b, nc, q, g, hg)
    bm = bm.reshape(b, nc, q, g, n)
    cm = cm.reshape(b, nc, q, g, n)
    a_cs = jnp.cumsum(dt * a, axis=2)
    dtx = x * dt[..., None]
    acs_t = jnp.moveaxis(a_cs, 2, -1)
    seg = acs_t[..., :, None] - acs_t[..., None, :]
    causal = jnp.tril(jnp.ones((q, q), dtype=bool))
    decay = jnp.exp(jnp.where(causal, seg, -jnp.inf))
    cb = jnp.einsum('bclgn,bcsgn->bcgls', cm, bm)
    y_diag = jnp.einsum('bcghls,bcsghp->bclghp', cb[:, :, :, None] * decay, dtx)
    decay_to_end = jnp.exp(a_cs[:, :, -1:] - a_cs)
    states = jnp.einsum('bcsgn,bcsghp->bcghpn', bm, dtx * decay_to_end[..., None])
    chunk_decay = jnp.exp(a_cs[:, :, -1])

    def step(h, inp):
        dec, st = inp
        return dec[..., None, None] * h + st, h

    h0 = jnp.zeros((b, g, hg, p, n), states.dtype)
    _, h_prev = lax.scan(step, h0, (jnp.moveaxis(chunk_decay, 1, 0), jnp.moveaxis(states, 1, 0)))
    h_prev = jnp.moveaxis(h_prev, 0, 1)
    y_off = jnp.einsum('bclgn,bcghpn->bclghp', cm, h_prev) * jnp.exp(a_cs)[..., None]
    return (y_diag + y_off).reshape(b, lp, g, hg, p)


def gla_chunked(q, k, v, log_a):
    b, l, h, dk = q.shape
    dv = v.shape[-1]
    nc = l // GLA_CHUNK
    c = GLA_CHUNK
    q = q.reshape(b, nc, c, h, dk)
    k = k.reshape(b, nc, c, h, dk)
    v = v.reshape(b, nc, c, h, dv)
    g_cs = jnp.cumsum(log_a.reshape(b, nc, c, h, dk), axis=2)
    q_dec = q * jnp.exp(g_cs)
    k_inv = k * jnp.exp(-g_cs)
    k_end = k * jnp.exp(g_cs[:, :, -1:] - g_cs)
    causal = jnp.tril(jnp.ones((c, c), dtype=bool))
    scores = jnp.einsum('bclhd,bcshd->bchls', q_dec, k_inv)
    o_intra = jnp.einsum('bchls,bcshv->bclhv', jnp.where(causal, scores, 0.0), v)
    chunk_decay = jnp.exp(g_cs[:, :, -1])

    def step(s, inp):
        qd, ke, vc, dec = inp
        o = jnp.einsum('blhd,bhdv->blhv', qd, s)
        s = dec[..., None] * s + jnp.einsum('blhd,blhv->bhdv', ke, vc)
        return s, o

    s0 = jnp.zeros((b, h, dk, dv), jnp.float32)
    _, o_inter = lax.scan(step, s0, (jnp.moveaxis(q_dec, 1, 0), jnp.moveaxis(k_end, 1, 0),
                                     jnp.moveaxis(v, 1, 0), jnp.moveaxis(chunk_decay, 1, 0)))
    o = o_intra + jnp.moveaxis(o_inter, 0, 1)
    return o.reshape(b, l, h, dv)


def hybrid_mixer(u, w_in, conv_w, conv_b, dt_bias, a_log, d_skip, ssd_norm_w,
                 w_decay_up, b_decay, gla_norm_w, w_out):
    b, l, _ = u.shape
    proj = u @ w_in
    cuts = np.cumsum([SSD_WIDTH, XBC_WIDTH, SSD_HEADS, GLA_K_WIDTH, GLA_K_WIDTH,
                      GLA_V_WIDTH, GLA_V_WIDTH]).tolist()
    z, xbc, dt_raw, q, k, v, g_out, a_lr = jnp.split(proj, cuts, axis=-1)

    xbc = jax.nn.silu(causal_depthwise_conv(xbc, conv_w, conv_b))
    xs, bm, cm = jnp.split(xbc, [SSD_WIDTH, SSD_WIDTH + SSD_GROUPS * SSD_STATE], axis=-1)
    dt = jax.nn.softplus(dt_raw.astype(jnp.float32) + dt_bias.astype(jnp.float32))
    a = -jnp.exp(a_log.astype(jnp.float32))
    pad = SSD_CHUNK - N_META
    lp = l + pad
    hg = SSD_HEADS // SSD_GROUPS

    def pad_front(t):
        return jnp.pad(t, ((0, 0), (pad, 0)) + ((0, 0),) * (t.ndim - 2))

    y = ssd_chunked(pad_front(xs).reshape(b, lp, SSD_GROUPS, hg, SSD_HEAD_DIM),
                    pad_front(dt).reshape(b, lp, SSD_GROUPS, hg),
                    a.reshape(SSD_GROUPS, hg),
                    pad_front(bm).reshape(b, lp, SSD_GROUPS, SSD_STATE),
                    pad_front(cm).reshape(b, lp, SSD_GROUPS, SSD_STATE))
    y = y[:, pad:].reshape(b, l, SSD_HEADS, SSD_HEAD_DIM) \
        + d_skip[:, None] * xs.reshape(b, l, SSD_HEADS, SSD_HEAD_DIM)
    gsz = SSD_WIDTH // SSD_GROUPS
    y = y.reshape(b, l, SSD_GROUPS, gsz) * jax.nn.silu(z).reshape(b, l, SSD_GROUPS, gsz)
    y_ssd = rms_norm(y, ssd_norm_w.reshape(SSD_GROUPS, gsz)).reshape(b, l, SSD_WIDTH)

    log_a = jax.nn.log_sigmoid((a_lr @ w_decay_up + b_decay).astype(jnp.float32)) / GLA_TAU
    o = gla_chunked((q * GLA_KEY_DIM ** -0.5).reshape(b, l, GLA_HEADS, GLA_KEY_DIM),
                    k.reshape(b, l, GLA_HEADS, GLA_KEY_DIM),
                    v.reshape(b, l, GLA_HEADS, GLA_VAL_DIM),
                    log_a.reshape(b, l, GLA_HEADS, GLA_KEY_DIM))
    o = rms_norm(o, gla_norm_w) * jax.nn.silu(g_out.astype(jnp.float32)).reshape(b, l, GLA_HEADS, GLA_VAL_DIM)
    y_gla = o.reshape(b, l, GLA_V_WIDTH)

    mixed = jnp.concatenate([y_ssd.astype(u.dtype), y_gla.astype(u.dtype)], axis=-1)
    return mixed @ w_out


def moe_ffn(t, w_router, b_router, w_gate_up, b_gate_up, w_down, b_down):
    n, d = t.shape
    logits = (t @ w_router + b_router).astype(jnp.float32)
    top_logits, top_idx = lax.top_k(logits, TOP_K)
    gates = jax.nn.softmax(top_logits, axis=-1)
    m = n * TOP_K
    expert_flat = top_idx.reshape(m)
    order = jnp.argsort(expert_flat)
    expert_sorted = expert_flat[order]
    counts = jnp.bincount(expert_flat, length=N_EXPERTS)
    padded = (counts + MOE_BLOCK - 1) // MOE_BLOCK * MOE_BLOCK
    start = jnp.cumsum(counts) - counts
    pend = jnp.cumsum(padded)
    pstart = pend - padded
    dest_sorted = (pstart[expert_sorted] + jnp.arange(m) - start[expert_sorted]).astype(jnp.int32)
    n_blocks = -(-m // MOE_BLOCK) + N_EXPERTS
    rows = n_blocks * MOE_BLOCK
    src_token = jnp.full((rows,), n, jnp.int32).at[dest_sorted].set((order // TOP_K).astype(jnp.int32))
    t_pad = jnp.concatenate([t, jnp.zeros((1, d), t.dtype)], axis=0)
    buf = t_pad[src_token].reshape(n_blocks, MOE_BLOCK, d)
    block_expert = jnp.minimum(jnp.searchsorted(pend, jnp.arange(n_blocks) * MOE_BLOCK, side='right'),
                               N_EXPERTS - 1)

    def expert_block(args):
        xb, e = args
        hgu = xb @ w_gate_up[e] + b_gate_up[e]
        gate, up = jnp.split(hgu, 2, axis=-1)
        gate = jnp.minimum(gate, SWIGLU_LIMIT)
        up = jnp.clip(up, -SWIGLU_LIMIT, SWIGLU_LIMIT)
        act = gate * jax.nn.sigmoid(SWIGLU_ALPHA * gate)
        return ((up + 1.0) * act) @ w_down[e] + b_down[e]

    y_buf = lax.map(expert_block, (buf, block_expert)).reshape(rows, d)
    dest = jnp.zeros((m,), jnp.int32).at[order].set(dest_sorted)
    y = y_buf[dest].reshape(n, TOP_K, d)
    return jnp.einsum('nk,nkd->nd', gates.astype(t.dtype), y)


def setup_inputs(seed: int = 0) -> dict:
    key = jax.random.key(seed)
    ks = jax.random.split(key, 24)
    f32 = jnp.float32
    nrm = lambda k, shape, scale: jax.random.normal(k, shape, f32) * scale
    dt = jnp.exp(jax.random.uniform(ks[6], (DEPTH, SSD_HEADS), f32)
                 * (math.log(0.1) - math.log(1e-3)) + math.log(1e-3))
    return {
        'x': nrm(ks[0], (BATCH, SEQ, D_MODEL), 1.0),
        'meta_tokens': nrm(ks[1], (N_META, D_MODEL), 1.0),
        'mix_norm_w': 1.0 + nrm(ks[2], (DEPTH, D_MODEL), 0.02),
        'w_in': nrm(ks[3], (DEPTH, D_MODEL, IN_WIDTH), D_MODEL ** -0.5),
        'conv_w': nrm(ks[4], (DEPTH, SSD_CONV, XBC_WIDTH), SSD_CONV ** -0.5),
        'conv_b': nrm(ks[5], (DEPTH, XBC_WIDTH), 0.01),
        'dt_bias': dt + jnp.log(-jnp.expm1(-dt)),
        'a_log': jnp.log(jax.random.uniform(ks[7], (DEPTH, SSD_HEADS), f32, 1.0, 16.0)),
        'd_skip': 1.0 + nrm(ks[8], (DEPTH, SSD_HEADS), 0.1),
        'ssd_norm_w': 1.0 + nrm(ks[9], (DEPTH, SSD_WIDTH), 0.02),
        'w_decay_up': nrm(ks[10], (DEPTH, GLA_RANK, GLA_K_WIDTH), GLA_RANK ** -0.5),
        'b_decay': nrm(ks[11], (DEPTH, GLA_K_WIDTH), 0.01),
        'gla_norm_w': 1.0 + nrm(ks[12], (DEPTH, GLA_VAL_DIM), 0.02),
        'w_out': nrm(ks[13], (DEPTH, MIX_WIDTH, D_MODEL), MIX_WIDTH ** -0.5),
        'ffn_norm_w': 1.0 + nrm(ks[14], (DEPTH, D_MODEL), 0.02),
        'w_router': nrm(ks[15], (DEPTH, D_MODEL, N_EXPERTS), D_MODEL ** -0.5),
        'b_router': nrm(ks[16], (DEPTH, N_EXPERTS), 0.01),
        'w_gate_up': nrm(ks[17], (DEPTH, N_EXPERTS, D_MODEL, 2 * D_FF), D_MODEL ** -0.5),
        'b_gate_up': nrm(ks[18], (DEPTH, N_EXPERTS, 2 * D_FF), 0.01),
        'w_down': nrm(ks[19], (DEPTH, N_EXPERTS, D_FF, D_MODEL), D_FF ** -0.5),
        'b_down': nrm(ks[20], (DEPTH, N_EXPERTS, D_MODEL), 0.01),
        'final_norm_w': 1.0 + nrm(ks[21], (D_MODEL,), 0.02),
    }


def reference(x, meta_tokens, mix_norm_w, w_in, conv_w, conv_b, dt_bias, a_log, d_skip,
              ssd_norm_w, w_decay_up, b_decay, gla_norm_w, w_out, ffn_norm_w, w_router,
              b_router, w_gate_up, b_gate_up, w_down, b_down, final_norm_w):
    b = x.shape[0]
    meta = jnp.broadcast_to(meta_tokens[None].astype(x.dtype), (b, N_META, D_MODEL))
    h = jnp.concatenate([meta, x], axis=1)
    for i in range(DEPTH):
        h = h + hybrid_mixer(rms_norm(h, mix_norm_w[i]), w_in[i], conv_w[i], conv_b[i],
                             dt_bias[i], a_log[i], d_skip[i], ssd_norm_w[i], w_decay_up[i],
                             b_decay[i], gla_norm_w[i], w_out[i])
        if i == DEPTH - 1:
            h = h[:, N_META:]
        bb, ll, dd = h.shape
        ffn = moe_ffn(rms_norm(h, ffn_norm_w[i]).reshape(bb * ll, dd), w_router[i], b_router[i],
                      w_gate_up[i], b_gate_up[i], w_down[i], b_down[i])
        h = h + ffn.reshape(bb, ll, dd)
    return rms_norm(h, final_norm_w)
```

```python
import functools

import jax
import jax.numpy as jnp
from jax import lax
from jax.experimental import pallas as pl
from jax.experimental.pallas import tpu as pltpu

F32 = jnp.float32
BF16 = jnp.bfloat16

D_MODEL = 1024
N_META = 16
EPS = 1e-5
SSD_HEAD_DIM = 64
SSD_HEADS = 16
SSD_GROUPS = 2
SSD_STATE = 128
SSD_CONV = 4
SSD_WIDTH = SSD_HEADS * SSD_HEAD_DIM
XBC_WIDTH = SSD_WIDTH + 2 * SSD_GROUPS * SSD_STATE
GLA_HEADS = 4
GLA_KEY_DIM = 128
GLA_VAL_DIM = 256
GLA_K_WIDTH = GLA_HEADS * GLA_KEY_DIM
GLA_V_WIDTH = GLA_HEADS * GLA_VAL_DIM
GLA_RANK = 16
GLA_TAU = 16.0
GLA_SUB = 16
N_EXPERTS = 32
TOP_K = 4
D_FF = D_MODEL
SWIGLU_LIMIT = 7.0
SWIGLU_ALPHA = 1.702
MOE_BLOCK = 256

CHUNK = 128
FRONT_PAD = CHUNK - N_META
MISC_WIDTH = 128
IN_PAD_WIDTH = SSD_WIDTH + XBC_WIDTH + 2 * GLA_K_WIDTH + 2 * GLA_V_WIDTH + MISC_WIDTH
TOKEN_TILE = 256
HALF = D_MODEL // 2
VMEM_LIMIT = 56 * 1024 * 1024


def _split2(x):
    hi = x.astype(BF16)
    lo = (x - hi.astype(F32)).astype(BF16)
    return hi, lo


def _split3(x):
    hi = x.astype(BF16)
    r = x - hi.astype(F32)
    mid = r.astype(BF16)
    lo = (r - mid.astype(F32)).astype(BF16)
    return hi, mid, lo


def _dot(a, b):
    return jnp.dot(a, b, preferred_element_type=F32)


def _dot_nt(a, b):
    return lax.dot_general(a, b, (((1,), (1,)), ((), ())), preferred_element_type=F32)


def _dot_tn(a, b):
    return lax.dot_general(a, b, (((0,), (0,)), ((), ())), preferred_element_type=F32)


def _silu(x):
    return x * jax.nn.sigmoid(x)


_IN_SECTIONS = (SSD_WIDTH, XBC_WIDTH, GLA_K_WIDTH, GLA_K_WIDTH, GLA_V_WIDTH, GLA_V_WIDTH, MISC_WIDTH)


def _in_proj_kernel(h_ref, nw_ref, w_ref, z_ref, xbc_ref, q_ref, k_ref, v_ref, g_ref, misc_ref):
    h = h_ref[...]
    u = h * lax.rsqrt(jnp.mean(h * h, axis=-1, keepdims=True) + EPS) * nw_ref[...]
    ub = u.astype(BF16)
    outs = (z_ref, xbc_ref, q_ref, k_ref, v_ref, g_ref, misc_ref)
    off = 0
    for o_ref, width in zip(outs, _IN_SECTIONS):
        o_ref[...] = _dot(ub, w_ref[:, off:off + width]).astype(o_ref.dtype)
        off += width


def _in_proj(hp, norm_w, w_in_r):
    rows = hp.shape[0]
    tm = 256 if rows % 256 == 0 else CHUNK
    row_spec = lambda w: pl.BlockSpec((tm, w), lambda i: (i, 0))
    out_dtypes = (BF16,) * 6 + (F32,)
    return pl.pallas_call(
        _in_proj_kernel,
        grid=(rows // tm,),
        in_specs=[
            row_spec(D_MODEL),
            pl.BlockSpec((1, D_MODEL), lambda i: (0, 0)),
            pl.BlockSpec((D_MODEL, IN_PAD_WIDTH), lambda i: (0, 0)),
        ],
        out_specs=[row_spec(w) for w in _IN_SECTIONS],
        out_shape=[jax.ShapeDtypeStruct((rows, w), dt) for w, dt in zip(_IN_SECTIONS, out_dtypes)],
        compiler_params=pltpu.CompilerParams(
            dimension_semantics=("parallel",), vmem_limit_bytes=VMEM_LIMIT),
        name="in_proj",
    )(hp, norm_w, w_in_r)


def _block_rows(x, rows, width):
    parts = []
    for r in rows:
        if r is None:
            parts.append(jnp.zeros((width, x.shape[1]), x.dtype))
        else:
            parts.append(jnp.broadcast_to(x[r:r + 1, :], (width, x.shape[1])))
    return jnp.concatenate(parts, axis=0)


def _mixer_kernel(z_ref, xbc_ref, q_ref, k_ref, v_ref, g_ref, misc_ref,
                  convw_ref, convb_ref, dtb_ref, aneg_ref, dskip_ref, ssdw_ref,
                  wdec_ref, bdec_ref, glaw_ref,
                  out_ref,
                  xext_ref, sstate_ref, gstate_ref):
    c = pl.program_id(1)

    @pl.when(c == 0)
    def _():
        xext_ref[0:8, :] = jnp.zeros((8, XBC_WIDTH), F32)
        sstate_ref[...] = jnp.zeros_like(sstate_ref)
        gstate_ref[...] = jnp.zeros_like(gstate_ref)

    row = lax.broadcasted_iota(jnp.int32, (CHUNK, CHUNK), 0)
    col = lax.broadcasted_iota(jnp.int32, (CHUNK, CHUNK), 1)
    causal = row >= col
    tri = jnp.where(causal, 1.0, 0.0).astype(BF16)
    valid = jnp.logical_or(c > 0, row >= FRONT_PAD)

    xext_ref[8:8 + CHUNK, :] = xbc_ref[...].astype(F32)
    conv = convb_ref[...] + convw_ref[0:1, :] * xext_ref[5:5 + CHUNK, :]
    for kk in range(1, SSD_CONV):
        conv = conv + convw_ref[kk:kk + 1, :] * xext_ref[5 + kk:5 + kk + CHUNK, :]
    xext_ref[0:8, :] = xext_ref[CHUNK:CHUNK + 8, :]
    valid_w = jnp.logical_or(c > 0, lax.broadcasted_iota(jnp.int32, (CHUNK, XBC_WIDTH), 0) >= FRONT_PAD)
    act = jnp.where(valid_w, _silu(conv), 0.0)
    xs = act[:, :SSD_WIDTH]
    bmat = act[:, SSD_WIDTH:SSD_WIDTH + SSD_GROUPS * SSD_STATE].astype(BF16)
    cmat = act[:, SSD_WIDTH + SSD_GROUPS * SSD_STATE:].astype(BF16)

    misc = misc_ref[...]
    dt = jnp.where(valid, jax.nn.softplus(misc + dtb_ref[...]), 0.0)
    d_a = dt * aneg_ref[...]
    p0, p1, p2 = _split3(d_a)
    a_cs = _dot(tri, p0) + _dot(tri, p1) + _dot(tri, p2)
    a_cs_t = a_cs.T

    lo_half = col < SSD_HEAD_DIM
    hg = SSD_HEADS // SSD_GROUPS
    pairs_per_group = hg // 2
    y_parts = []
    for gi in range(SSD_GROUPS):
        b_g = bmat[:, gi * SSD_STATE:(gi + 1) * SSD_STATE]
        c_g = cmat[:, gi * SSD_STATE:(gi + 1) * SSD_STATE]
        c_g32 = c_g.astype(F32)
        cb = _dot_nt(c_g, b_g)
        w_parts = []
        cd_parts = []
        for pj in range(pairs_per_group):
            h0 = gi * hg + 2 * pj
            h1 = h0 + 1
            lanes = slice(h0 * SSD_HEAD_DIM, (h1 + 1) * SSD_HEAD_DIM)
            ab0 = jnp.broadcast_to(a_cs[:, h0:h0 + 1], (CHUNK, CHUNK))
            ab1 = jnp.broadcast_to(a_cs[:, h1:h1 + 1], (CHUNK, CHUNK))
            db0 = jnp.broadcast_to(dt[:, h0:h0 + 1], (CHUNK, CHUNK))
            db1 = jnp.broadcast_to(dt[:, h1:h1 + 1], (CHUNK, CHUNK))
            l0 = jnp.exp(jnp.where(causal, ab0 - a_cs_t[h0:h0 + 1, :], -jnp.inf))
            l1 = jnp.exp(jnp.where(causal, ab1 - a_cs_t[h1:h1 + 1, :], -jnp.inf))
            lhs = jnp.concatenate(
                [(cb * l0).astype(BF16), (c_g32 * jnp.exp(ab0)).astype(BF16),
                 (cb * l1).astype(BF16), (c_g32 * jnp.exp(ab1)).astype(BF16)], axis=1)
            xs_p = xs[:, lanes]
            dtx = xs_p * jnp.where(lo_half, db0, db1)
            st = sstate_ref[gi, :, pj * CHUNK:(pj + 1) * CHUNK]
            rhs = jnp.concatenate(
                [jnp.where(lo_half, dtx, 0.0).astype(BF16), jnp.where(lo_half, st, 0.0).astype(BF16),
                 jnp.where(lo_half, 0.0, dtx).astype(BF16), jnp.where(lo_half, 0.0, st).astype(BF16)],
                axis=0)
            y_p = _dot(lhs, rhs) + dskip_ref[:, lanes] * xs_p
            y_parts.append(y_p)
            ae = jnp.where(lo_half, ab0, ab1)
            a_last = ae[CHUNK - 1:CHUNK, :]
            w_parts.append((dtx * jnp.exp(a_last - ae)).astype(BF16))
            cd_parts.append(jnp.exp(a_last))
        upd = _dot_tn(b_g, jnp.concatenate(w_parts, axis=1))
        sstate_ref[gi] = jnp.concatenate(cd_parts, axis=1) * sstate_ref[gi] + upd
    y = jnp.concatenate(y_parts, axis=1)
    y = y * _silu(z_ref[...].astype(F32))
    gsz = SSD_WIDTH // SSD_GROUPS
    for gi in range(SSD_GROUPS):
        yg = y[:, gi * gsz:(gi + 1) * gsz]
        yn = yg * lax.rsqrt(jnp.mean(yg * yg, axis=-1, keepdims=True) + EPS)
        out_ref[:, gi * gsz:(gi + 1) * gsz] = (yn * ssdw_ref[:, gi * gsz:(gi + 1) * gsz]).astype(out_ref.dtype)

    m0, m1 = _split2(misc)
    pre = _dot(m0, wdec_ref[...]) + _dot(m1, wdec_ref[...]) + bdec_ref[...]
    log_a = jax.nn.log_sigmoid(pre) * (1.0 / GLA_TAU)
    g0, g1 = _split2(log_a)
    gcum = _dot(tri, g0) + _dot(tri, g1)
    roww = lax.broadcasted_iota(jnp.int32, (CHUNK, GLA_K_WIDTH), 0)
    qf = q_ref[...].astype(F32) * (GLA_KEY_DIM ** -0.5)
    kf = k_ref[...].astype(F32)
    nsub = CHUNK // GLA_SUB
    g_start = _block_rows(gcum, [None] + [GLA_SUB * i - 1 for i in range(1, nsub)], GLA_SUB)
    e0 = gcum - g_start
    q_lv = [(qf * jnp.exp(e0)).astype(BF16)]
    k_lv = [(kf * jnp.exp(-e0)).astype(BF16)]
    shifts = (4, 5, 6)
    for sh in shifts:
        b = 1 << sh
        bound = _block_rows(gcum, [2 * b * i + b - 1 for i in range(CHUNK // (2 * b))], 2 * b)
        second = ((roww >> sh) & 1) == 1
        x_l = jnp.exp(jnp.where(second, gcum - bound, bound - gcum))
        q_lv.append((qf * x_l).astype(BF16))
        k_lv.append((kf * x_l).astype(BF16))
    masks = [jnp.logical_and((row >> 4) == (col >> 4), causal)]
    for sh in shifts:
        same = (row >> (sh + 1)) == (col >> (sh + 1))
        m = jnp.logical_and(same, jnp.logical_and(((row >> sh) & 1) == 1, ((col >> sh) & 1) == 0))
        masks.append(m)
    g_last = gcum[CHUNK - 1:CHUNK, :]
    q_in = (qf * jnp.exp(gcum)).astype(BF16)
    k_end = (kf * jnp.exp(g_last - gcum)).astype(BF16)
    dec = jnp.exp(g_last)
    vb = v_ref[...]
    for hh in range(GLA_HEADS):
        kl = slice(hh * GLA_KEY_DIM, (hh + 1) * GLA_KEY_DIM)
        vl = slice(hh * GLA_VAL_DIM, (hh + 1) * GLA_VAL_DIM)
        scores = jnp.zeros((CHUNK, CHUNK), F32)
        for lv in range(len(masks)):
            scores = scores + jnp.where(masks[lv], _dot_nt(q_lv[lv][:, kl], k_lv[lv][:, kl]), 0.0)
        v_h = vb[:, vl]
        s_t = gstate_ref[hh]
        o = _dot(scores.astype(BF16), v_h) + _dot_nt(q_in[:, kl], s_t.astype(BF16))
        gstate_ref[hh] = dec[:, kl] * s_t + _dot_tn(v_h, k_end[:, kl])
        o = o * lax.rsqrt(jnp.mean(o * o, axis=-1, keepdims=True) + EPS) * glaw_ref[...]
        o = o * _silu(g_ref[:, vl].astype(F32))
        out_ref[:, SSD_WIDTH + hh * GLA_VAL_DIM:SSD_WIDTH + (hh + 1) * GLA_VAL_DIM] = o.astype(out_ref.dtype)


def _mixer(z, xbc, q, k, v, g, misc, params, batch, nchunks):
    widths = (SSD_WIDTH, XBC_WIDTH, GLA_K_WIDTH, GLA_K_WIDTH, GLA_V_WIDTH, GLA_V_WIDTH, MISC_WIDTH)
    row_spec = lambda w: pl.BlockSpec((CHUNK, w), lambda b, c: (b * nchunks + c, 0))
    par_spec = lambda p: pl.BlockSpec(p.shape, lambda b, c: (0,) * p.ndim)
    n_out = batch * (nchunks - 1) * CHUNK
    return pl.pallas_call(
        _mixer_kernel,
        grid=(batch, nchunks),
        in_specs=[row_spec(w) for w in widths] + [par_spec(p) for p in params],
        out_specs=pl.BlockSpec((CHUNK, SSD_WIDTH + GLA_V_WIDTH),
                               lambda b, c: (b * (nchunks - 1) + jnp.maximum(c - 1, 0), 0)),
        out_shape=jax.ShapeDtypeStruct((n_out, SSD_WIDTH + GLA_V_WIDTH), BF16),
        scratch_shapes=[
            pltpu.VMEM((CHUNK + 8, XBC_WIDTH), F32),
            pltpu.VMEM((SSD_GROUPS, SSD_STATE, SSD_WIDTH // SSD_GROUPS), F32),
            pltpu.VMEM((GLA_HEADS, GLA_VAL_DIM, GLA_KEY_DIM), F32),
        ],
        compiler_params=pltpu.CompilerParams(
            dimension_semantics=("arbitrary", "arbitrary"), vmem_limit_bytes=VMEM_LIMIT),
        name="mixer",
    )(z, xbc, q, k, v, g, misc, *params)


def _post_kernel(x_ref, mixed_ref, wout_ref, fnw_ref, wrt_ref, br_ref,
                 h2_ref, tp_ref, idx_ref, gate_ref, rank_ref, cnt_ref,
                 run_ref):
    i = pl.program_id(0)

    @pl.when(i == 0)
    def _():
        run_ref[...] = jnp.zeros_like(run_ref)

    h2 = x_ref[...] + _dot(mixed_ref[...], wout_ref[...])
    h2_ref[...] = h2
    t = h2 * lax.rsqrt(jnp.mean(h2 * h2, axis=-1, keepdims=True) + EPS) * fnw_ref[...]
    tb = t.astype(BF16)
    tb32 = tb.astype(F32)
    lo_bits = pltpu.bitcast(tb32[:, :HALF], jnp.uint32) >> 16
    hi_bits = pltpu.bitcast(tb32[:, HALF:], jnp.uint32) & jnp.uint32(0xFFFF0000)
    tp_ref[...] = lo_bits | hi_bits

    t0, t1 = _split2(t)
    w0 = wrt_ref[0]
    w1 = wrt_ref[1]
    logits = _dot_nt(w0, t0) + _dot_nt(w0, t1) + _dot_nt(w1, t0) + br_ref[...]
    e_iota = lax.broadcasted_iota(jnp.int32, logits.shape, 0)
    work = logits
    sel_any = jnp.zeros(logits.shape, F32)
    tops = []
    idxs = []
    onehots = []
    for _ in range(TOP_K):
        m = jnp.max(work, axis=0, keepdims=True)
        idx = jnp.min(jnp.where(work == m, e_iota, N_EXPERTS), axis=0, keepdims=True)
        hit = e_iota == idx
        tops.append(m)
        idxs.append(idx)
        onehots.append(hit)
        sel_any = sel_any + jnp.where(hit, 1.0, 0.0)
        work = jnp.where(hit, -jnp.inf, work)
    exps = [jnp.exp(tv - tops[0]) for tv in tops]
    denom = exps[0] + exps[1] + exps[2] + exps[3]
    zeros4 = jnp.zeros((8 - TOP_K, logits.shape[1]), F32)
    gate_ref[...] = jnp.concatenate([e / denom for e in exps] + [zeros4], axis=0)
    idx_ref[...] = jnp.concatenate(idxs, axis=0)

    tile = logits.shape[1]
    r_i = lax.broadcasted_iota(jnp.int32, (tile, tile), 0)
    c_i = lax.broadcasted_iota(jnp.int32, (tile, tile), 1)
    upper = jnp.where(r_i < c_i, 1.0, 0.0).astype(BF16)
    before = _dot(sel_any.astype(BF16), upper) + run_ref[:, 0:1]
    ranks = [jnp.sum(jnp.where(h, before, 0.0), axis=0, keepdims=True) for h in onehots]
    rank_ref[...] = jnp.concatenate(ranks, axis=0).astype(jnp.int32)
    run_new = run_ref[...] + jnp.sum(sel_any, axis=1, keepdims=True)
    run_ref[...] = run_new
    cnt_ref[...] = run_new.astype(jnp.int32)


def _post(x_flat, mixed, w_out_b, ffn_norm_w, w_router_t, b_router_col):
    n = x_flat.shape[0]
    tt = TOKEN_TILE
    const = lambda shape: pl.BlockSpec(shape, lambda i: (0,) * len(shape))
    return pl.pallas_call(
        _post_kernel,
        grid=(n // tt,),
        in_specs=[
            pl.BlockSpec((tt, D_MODEL), lambda i: (i, 0)),
            pl.BlockSpec((tt, SSD_WIDTH + GLA_V_WIDTH), lambda i: (i, 0)),
            const(w_out_b.shape),
            const((1, D_MODEL)),
            const(w_router_t.shape),
            const((N_EXPERTS, 1)),
        ],
        out_specs=[
            pl.BlockSpec((tt, D_MODEL), lambda i: (i, 0)),
            pl.BlockSpec((tt, HALF), lambda i: (i, 0)),
            pl.BlockSpec((TOP_K, tt), lambda i: (0, i)),
            pl.BlockSpec((8, tt), lambda i: (0, i)),
            pl.BlockSpec((TOP_K, tt), lambda i: (0, i)),
            const((N_EXPERTS, 128)),
        ],
        out_shape=[
            jax.ShapeDtypeStruct((n, D_MODEL), F32),
            jax.ShapeDtypeStruct((n, HALF), jnp.uint32),
            jax.ShapeDtypeStruct((TOP_K, n), jnp.int32),
            jax.ShapeDtypeStruct((8, n), F32),
            jax.ShapeDtypeStruct((TOP_K, n), jnp.int32),
            jax.ShapeDtypeStruct((N_EXPERTS, 128), jnp.int32),
        ],
        scratch_shapes=[pltpu.VMEM((N_EXPERTS, 128), F32)],
        compiler_params=pltpu.CompilerParams(
            dimension_semantics=("arbitrary",), vmem_limit_bytes=VMEM_LIMIT),
        name="post",
    )(x_flat, mixed, w_out_b, ffn_norm_w, w_router_t, b_router_col)


def _row_copy(src_ref, src_row, dst_ref, dst_row, sem):
    return pltpu.make_async_copy(src_ref.at[pl.ds(src_row, 1), :], dst_ref.at[pl.ds(dst_row, 1), :], sem)


def _dispatch_kernel(dest_ref, tp_ref, buf_in_ref, buf_ref, sem):
    del buf_in_ref
    tt = tp_ref.shape[0]

    def issue(t, carry):
        for kk in range(TOP_K):
            _row_copy(tp_ref, t, buf_ref, dest_ref[kk, t], sem.at[kk]).start()
        return carry

    lax.fori_loop(0, tt, issue, 0)

    def drain(t, carry):
        for kk in range(TOP_K):
            _row_copy(tp_ref, t, buf_ref, dest_ref[kk, t], sem.at[kk]).wait()
        return carry

    lax.fori_loop(0, tt, drain, 0)


def _dispatch(dest, t_packed, rows):
    n = t_packed.shape[0]
    tt = TOKEN_TILE
    buf0 = jnp.zeros((rows, HALF), jnp.uint32)
    return pl.pallas_call(
        _dispatch_kernel,
        grid=(n // tt,),
        in_specs=[
            pl.BlockSpec((TOP_K, tt), lambda i: (0, i), memory_space=pltpu.SMEM),
            pl.BlockSpec((tt, HALF), lambda i: (i, 0)),
            pl.BlockSpec(memory_space=pl.ANY),
        ],
        out_specs=pl.BlockSpec(memory_space=pl.ANY),
        out_shape=jax.ShapeDtypeStruct((rows, HALF), jnp.uint32),
        scratch_shapes=[pltpu.SemaphoreType.DMA((TOP_K,))],
        input_output_aliases={2: 0},
        compiler_params=pltpu.CompilerParams(
            dimension_semantics=("arbitrary",), vmem_limit_bytes=VMEM_LIMIT),
        name="dispatch",
    )(dest, t_packed, buf0)


def _expert_kernel(be_ref, xp_ref, wgu_ref, bgu_ref, wd_ref, bd_ref, y_ref):
    del be_ref
    w = xp_ref[...]
    x_lo = pltpu.bitcast(w << 16, F32).astype(BF16)
    x_hi = pltpu.bitcast(w & jnp.uint32(0xFFFF0000), F32).astype(BF16)
    hgu = _dot(x_lo, wgu_ref[:HALF, :]) + _dot(x_hi, wgu_ref[HALF:, :]) + bgu_ref[...]
    gate = jnp.minimum(hgu[:, :D_FF], SWIGLU_LIMIT)
    up = jnp.clip(hgu[:, D_FF:], -SWIGLU_LIMIT, SWIGLU_LIMIT)
    act = gate * jax.nn.sigmoid(SWIGLU_ALPHA * gate)
    y_ref[...] = _dot(((up + 1.0) * act).astype(BF16), wd_ref[...]) + bd_ref[...]


def _experts(block_expert, buf, w_gu_b, b_gu, w_d_b, b_d):
    rows = buf.shape[0]
    n_blocks = rows // MOE_BLOCK
    grid_spec = pltpu.PrefetchScalarGridSpec(
        num_scalar_prefetch=1,
        grid=(n_blocks,),
        in_specs=[
            pl.BlockSpec((MOE_BLOCK, HALF), lambda i, be: (i, 0)),
            pl.BlockSpec((None, D_MODEL, 2 * D_FF), lambda i, be: (be[i], 0, 0)),
            pl.BlockSpec((None, 1, 2 * D_FF), lambda i, be: (be[i], 0, 0)),
            pl.BlockSpec((None, D_FF, D_MODEL), lambda i, be: (be[i], 0, 0)),
            pl.BlockSpec((None, 1, D_MODEL), lambda i, be: (be[i], 0, 0)),
        ],
        out_specs=pl.BlockSpec((MOE_BLOCK, D_MODEL), lambda i, be: (i, 0)),
    )
    return pl.pallas_call(
        _expert_kernel,
        grid_spec=grid_spec,
        out_shape=jax.ShapeDtypeStruct((rows, D_MODEL), F32),
        compiler_params=pltpu.CompilerParams(
            dimension_semantics=("arbitrary",), vmem_limit_bytes=VMEM_LIMIT),
        name="experts",
    )(block_expert, buf, w_gu_b, b_gu, w_d_b, b_d)


def _combine_kernel(dest_ref, gate_ref, h2_ref, fw_ref, y_hbm_ref, out_ref, ybuf_ref, sem):
    tt = h2_ref.shape[0]

    def issue(t, carry):
        for kk in range(TOP_K):
            _row_copy(y_hbm_ref, dest_ref[kk, t], ybuf_ref.at[kk], t, sem.at[kk]).start()
        return carry

    lax.fori_loop(0, tt, issue, 0)

    def drain(t, carry):
        for kk in range(TOP_K):
            _row_copy(y_hbm_ref, dest_ref[kk, t], ybuf_ref.at[kk], t, sem.at[kk]).wait()
        return carry

    lax.fori_loop(0, tt, drain, 0)

    gpad = jnp.concatenate([gate_ref[...], jnp.zeros((128 - 8, tt), F32)], axis=0)
    gcol = gpad.T
    acc = h2_ref[...]
    for kk in range(TOP_K):
        acc = acc + gcol[:, kk:kk + 1] * ybuf_ref[kk]
    out_ref[...] = acc * lax.rsqrt(jnp.mean(acc * acc, axis=-1, keepdims=True) + EPS) * fw_ref[...]


def _combine(dest, gates, h2, final_norm_w, y_buf):
    n = h2.shape[0]
    tt = TOKEN_TILE
    return pl.pallas_call(
        _combine_kernel,
        grid=(n // tt,),
        in_specs=[
            pl.BlockSpec((TOP_K, tt), lambda i: (0, i), memory_space=pltpu.SMEM),
            pl.BlockSpec((8, tt), lambda i: (0, i)),
            pl.BlockSpec((tt, D_MODEL), lambda i: (i, 0)),
            pl.BlockSpec((1, D_MODEL), lambda i: (0, 0)),
            pl.BlockSpec(memory_space=pl.ANY),
        ],
        out_specs=pl.BlockSpec((tt, D_MODEL), lambda i: (i, 0)),
        out_shape=jax.ShapeDtypeStruct((n, D_MODEL), F32),
        scratch_shapes=[
            pltpu.VMEM((TOP_K, tt, D_MODEL), F32),
            pltpu.SemaphoreType.DMA((TOP_K,)),
        ],
        compiler_params=pltpu.CompilerParams(
            dimension_semantics=("arbitrary",), vmem_limit_bytes=VMEM_LIMIT),
        name="combine",
    )(dest, gates, h2, final_norm_w, y_buf)


def _pad_lanes(v, width):
    return jnp.pad(v, ((0, 0), (0, width - v.shape[1])))


def kernel(x, meta_tokens, mix_norm_w, w_in, conv_w, conv_b, dt_bias, a_log, d_skip, ssd_norm_w,
           w_decay_up, b_decay, gla_norm_w, w_out, ffn_norm_w, w_router, b_router, w_gate_up,
           b_gate_up, w_down, b_down, final_norm_w):
    batch, seq, d = x.shape
    assert d == D_MODEL and seq % TOKEN_TILE == 0
    assert mix_norm_w.shape[0] == 1, "single-layer block"
    nchunks = (FRONT_PAD + N_META + seq) // CHUNK

    meta = jnp.broadcast_to(meta_tokens[None].astype(x.dtype), (batch, N_META, D_MODEL))
    hp = jnp.concatenate([jnp.zeros((batch, FRONT_PAD, D_MODEL), x.dtype), meta, x], axis=1)
    hp = hp.reshape(batch * nchunks * CHUNK, D_MODEL)

    wi = w_in[0]
    o_z, o_xbc = 0, SSD_WIDTH
    o_dt = o_xbc + XBC_WIDTH
    o_q = o_dt + SSD_HEADS
    o_k = o_q + GLA_K_WIDTH
    o_v = o_k + GLA_K_WIDTH
    o_g = o_v + GLA_V_WIDTH
    o_a = o_g + GLA_V_WIDTH
    w_misc = jnp.concatenate(
        [wi[:, o_dt:o_dt + SSD_HEADS], wi[:, o_a:o_a + GLA_RANK],
         jnp.zeros((D_MODEL, MISC_WIDTH - SSD_HEADS - GLA_RANK), wi.dtype)], axis=1)
    w_in_r = jnp.concatenate(
        [wi[:, o_z:o_dt], wi[:, o_q:o_a], w_misc], axis=1).astype(BF16)
    z, xbc, q, k, v, g, misc = _in_proj(hp, mix_norm_w[0][None, :], w_in_r)

    dtb = _pad_lanes(dt_bias[0][None, :].astype(F32), MISC_WIDTH)
    aneg = _pad_lanes(-jnp.exp(a_log[0].astype(F32))[None, :], MISC_WIDTH)
    dskip = jnp.repeat(d_skip[0].astype(F32), SSD_HEAD_DIM)[None, :]
    wdec = jnp.zeros((MISC_WIDTH, GLA_K_WIDTH), F32).at[SSD_HEADS:SSD_HEADS + GLA_RANK].set(w_decay_up[0])
    params = (conv_w[0], conv_b[0][None, :], dtb, aneg, dskip, ssd_norm_w[0][None, :],
              wdec.astype(BF16), b_decay[0][None, :], gla_norm_w[0][None, :])
    mixed = _mixer(z, xbc, q, k, v, g, misc, params, batch, nchunks)

    n = batch * seq
    wr_hi, wr_lo = _split2(w_router[0].T.astype(F32))
    h2, t_packed, top_idx, gates, rank, counts = _post(
        x.reshape(n, D_MODEL), mixed, w_out[0].astype(BF16), ffn_norm_w[0][None, :],
        jnp.stack([wr_hi, wr_lo]), b_router[0][:, None])

    counts = counts[:, 0]
    padded = (counts + MOE_BLOCK - 1) // MOE_BLOCK * MOE_BLOCK
    pend = jnp.cumsum(padded)
    pstart = pend - padded
    dest = (pstart[top_idx] + rank).astype(jnp.int32)
    n_blocks = -(-(n * TOP_K) // MOE_BLOCK) + N_EXPERTS
    block_expert = jnp.minimum(
        jnp.searchsorted(pend, jnp.arange(n_blocks) * MOE_BLOCK, side='right'),
        N_EXPERTS - 1).astype(jnp.int32)

    buf = _dispatch(dest, t_packed, n_blocks * MOE_BLOCK)
    y_buf = _experts(block_expert, buf, w_gate_up[0].astype(BF16), b_gate_up[0][:, None, :],
                     w_down[0].astype(BF16), b_down[0][:, None, :])
    out = _combine(dest, gates, h2, final_norm_w[None, :], y_buf)
    return out.reshape(batch, seq, D_MODEL)
```

```python
import functools

import jax
import jax.numpy as jnp
from jax import lax
from jax.experimental import pallas as pl
from jax.experimental.pallas import tpu as pltpu

F32 = jnp.float32
BF16 = jnp.bfloat16

D_MODEL = 1024
N_META = 16
EPS = 1e-5
SSD_HEAD_DIM = 64
SSD_HEADS = 16
SSD_GROUPS = 2
SSD_STATE = 128
SSD_CONV = 4
SSD_WIDTH = SSD_HEADS * SSD_HEAD_DIM
XBC_WIDTH = SSD_WIDTH + 2 * SSD_GROUPS * SSD_STATE
GLA_HEADS = 4
GLA_KEY_DIM = 128
GLA_VAL_DIM = 256
GLA_K_WIDTH = GLA_HEADS * GLA_KEY_DIM
GLA_V_WIDTH = GLA_HEADS * GLA_VAL_DIM
GLA_RANK = 16
GLA_TAU = 16.0
GLA_SUB = 16
N_EXPERTS = 32
TOP_K = 4
D_FF = D_MODEL
SWIGLU_LIMIT = 7.0
SWIGLU_ALPHA = 1.702
MOE_BLOCK = 256

CHUNK = 128
FRONT_PAD = CHUNK - N_META
MISC_WIDTH = 128
IN_PAD_WIDTH = SSD_WIDTH + XBC_WIDTH + 2 * GLA_K_WIDTH + 2 * GLA_V_WIDTH + MISC_WIDTH
TOKEN_TILE = 256
HALF = D_MODEL // 2
VMEM_LIMIT = 56 * 1024 * 1024


def _split2(x):
    hi = x.astype(BF16)
    lo = (x - hi.astype(F32)).astype(BF16)
    return hi, lo


def _split3(x):
    hi = x.astype(BF16)
    r = x - hi.astype(F32)
    mid = r.astype(BF16)
    lo = (r - mid.astype(F32)).astype(BF16)
    return hi, mid, lo


def _dot(a, b):
    return jnp.dot(a, b, preferred_element_type=F32)


def _dot_nt(a, b):
    return lax.dot_general(a, b, (((1,), (1,)), ((), ())), preferred_element_type=F32)


def _dot_tn(a, b):
    return lax.dot_general(a, b, (((0,), (0,)), ((), ())), preferred_element_type=F32)


def _silu(x):
    return x * jax.nn.sigmoid(x)


_IN_SECTIONS = (SSD_WIDTH, XBC_WIDTH, GLA_K_WIDTH, GLA_K_WIDTH, GLA_V_WIDTH, GLA_V_WIDTH, MISC_WIDTH)


def _in_proj_kernel(x_ref, meta_ref, nw_ref, w_ref, z_ref, xbc_ref, q_ref, k_ref, v_ref, g_ref, misc_ref):
    h = jnp.where(pl.program_id(0) < pl.num_programs(0) - 1, x_ref[...], meta_ref[...])
    u = h * lax.rsqrt(jnp.mean(h * h, axis=-1, keepdims=True) + EPS) * nw_ref[...]
    ub = u.astype(BF16)
    outs = (z_ref, xbc_ref, q_ref, k_ref, v_ref, g_ref, misc_ref)
    off = 0
    for o_ref, width in zip(outs, _IN_SECTIONS):
        o_ref[...] = _dot(ub, w_ref[:, off:off + width]).astype(o_ref.dtype)
        off += width


def _in_proj(x_flat, meta_tile, norm_w, w_in_r):
    tm = TOKEN_TILE
    nx = x_flat.shape[0] // tm
    rows = (nx + 1) * tm
    row_spec = lambda w: pl.BlockSpec((tm, w), lambda i: (i, 0))
    out_dtypes = (BF16,) * 6 + (F32,)
    return pl.pallas_call(
        _in_proj_kernel,
        grid=(nx + 1,),
        in_specs=[
            pl.BlockSpec((tm, D_MODEL), lambda i: (jnp.minimum(i, nx - 1), 0)),
            pl.BlockSpec((tm, D_MODEL), lambda i: (0, 0)),
            pl.BlockSpec((1, D_MODEL), lambda i: (0, 0)),
            pl.BlockSpec((D_MODEL, IN_PAD_WIDTH), lambda i: (0, 0)),
        ],
        out_specs=[row_spec(w) for w in _IN_SECTIONS],
        out_shape=[jax.ShapeDtypeStruct((rows, w), dt) for w, dt in zip(_IN_SECTIONS, out_dtypes)],
        compiler_params=pltpu.CompilerParams(
            dimension_semantics=("arbitrary",), vmem_limit_bytes=VMEM_LIMIT),
        name="in_proj",
    )(x_flat, meta_tile, norm_w, w_in_r)


def _block_rows(x, rows, width):
    parts = []
    for r in rows:
        if r is None:
            parts.append(jnp.zeros((width, x.shape[1]), x.dtype))
        else:
            parts.append(jnp.broadcast_to(x[r:r + 1, :], (width, x.shape[1])))
    return jnp.concatenate(parts, axis=0)


def _mixer_kernel(z_ref, xbc_ref, q_ref, k_ref, v_ref, g_ref, misc_ref,
                  convw_ref, convb_ref, dtb_ref, aneg_ref, dskip_ref, ssdw_ref,
                  wdec_ref, bdec_ref, glaw_ref,
                  out_ref,
                  xext_ref, sstate_ref, gstate_ref):
    c = pl.program_id(1)

    @pl.when(c == 0)
    def _():
        xext_ref[0:8, :] = jnp.zeros((8, XBC_WIDTH), F32)
        sstate_ref[...] = jnp.zeros_like(sstate_ref)
        gstate_ref[...] = jnp.zeros_like(gstate_ref)

    row = lax.broadcasted_iota(jnp.int32, (CHUNK, CHUNK), 0)
    col = lax.broadcasted_iota(jnp.int32, (CHUNK, CHUNK), 1)
    causal = row >= col
    tri = jnp.where(causal, 1.0, 0.0).astype(BF16)
    valid = jnp.logical_or(c > 0, row >= FRONT_PAD)

    xext_ref[8:8 + CHUNK, :] = xbc_ref[...].astype(F32)
    conv = convb_ref[...] + convw_ref[0:1, :] * xext_ref[5:5 + CHUNK, :]
    for kk in range(1, SSD_CONV):
        conv = conv + convw_ref[kk:kk + 1, :] * xext_ref[5 + kk:5 + kk + CHUNK, :]
    xext_ref[0:8, :] = xext_ref[CHUNK:CHUNK + 8, :]
    valid_w = jnp.logical_or(c > 0, lax.broadcasted_iota(jnp.int32, (CHUNK, XBC_WIDTH), 0) >= FRONT_PAD)
    act = jnp.where(valid_w, _silu(conv), 0.0)
    xs = act[:, :SSD_WIDTH]
    bmat = act[:, SSD_WIDTH:SSD_WIDTH + SSD_GROUPS * SSD_STATE].astype(BF16)
    cmat = act[:, SSD_WIDTH + SSD_GROUPS * SSD_STATE:].astype(BF16)

    misc = misc_ref[...]
    dt = jnp.where(valid, jax.nn.softplus(misc + dtb_ref[...]), 0.0)
    d_a = dt * aneg_ref[...]
    p0, p1, p2 = _split3(d_a)
    a_cs = _dot(tri, p0) + _dot(tri, p1) + _dot(tri, p2)
    a_cs_t = a_cs.T

    lo_half = col < SSD_HEAD_DIM
    hg = SSD_HEADS // SSD_GROUPS
    pairs_per_group = hg // 2
    y_parts = []
    for gi in range(SSD_GROUPS):
        b_g = bmat[:, gi * SSD_STATE:(gi + 1) * SSD_STATE]
        c_g = cmat[:, gi * SSD_STATE:(gi + 1) * SSD_STATE]
        c_g32 = c_g.astype(F32)
        cb = _dot_nt(c_g, b_g)
        w_parts = []
        cd_parts = []
        for pj in range(pairs_per_group):
            h0 = gi * hg + 2 * pj
            h1 = h0 + 1
            lanes = slice(h0 * SSD_HEAD_DIM, (h1 + 1) * SSD_HEAD_DIM)
            ab0 = jnp.broadcast_to(a_cs[:, h0:h0 + 1], (CHUNK, CHUNK))
            ab1 = jnp.broadcast_to(a_cs[:, h1:h1 + 1], (CHUNK, CHUNK))
            db0 = jnp.broadcast_to(dt[:, h0:h0 + 1], (CHUNK, CHUNK))
            db1 = jnp.broadcast_to(dt[:, h1:h1 + 1], (CHUNK, CHUNK))
            l0 = jnp.exp(jnp.where(causal, ab0 - a_cs_t[h0:h0 + 1, :], -jnp.inf))
            l1 = jnp.exp(jnp.where(causal, ab1 - a_cs_t[h1:h1 + 1, :], -jnp.inf))
            lhs = jnp.concatenate(
                [(cb * l0).astype(BF16), (c_g32 * jnp.exp(ab0)).astype(BF16),
                 (cb * l1).astype(BF16), (c_g32 * jnp.exp(ab1)).astype(BF16)], axis=1)
            xs_p = xs[:, lanes]
            dtx = xs_p * jnp.where(lo_half, db0, db1)
            st = sstate_ref[gi, :, pj * CHUNK:(pj + 1) * CHUNK]
            rhs = jnp.concatenate(
                [jnp.where(lo_half, dtx, 0.0).astype(BF16), jnp.where(lo_half, st, 0.0).astype(BF16),
                 jnp.where(lo_half, 0.0, dtx).astype(BF16), jnp.where(lo_half, 0.0, st).astype(BF16)],
                axis=0)
            y_p = _dot(lhs, rhs) + dskip_ref[:, lanes] * xs_p
            y_parts.append(y_p)
            ae = jnp.where(lo_half, ab0, ab1)
            a_last = ae[CHUNK - 1:CHUNK, :]
            w_parts.append((dtx * jnp.exp(a_last - ae)).astype(BF16))
            cd_parts.append(jnp.exp(a_last))
        upd = _dot_tn(b_g, jnp.concatenate(w_parts, axis=1))
        sstate_ref[gi] = jnp.concatenate(cd_parts, axis=1) * sstate_ref[gi] + upd
    y = jnp.concatenate(y_parts, axis=1)
    y = y * _silu(z_ref[...].astype(F32))
    gsz = SSD_WIDTH // SSD_GROUPS
    for gi in range(SSD_GROUPS):
        yg = y[:, gi * gsz:(gi + 1) * gsz]
        yn = yg * lax.rsqrt(jnp.mean(yg * yg, axis=-1, keepdims=True) + EPS)
        out_ref[:, gi * gsz:(gi + 1) * gsz] = (yn * ssdw_ref[:, gi * gsz:(gi + 1) * gsz]).astype(out_ref.dtype)

    m0, m1 = _split2(misc)
    pre = _dot(m0, wdec_ref[...]) + _dot(m1, wdec_ref[...]) + bdec_ref[...]
    log_a = jax.nn.log_sigmoid(pre) * (1.0 / GLA_TAU)
    g0, g1 = _split2(log_a)
    gcum = _dot(tri, g0) + _dot(tri, g1)
    roww = lax.broadcasted_iota(jnp.int32, (CHUNK, GLA_K_WIDTH), 0)
    qf = q_ref[...].astype(F32) * (GLA_KEY_DIM ** -0.5)
    kf = k_ref[...].astype(F32)
    nsub = CHUNK // GLA_SUB
    g_start = _block_rows(gcum, [None] + [GLA_SUB * i - 1 for i in range(1, nsub)], GLA_SUB)
    e0 = gcum - g_start
    q_lv = [(qf * jnp.exp(e0)).astype(BF16)]
    k_lv = [(kf * jnp.exp(-e0)).astype(BF16)]
    shifts = (4, 5, 6)
    for sh in shifts:
        b = 1 << sh
        bound = _block_rows(gcum, [2 * b * i + b - 1 for i in range(CHUNK // (2 * b))], 2 * b)
        second = ((roww >> sh) & 1) == 1
        x_l = jnp.exp(jnp.where(second, gcum - bound, bound - gcum))
        q_lv.append((qf * x_l).astype(BF16))
        k_lv.append((kf * x_l).astype(BF16))
    masks = [jnp.logical_and((row >> 4) == (col >> 4), causal)]
    for sh in shifts:
        same = (row >> (sh + 1)) == (col >> (sh + 1))
        m = jnp.logical_and(same, jnp.logical_and(((row >> sh) & 1) == 1, ((col >> sh) & 1) == 0))
        masks.append(m)
    g_last = gcum[CHUNK - 1:CHUNK, :]
    q_in = (qf * jnp.exp(gcum)).astype(BF16)
    k_end = (kf * jnp.exp(g_last - gcum)).astype(BF16)
    dec = jnp.exp(g_last)
    vb = v_ref[...]
    for hh in range(GLA_HEADS):
        kl = slice(hh * GLA_KEY_DIM, (hh + 1) * GLA_KEY_DIM)
        vl = slice(hh * GLA_VAL_DIM, (hh + 1) * GLA_VAL_DIM)
        scores = jnp.zeros((CHUNK, CHUNK), F32)
        for lv in range(len(masks)):
            scores = scores + jnp.where(masks[lv], _dot_nt(q_lv[lv][:, kl], k_lv[lv][:, kl]), 0.0)
        v_h = vb[:, vl]
        s_t = gstate_ref[hh]
        o = _dot(scores.astype(BF16), v_h) + _dot_nt(q_in[:, kl], s_t.astype(BF16))
        gstate_ref[hh] = dec[:, kl] * s_t + _dot_tn(v_h, k_end[:, kl])
        o = o * lax.rsqrt(jnp.mean(o * o, axis=-1, keepdims=True) + EPS) * glaw_ref[...]
        o = o * _silu(g_ref[:, vl].astype(F32))
        out_ref[:, SSD_WIDTH + hh * GLA_VAL_DIM:SSD_WIDTH + (hh + 1) * GLA_VAL_DIM] = o.astype(out_ref.dtype)


def _mixer(z, xbc, q, k, v, g, misc, params, batch, nchunks):
    widths = (SSD_WIDTH, XBC_WIDTH, GLA_K_WIDTH, GLA_K_WIDTH, GLA_V_WIDTH, GLA_V_WIDTH, MISC_WIDTH)
    meta_block = z.shape[0] // CHUNK - 1
    row_spec = lambda w: pl.BlockSpec(
        (CHUNK, w), lambda b, c: (jnp.where(c == 0, meta_block, b * (nchunks - 1) + c - 1), 0))
    par_spec = lambda p: pl.BlockSpec(p.shape, lambda b, c: (0,) * p.ndim)
    n_out = batch * (nchunks - 1) * CHUNK
    return pl.pallas_call(
        _mixer_kernel,
        grid=(batch, nchunks),
        in_specs=[row_spec(w) for w in widths] + [par_spec(p) for p in params],
        out_specs=pl.BlockSpec((CHUNK, SSD_WIDTH + GLA_V_WIDTH),
                               lambda b, c: (b * (nchunks - 1) + jnp.maximum(c - 1, 0), 0)),
        out_shape=jax.ShapeDtypeStruct((n_out, SSD_WIDTH + GLA_V_WIDTH), BF16),
        scratch_shapes=[
            pltpu.VMEM((CHUNK + 8, XBC_WIDTH), F32),
            pltpu.VMEM((SSD_GROUPS, SSD_STATE, SSD_WIDTH // SSD_GROUPS), F32),
            pltpu.VMEM((GLA_HEADS, GLA_VAL_DIM, GLA_KEY_DIM), F32),
        ],
        compiler_params=pltpu.CompilerParams(
            dimension_semantics=("arbitrary", "arbitrary"), vmem_limit_bytes=VMEM_LIMIT),
        name="mixer",
    )(z, xbc, q, k, v, g, misc, *params)


def _post_kernel(x_ref, mixed_ref, wout_ref, fnw_ref, wrt_ref, br_ref,
                 h2_ref, tp_ref, idx_ref, gate_ref, rank_ref, cnt_ref,
                 run_ref):
    i = pl.program_id(0)

    @pl.when(i == 0)
    def _():
        run_ref[...] = jnp.zeros_like(run_ref)

    h2 = x_ref[...] + _dot(mixed_ref[...], wout_ref[...])
    h2_ref[...] = h2
    t = h2 * lax.rsqrt(jnp.mean(h2 * h2, axis=-1, keepdims=True) + EPS) * fnw_ref[...]
    tb = t.astype(BF16)
    tb32 = tb.astype(F32)
    lo_bits = pltpu.bitcast(tb32[:, :HALF], jnp.uint32) >> 16
    hi_bits = pltpu.bitcast(tb32[:, HALF:], jnp.uint32) & jnp.uint32(0xFFFF0000)
    tp_ref[...] = lo_bits | hi_bits

    t0, t1 = _split2(t)
    w0 = wrt_ref[0]
    w1 = wrt_ref[1]
    logits = _dot_nt(w0, t0) + _dot_nt(w0, t1) + _dot_nt(w1, t0) + br_ref[...]
    e_iota = lax.broadcasted_iota(jnp.int32, logits.shape, 0)
    work = logits
    sel_any = jnp.zeros(logits.shape, F32)
    tops = []
    idxs = []
    onehots = []
    for _ in range(TOP_K):
        m = jnp.max(work, axis=0, keepdims=True)
        idx = jnp.min(jnp.where(work == m, e_iota, N_EXPERTS), axis=0, keepdims=True)
        hit = e_iota == idx
        tops.append(m)
        idxs.append(idx)
        onehots.append(hit)
        sel_any = sel_any + jnp.where(hit, 1.0, 0.0)
        work = jnp.where(hit, -jnp.inf, work)
    exps = [jnp.exp(tv - tops[0]) for tv in tops]
    denom = exps[0] + exps[1] + exps[2] + exps[3]
    zeros4 = jnp.zeros((8 - TOP_K, logits.shape[1]), F32)
    gate_ref[...] = jnp.concatenate([e / denom for e in exps] + [zeros4], axis=0)
    idx_ref[...] = jnp.concatenate(idxs, axis=0)

    tile = logits.shape[1]
    r_i = lax.broadcasted_iota(jnp.int32, (tile, tile), 0)
    c_i = lax.broadcasted_iota(jnp.int32, (tile, tile), 1)
    upper = jnp.where(r_i < c_i, 1.0, 0.0).astype(BF16)
    before = _dot(sel_any.astype(BF16), upper) + run_ref[:, 0:1]
    ranks = [jnp.sum(jnp.where(h, before, 0.0), axis=0, keepdims=True) for h in onehots]
    rank_ref[...] = jnp.concatenate(ranks, axis=0).astype(jnp.int32)
    run_new = run_ref[...] + jnp.sum(sel_any, axis=1, keepdims=True)
    run_ref[...] = run_new
    cnt_ref[...] = run_new.astype(jnp.int32)


def _post(x_flat, mixed, w_out_b, ffn_norm_w, w_router_t, b_router_col):
    n = x_flat.shape[0]
    tt = TOKEN_TILE
    const = lambda shape: pl.BlockSpec(shape, lambda i: (0,) * len(shape))
    return pl.pallas_call(
        _post_kernel,
        grid=(n // tt,),
        in_specs=[
            pl.BlockSpec((tt, D_MODEL), lambda i: (i, 0)),
            pl.BlockSpec((tt, SSD_WIDTH + GLA_V_WIDTH), lambda i: (i, 0)),
            const(w_out_b.shape),
            const((1, D_MODEL)),
            const(w_router_t.shape),
            const((N_EXPERTS, 1)),
        ],
        out_specs=[
            pl.BlockSpec((tt, D_MODEL), lambda i: (i, 0)),
            pl.BlockSpec((tt, HALF), lambda i: (i, 0)),
            pl.BlockSpec((TOP_K, tt), lambda i: (0, i)),
            pl.BlockSpec((8, tt), lambda i: (0, i)),
            pl.BlockSpec((TOP_K, tt), lambda i: (0, i)),
            const((N_EXPERTS, 128)),
        ],
        out_shape=[
            jax.ShapeDtypeStruct((n, D_MODEL), F32),
            jax.ShapeDtypeStruct((n, HALF), jnp.uint32),
            jax.ShapeDtypeStruct((TOP_K, n), jnp.int32),
            jax.ShapeDtypeStruct((8, n), F32),
            jax.ShapeDtypeStruct((TOP_K, n), jnp.int32),
            jax.ShapeDtypeStruct((N_EXPERTS, 128), jnp.int32),
        ],
        scratch_shapes=[pltpu.VMEM((N_EXPERTS, 128), F32)],
        compiler_params=pltpu.CompilerParams(
            dimension_semantics=("arbitrary",), vmem_limit_bytes=VMEM_LIMIT),
        name="post",
    )(x_flat, mixed, w_out_b, ffn_norm_w, w_router_t, b_router_col)


def _dest_kernel(pstart_ref, idx_ref, rank_ref, dest_ref):
    idx = idx_ref[...]
    d = rank_ref[...]
    for e in range(N_EXPERTS):
        d = d + jnp.where(idx == e, pstart_ref[e], 0)
    dest_ref[...] = d


def _dest(pstart, top_idx, rank):
    return pl.pallas_call(
        _dest_kernel,
        in_specs=[
            pl.BlockSpec(memory_space=pltpu.SMEM),
            pl.BlockSpec(memory_space=pltpu.VMEM),
            pl.BlockSpec(memory_space=pltpu.VMEM),
        ],
        out_specs=pl.BlockSpec(memory_space=pltpu.VMEM),
        out_shape=jax.ShapeDtypeStruct(top_idx.shape, jnp.int32),
        name="dest",
    )(pstart, top_idx, rank)


def _row_copy(src_ref, src_row, dst_ref, dst_row, sem):
    return pltpu.make_async_copy(src_ref.at[pl.ds(src_row, 1), :], dst_ref.at[pl.ds(dst_row, 1), :], sem)


def _dispatch_kernel(dest_ref, tp_ref, buf_in_ref, buf_ref, sem):
    del buf_in_ref
    tt = tp_ref.shape[0]

    def issue(t, carry):
        for kk in range(TOP_K):
            _row_copy(tp_ref, t, buf_ref, dest_ref[kk, t], sem.at[kk]).start()
        return carry

    lax.fori_loop(0, tt, issue, 0)

    def drain(t, carry):
        for kk in range(TOP_K):
            _row_copy(tp_ref, t, buf_ref, dest_ref[kk, t], sem.at[kk]).wait()
        return carry

    lax.fori_loop(0, tt, drain, 0)


def _dispatch(dest, t_packed, rows):
    n = t_packed.shape[0]
    tt = TOKEN_TILE
    buf0 = jnp.zeros((rows, HALF), jnp.uint32)
    return pl.pallas_call(
        _dispatch_kernel,
        grid=(n // tt,),
        in_specs=[
            pl.BlockSpec((TOP_K, tt), lambda i: (0, i), memory_space=pltpu.SMEM),
            pl.BlockSpec((tt, HALF), lambda i: (i, 0)),
            pl.BlockSpec(memory_space=pl.ANY),
        ],
        out_specs=pl.BlockSpec(memory_space=pl.ANY),
        out_shape=jax.ShapeDtypeStruct((rows, HALF), jnp.uint32),
        scratch_shapes=[pltpu.SemaphoreType.DMA((TOP_K,))],
        input_output_aliases={2: 0},
        compiler_params=pltpu.CompilerParams(
            dimension_semantics=("arbitrary",), vmem_limit_bytes=VMEM_LIMIT),
        name="dispatch",
    )(dest, t_packed, buf0)


def _expert_kernel(be_ref, xp_ref, wgu_ref, bgu_ref, wd_ref, bd_ref, y_ref):
    del be_ref
    w = xp_ref[...]
    x_lo = pltpu.bitcast(w << 16, F32).astype(BF16)
    x_hi = pltpu.bitcast(w & jnp.uint32(0xFFFF0000), F32).astype(BF16)
    hgu = _dot(x_lo, wgu_ref[:HALF, :]) + _dot(x_hi, wgu_ref[HALF:, :]) + bgu_ref[...]
    gate = jnp.minimum(hgu[:, :D_FF], SWIGLU_LIMIT)
    up = jnp.clip(hgu[:, D_FF:], -SWIGLU_LIMIT, SWIGLU_LIMIT)
    act = gate * jax.nn.sigmoid(SWIGLU_ALPHA * gate)
    y_ref[...] = _dot(((up + 1.0) * act).astype(BF16), wd_ref[...]) + bd_ref[...]


def _experts(block_expert, buf, w_gu_b, b_gu, w_d_b, b_d):
    rows = buf.shape[0]
    n_blocks = rows // MOE_BLOCK
    grid_spec = pltpu.PrefetchScalarGridSpec(
        num_scalar_prefetch=1,
        grid=(n_blocks,),
        in_specs=[
            pl.BlockSpec((MOE_BLOCK, HALF), lambda i, be: (i, 0)),
            pl.BlockSpec((None, D_MODEL, 2 * D_FF), lambda i, be: (be[i], 0, 0)),
            pl.BlockSpec((None, 1, 2 * D_FF), lambda i, be: (be[i], 0, 0)),
            pl.BlockSpec((None, D_FF, D_MODEL), lambda i, be: (be[i], 0, 0)),
            pl.BlockSpec((None, 1, D_MODEL), lambda i, be: (be[i], 0, 0)),
        ],
        out_specs=pl.BlockSpec((MOE_BLOCK, D_MODEL), lambda i, be: (i, 0)),
    )
    return pl.pallas_call(
        _expert_kernel,
        grid_spec=grid_spec,
        out_shape=jax.ShapeDtypeStruct((rows, D_MODEL), F32),
        compiler_params=pltpu.CompilerParams(
            dimension_semantics=("arbitrary",), vmem_limit_bytes=VMEM_LIMIT),
        name="experts",
    )(block_expert, buf, w_gu_b, b_gu, w_d_b, b_d)


def _combine_kernel(dest_ref, gate_ref, h2_ref, fw_ref, y_hbm_ref, out_ref, ybuf_ref, sem):
    tt = h2_ref.shape[0]

    def issue(t, carry):
        for kk in range(TOP_K):
            _row_copy(y_hbm_ref, dest_ref[kk, t], ybuf_ref.at[kk], t, sem.at[kk]).start()
        return carry

    lax.fori_loop(0, tt, issue, 0)

    def drain(t, carry):
        for kk in range(TOP_K):
            _row_copy(y_hbm_ref, dest_ref[kk, t], ybuf_ref.at[kk], t, sem.at[kk]).wait()
        return carry

    lax.fori_loop(0, tt, drain, 0)

    gpad = jnp.concatenate([gate_ref[...], jnp.zeros((128 - 8, tt), F32)], axis=0)
    gcol = gpad.T
    acc = h2_ref[...]
    for kk in range(TOP_K):
        acc = acc + gcol[:, kk:kk + 1] * ybuf_ref[kk]
    out_ref[...] = acc * lax.rsqrt(jnp.mean(acc * acc, axis=-1, keepdims=True) + EPS) * fw_ref[...]


def _combine(dest, gates, h2, final_norm_w, y_buf):
    n = h2.shape[0]
    tt = TOKEN_TILE
    return pl.pallas_call(
        _combine_kernel,
        grid=(n // tt,),
        in_specs=[
            pl.BlockSpec((TOP_K, tt), lambda i: (0, i), memory_space=pltpu.SMEM),
            pl.BlockSpec((8, tt), lambda i: (0, i)),
            pl.BlockSpec((tt, D_MODEL), lambda i: (i, 0)),
            pl.BlockSpec((1, D_MODEL), lambda i: (0, 0)),
            pl.BlockSpec(memory_space=pl.ANY),
        ],
        out_specs=pl.BlockSpec((tt, D_MODEL), lambda i: (i, 0)),
        out_shape=jax.ShapeDtypeStruct((n, D_MODEL), F32),
        scratch_shapes=[
            pltpu.VMEM((TOP_K, tt, D_MODEL), F32),
            pltpu.SemaphoreType.DMA((TOP_K,)),
        ],
        compiler_params=pltpu.CompilerParams(
            dimension_semantics=("arbitrary",), vmem_limit_bytes=VMEM_LIMIT),
        name="combine",
    )(dest, gates, h2, final_norm_w, y_buf)


def _pad_lanes(v, width):
    return jnp.pad(v, ((0, 0), (0, width - v.shape[1])))


def kernel(x, meta_tokens, mix_norm_w, w_in, conv_w, conv_b, dt_bias, a_log, d_skip, ssd_norm_w,
           w_decay_up, b_decay, gla_norm_w, w_out, ffn_norm_w, w_router, b_router, w_gate_up,
           b_gate_up, w_down, b_down, final_norm_w):
    batch, seq, d = x.shape
    assert d == D_MODEL and seq % TOKEN_TILE == 0
    assert mix_norm_w.shape[0] == 1, "single-layer block"
    nchunks = (FRONT_PAD + N_META + seq) // CHUNK

    n = batch * seq
    x_flat = x.reshape(n, D_MODEL)
    meta_tile = jnp.concatenate(
        [jnp.zeros((TOKEN_TILE - N_META, D_MODEL), x.dtype), meta_tokens.astype(x.dtype)], axis=0)

    wi = w_in[0]
    o_z, o_xbc = 0, SSD_WIDTH
    o_dt = o_xbc + XBC_WIDTH
    o_q = o_dt + SSD_HEADS
    o_k = o_q + GLA_K_WIDTH
    o_v = o_k + GLA_K_WIDTH
    o_g = o_v + GLA_V_WIDTH
    o_a = o_g + GLA_V_WIDTH
    w_misc = jnp.concatenate(
        [wi[:, o_dt:o_dt + SSD_HEADS], wi[:, o_a:o_a + GLA_RANK],
         jnp.zeros((D_MODEL, MISC_WIDTH - SSD_HEADS - GLA_RANK), wi.dtype)], axis=1)
    w_in_r = jnp.concatenate(
        [wi[:, o_z:o_dt], wi[:, o_q:o_a], w_misc], axis=1).astype(BF16)
    z, xbc, q, k, v, g, misc = _in_proj(x_flat, meta_tile, mix_norm_w[0][None, :], w_in_r)

    dtb = _pad_lanes(dt_bias[0][None, :].astype(F32), MISC_WIDTH)
    aneg = _pad_lanes(-jnp.exp(a_log[0].astype(F32))[None, :], MISC_WIDTH)
    dskip = jnp.repeat(d_skip[0].astype(F32), SSD_HEAD_DIM)[None, :]
    wdec = jnp.zeros((MISC_WIDTH, GLA_K_WIDTH), F32).at[SSD_HEADS:SSD_HEADS + GLA_RANK].set(w_decay_up[0])
    params = (conv_w[0], conv_b[0][None, :], dtb, aneg, dskip, ssd_norm_w[0][None, :],
              wdec.astype(BF16), b_decay[0][None, :], gla_norm_w[0][None, :])
    mixed = _mixer(z, xbc, q, k, v, g, misc, params, batch, nchunks)

    wr_hi, wr_lo = _split2(w_router[0].T.astype(F32))
    h2, t_packed, top_idx, gates, rank, counts = _post(
        x_flat, mixed, w_out[0].astype(BF16), ffn_norm_w[0][None, :],
        jnp.stack([wr_hi, wr_lo]), b_router[0][:, None])

    counts = counts[:, 0]
    padded = (counts + MOE_BLOCK - 1) // MOE_BLOCK * MOE_BLOCK
    pend = jnp.cumsum(padded)
    pstart = pend - padded
    dest = _dest(pstart.astype(jnp.int32), top_idx, rank)
    n_blocks = -(-(n * TOP_K) // MOE_BLOCK) + N_EXPERTS
    block_pos = jnp.arange(n_blocks, dtype=jnp.int32) * MOE_BLOCK
    block_expert = jnp.minimum(
        jnp.sum((pend[None, :] <= block_pos[:, None]).astype(jnp.int32), axis=1), N_EXPERTS - 1)

    buf = _dispatch(dest, t_packed, n_blocks * MOE_BLOCK)
    y_buf = _experts(block_expert, buf, w_gate_up[0].astype(BF16), b_gate_up[0][:, None, :],
                     w_down[0].astype(BF16), b_down[0][:, None, :])
    out = _combine(dest, gates, h2, final_norm_w[None, :], y_buf)
    return out.reshape(batch, seq, D_MODEL)
```

```python
import functools

import jax
import jax.numpy as jnp
from jax import lax
from jax.experimental import pallas as pl
from jax.experimental.pallas import tpu as pltpu

F32 = jnp.float32
BF16 = jnp.bfloat16

D_MODEL = 1024
N_META = 16
EPS = 1e-5
SSD_HEAD_DIM = 64
SSD_HEADS = 16
SSD_GROUPS = 2
SSD_STATE = 128
SSD_CONV = 4
SSD_WIDTH = SSD_HEADS * SSD_HEAD_DIM
XBC_WIDTH = SSD_WIDTH + 2 * SSD_GROUPS * SSD_STATE
GLA_HEADS = 4
GLA_KEY_DIM = 128
GLA_VAL_DIM = 256
GLA_K_WIDTH = GLA_HEADS * GLA_KEY_DIM
GLA_V_WIDTH = GLA_HEADS * GLA_VAL_DIM
GLA_RANK = 16
GLA_TAU = 16.0
GLA_SUB = 16
N_EXPERTS = 32
TOP_K = 4
D_FF = D_MODEL
SWIGLU_LIMIT = 7.0
SWIGLU_ALPHA = 1.702
MOE_BLOCK = 256

CHUNK = 128
FRONT_PAD = CHUNK - N_META
MISC_WIDTH = 128
IN_PAD_WIDTH = SSD_WIDTH + XBC_WIDTH + 2 * GLA_K_WIDTH + 2 * GLA_V_WIDTH + MISC_WIDTH
TOKEN_TILE = 256
HALF = D_MODEL // 2
NO_ROW = 1 << 12
SEG_ALIGN = 8
SORT_ROWS = -(-(TOP_K * TOKEN_TILE + N_EXPERTS * (SEG_ALIGN - 1)) // TOKEN_TILE) * TOKEN_TILE
VMEM_LIMIT = 56 * 1024 * 1024


def _split2(x):
    hi = x.astype(BF16)
    lo = (x - hi.astype(F32)).astype(BF16)
    return hi, lo


def _split3(x):
    hi = x.astype(BF16)
    r = x - hi.astype(F32)
    mid = r.astype(BF16)
    lo = (r - mid.astype(F32)).astype(BF16)
    return hi, mid, lo


def _dot(a, b):
    return jnp.dot(a, b, preferred_element_type=F32)


def _dot_nt(a, b):
    return lax.dot_general(a, b, (((1,), (1,)), ((), ())), preferred_element_type=F32)


def _dot_tn(a, b):
    return lax.dot_general(a, b, (((0,), (0,)), ((), ())), preferred_element_type=F32)


def _silu(x):
    return x * jax.nn.sigmoid(x)


_IN_SECTIONS = (SSD_WIDTH, XBC_WIDTH, GLA_K_WIDTH, GLA_K_WIDTH, GLA_V_WIDTH, GLA_V_WIDTH, MISC_WIDTH)


def _in_proj_kernel(x_ref, meta_ref, nw_ref, w_ref, z_ref, xbc_ref, q_ref, k_ref, v_ref, g_ref, misc_ref):
    h = jnp.where(pl.program_id(0) < pl.num_programs(0) - 1, x_ref[...], meta_ref[...])
    u = h * lax.rsqrt(jnp.mean(h * h, axis=-1, keepdims=True) + EPS) * nw_ref[...]
    ub = u.astype(BF16)
    outs = (z_ref, xbc_ref, q_ref, k_ref, v_ref, g_ref, misc_ref)
    off = 0
    for o_ref, width in zip(outs, _IN_SECTIONS):
        o_ref[...] = _dot(ub, w_ref[:, off:off + width]).astype(o_ref.dtype)
        off += width


def _in_proj(x_flat, meta_tile, norm_w, w_in_r):
    tm = TOKEN_TILE
    nx = x_flat.shape[0] // tm
    rows = (nx + 1) * tm
    row_spec = lambda w: pl.BlockSpec((tm, w), lambda i: (i, 0))
    out_dtypes = (BF16,) * 6 + (F32,)
    return pl.pallas_call(
        _in_proj_kernel,
        grid=(nx + 1,),
        in_specs=[
            pl.BlockSpec((tm, D_MODEL), lambda i: (jnp.minimum(i, nx - 1), 0)),
            pl.BlockSpec((tm, D_MODEL), lambda i: (0, 0)),
            pl.BlockSpec((1, D_MODEL), lambda i: (0, 0)),
            pl.BlockSpec((D_MODEL, IN_PAD_WIDTH), lambda i: (0, 0)),
        ],
        out_specs=[row_spec(w) for w in _IN_SECTIONS],
        out_shape=[jax.ShapeDtypeStruct((rows, w), dt) for w, dt in zip(_IN_SECTIONS, out_dtypes)],
        compiler_params=pltpu.CompilerParams(
            dimension_semantics=("arbitrary",), vmem_limit_bytes=VMEM_LIMIT),
        name="in_proj",
    )(x_flat, meta_tile, norm_w, w_in_r)


def _block_rows(x, rows, width):
    parts = []
    for r in rows:
        if r is None:
            parts.append(jnp.zeros((width, x.shape[1]), x.dtype))
        else:
            parts.append(jnp.broadcast_to(x[r:r + 1, :], (width, x.shape[1])))
    return jnp.concatenate(parts, axis=0)


def _mixer_kernel(z_ref, xbc_ref, q_ref, k_ref, v_ref, g_ref, misc_ref,
                  convw_ref, convb_ref, dtb_ref, aneg_ref, dskip_ref, ssdw_ref,
                  wdec_ref, bdec_ref, glaw_ref,
                  out_ref,
                  xext_ref, sstate_ref, gstate_ref):
    c = pl.program_id(1)

    @pl.when(c == 0)
    def _():
        xext_ref[0:8, :] = jnp.zeros((8, XBC_WIDTH), F32)
        sstate_ref[...] = jnp.zeros_like(sstate_ref)
        gstate_ref[...] = jnp.zeros_like(gstate_ref)

    row = lax.broadcasted_iota(jnp.int32, (CHUNK, CHUNK), 0)
    col = lax.broadcasted_iota(jnp.int32, (CHUNK, CHUNK), 1)
    causal = row >= col
    tri = jnp.where(causal, 1.0, 0.0).astype(BF16)
    valid = jnp.logical_or(c > 0, row >= FRONT_PAD)

    xext_ref[8:8 + CHUNK, :] = xbc_ref[...].astype(F32)
    conv = convb_ref[...] + convw_ref[0:1, :] * xext_ref[5:5 + CHUNK, :]
    for kk in range(1, SSD_CONV):
        conv = conv + convw_ref[kk:kk + 1, :] * xext_ref[5 + kk:5 + kk + CHUNK, :]
    xext_ref[0:8, :] = xext_ref[CHUNK:CHUNK + 8, :]
    valid_w = jnp.logical_or(c > 0, lax.broadcasted_iota(jnp.int32, (CHUNK, XBC_WIDTH), 0) >= FRONT_PAD)
    act = jnp.where(valid_w, _silu(conv), 0.0)
    xs = act[:, :SSD_WIDTH]
    bmat = act[:, SSD_WIDTH:SSD_WIDTH + SSD_GROUPS * SSD_STATE].astype(BF16)
    cmat = act[:, SSD_WIDTH + SSD_GROUPS * SSD_STATE:].astype(BF16)

    misc = misc_ref[...]
    dt = jnp.where(valid, jax.nn.softplus(misc + dtb_ref[...]), 0.0)
    d_a = dt * aneg_ref[...]
    p0, p1, p2 = _split3(d_a)
    a_cs = _dot(tri, p0) + _dot(tri, p1) + _dot(tri, p2)
    a_cs_t = a_cs.T

    lo_half = col < SSD_HEAD_DIM
    hg = SSD_HEADS // SSD_GROUPS
    pairs_per_group = hg // 2
    y_parts = []
    for gi in range(SSD_GROUPS):
        b_g = bmat[:, gi * SSD_STATE:(gi + 1) * SSD_STATE]
        c_g = cmat[:, gi * SSD_STATE:(gi + 1) * SSD_STATE]
        c_g32 = c_g.astype(F32)
        cb = _dot_nt(c_g, b_g)
        w_parts = []
        cd_parts = []
        for pj in range(pairs_per_group):
            h0 = gi * hg + 2 * pj
            h1 = h0 + 1
            lanes = slice(h0 * SSD_HEAD_DIM, (h1 + 1) * SSD_HEAD_DIM)
            ab0 = jnp.broadcast_to(a_cs[:, h0:h0 + 1], (CHUNK, CHUNK))
            ab1 = jnp.broadcast_to(a_cs[:, h1:h1 + 1], (CHUNK, CHUNK))
            db0 = jnp.broadcast_to(dt[:, h0:h0 + 1], (CHUNK, CHUNK))
            db1 = jnp.broadcast_to(dt[:, h1:h1 + 1], (CHUNK, CHUNK))
            l0 = jnp.exp(jnp.where(causal, ab0 - a_cs_t[h0:h0 + 1, :], -jnp.inf))
            l1 = jnp.exp(jnp.where(causal, ab1 - a_cs_t[h1:h1 + 1, :], -jnp.inf))
            lhs = jnp.concatenate(
                [(cb * l0).astype(BF16), (c_g32 * jnp.exp(ab0)).astype(BF16),
                 (cb * l1).astype(BF16), (c_g32 * jnp.exp(ab1)).astype(BF16)], axis=1)
            xs_p = xs[:, lanes]
            dtx = xs_p * jnp.where(lo_half, db0, db1)
            st = sstate_ref[gi, :, pj * CHUNK:(pj + 1) * CHUNK]
            rhs = jnp.concatenate(
                [jnp.where(lo_half, dtx, 0.0).astype(BF16), jnp.where(lo_half, st, 0.0).astype(BF16),
                 jnp.where(lo_half, 0.0, dtx).astype(BF16), jnp.where(lo_half, 0.0, st).astype(BF16)],
                axis=0)
            y_p = _dot(lhs, rhs) + dskip_ref[:, lanes] * xs_p
            y_parts.append(y_p)
            ae = jnp.where(lo_half, ab0, ab1)
            a_last = ae[CHUNK - 1:CHUNK, :]
            w_parts.append((dtx * jnp.exp(a_last - ae)).astype(BF16))
            cd_parts.append(jnp.exp(a_last))
        upd = _dot_tn(b_g, jnp.concatenate(w_parts, axis=1))
        sstate_ref[gi] = jnp.concatenate(cd_parts, axis=1) * sstate_ref[gi] + upd
    y = jnp.concatenate(y_parts, axis=1)
    y = y * _silu(z_ref[...].astype(F32))
    gsz = SSD_WIDTH // SSD_GROUPS
    for gi in range(SSD_GROUPS):
        yg = y[:, gi * gsz:(gi + 1) * gsz]
        yn = yg * lax.rsqrt(jnp.mean(yg * yg, axis=-1, keepdims=True) + EPS)
        out_ref[:, gi * gsz:(gi + 1) * gsz] = (yn * ssdw_ref[:, gi * gsz:(gi + 1) * gsz]).astype(out_ref.dtype)

    m0, m1 = _split2(misc)
    pre = _dot(m0, wdec_ref[...]) + _dot(m1, wdec_ref[...]) + bdec_ref[...]
    log_a = jax.nn.log_sigmoid(pre) * (1.0 / GLA_TAU)
    g0, g1 = _split2(log_a)
    gcum = _dot(tri, g0) + _dot(tri, g1)
    roww = lax.broadcasted_iota(jnp.int32, (CHUNK, GLA_K_WIDTH), 0)
    qf = q_ref[...].astype(F32) * (GLA_KEY_DIM ** -0.5)
    kf = k_ref[...].astype(F32)
    nsub = CHUNK // GLA_SUB
    g_start = _block_rows(gcum, [None] + [GLA_SUB * i - 1 for i in range(1, nsub)], GLA_SUB)
    e0 = gcum - g_start
    q_lv = [(qf * jnp.exp(e0)).astype(BF16)]
    k_lv = [(kf * jnp.exp(-e0)).astype(BF16)]
    shifts = (4, 5, 6)
    for sh in shifts:
        b = 1 << sh
        bound = _block_rows(gcum, [2 * b * i + b - 1 for i in range(CHUNK // (2 * b))], 2 * b)
        second = ((roww >> sh) & 1) == 1
        x_l = jnp.exp(jnp.where(second, gcum - bound, bound - gcum))
        q_lv.append((qf * x_l).astype(BF16))
        k_lv.append((kf * x_l).astype(BF16))
    masks = [jnp.logical_and((row >> 4) == (col >> 4), causal)]
    for sh in shifts:
        same = (row >> (sh + 1)) == (col >> (sh + 1))
        m = jnp.logical_and(same, jnp.logical_and(((row >> sh) & 1) == 1, ((col >> sh) & 1) == 0))
        masks.append(m)
    g_last = gcum[CHUNK - 1:CHUNK, :]
    q_in = (qf * jnp.exp(gcum)).astype(BF16)
    k_end = (kf * jnp.exp(g_last - gcum)).astype(BF16)
    dec = jnp.exp(g_last)
    vb = v_ref[...]
    for hh in range(GLA_HEADS):
        kl = slice(hh * GLA_KEY_DIM, (hh + 1) * GLA_KEY_DIM)
        vl = slice(hh * GLA_VAL_DIM, (hh + 1) * GLA_VAL_DIM)
        scores = jnp.zeros((CHUNK, CHUNK), F32)
        for lv in range(len(masks)):
            scores = scores + jnp.where(masks[lv], _dot_nt(q_lv[lv][:, kl], k_lv[lv][:, kl]), 0.0)
        v_h = vb[:, vl]
        s_t = gstate_ref[hh]
        o = _dot(scores.astype(BF16), v_h) + _dot_nt(q_in[:, kl], s_t.astype(BF16))
        gstate_ref[hh] = dec[:, kl] * s_t + _dot_tn(v_h, k_end[:, kl])
        o = o * lax.rsqrt(jnp.mean(o * o, axis=-1, keepdims=True) + EPS) * glaw_ref[...]
        o = o * _silu(g_ref[:, vl].astype(F32))
        out_ref[:, SSD_WIDTH + hh * GLA_VAL_DIM:SSD_WIDTH + (hh + 1) * GLA_VAL_DIM] = o.astype(out_ref.dtype)


def _mixer(z, xbc, q, k, v, g, misc, params, batch, nchunks):
    widths = (SSD_WIDTH, XBC_WIDTH, GLA_K_WIDTH, GLA_K_WIDTH, GLA_V_WIDTH, GLA_V_WIDTH, MISC_WIDTH)
    meta_block = z.shape[0] // CHUNK - 1
    row_spec = lambda w: pl.BlockSpec(
        (CHUNK, w), lambda b, c: (jnp.where(c == 0, meta_block, b * (nchunks - 1) + c - 1), 0))
    par_spec = lambda p: pl.BlockSpec(p.shape, lambda b, c: (0,) * p.ndim)
    n_out = batch * (nchunks - 1) * CHUNK
    return pl.pallas_call(
        _mixer_kernel,
        grid=(batch, nchunks),
        in_specs=[row_spec(w) for w in widths] + [par_spec(p) for p in params],
        out_specs=pl.BlockSpec((CHUNK, SSD_WIDTH + GLA_V_WIDTH),
                               lambda b, c: (b * (nchunks - 1) + jnp.maximum(c - 1, 0), 0)),
        out_shape=jax.ShapeDtypeStruct((n_out, SSD_WIDTH + GLA_V_WIDTH), BF16),
        scratch_shapes=[
            pltpu.VMEM((CHUNK + 8, XBC_WIDTH), F32),
            pltpu.VMEM((SSD_GROUPS, SSD_STATE, SSD_WIDTH // SSD_GROUPS), F32),
            pltpu.VMEM((GLA_HEADS, GLA_VAL_DIM, GLA_KEY_DIM), F32),
        ],
        compiler_params=pltpu.CompilerParams(
            dimension_semantics=("arbitrary", "arbitrary"), vmem_limit_bytes=VMEM_LIMIT),
        name="mixer",
    )(z, xbc, q, k, v, g, misc, *params)


def _post_kernel(x_ref, mixed_ref, wout_ref, fnw_ref, wrt_ref, br_ref,
                 h2_ref, t_ref, idx_ref, gate_ref, lrank_ref, cnt_ref):
    h2 = x_ref[...] + _dot(mixed_ref[...], wout_ref[...])
    h2_ref[...] = h2
    t = h2 * lax.rsqrt(jnp.mean(h2 * h2, axis=-1, keepdims=True) + EPS) * fnw_ref[...]
    t_ref[...] = t.astype(BF16)

    t0, t1 = _split2(t)
    w0 = wrt_ref[0]
    w1 = wrt_ref[1]
    logits = _dot_nt(w0, t0) + _dot_nt(w0, t1) + _dot_nt(w1, t0) + br_ref[...]
    e_iota = lax.broadcasted_iota(jnp.int32, logits.shape, 0)
    work = logits
    sel_any = jnp.zeros(logits.shape, F32)
    tops = []
    idxs = []
    onehots = []
    for _ in range(TOP_K):
        m = jnp.max(work, axis=0, keepdims=True)
        idx = jnp.min(jnp.where(work == m, e_iota, N_EXPERTS), axis=0, keepdims=True)
        hit = e_iota == idx
        tops.append(m)
        idxs.append(idx)
        onehots.append(hit)
        sel_any = sel_any + jnp.where(hit, 1.0, 0.0)
        work = jnp.where(hit, -jnp.inf, work)
    exps = [jnp.exp(tv - tops[0]) for tv in tops]
    denom = exps[0] + exps[1] + exps[2] + exps[3]
    tile = logits.shape[1]
    pad_rows = 8 - TOP_K
    gate_ref[...] = jnp.concatenate([e / denom for e in exps] + [jnp.zeros((pad_rows, tile), F32)], axis=0)
    idx_ref[...] = jnp.concatenate(idxs + [jnp.zeros((pad_rows, tile), jnp.int32)], axis=0)

    r_i = lax.broadcasted_iota(jnp.int32, (tile, tile), 0)
    c_i = lax.broadcasted_iota(jnp.int32, (tile, tile), 1)
    upper = jnp.where(r_i < c_i, 1.0, 0.0).astype(BF16)
    before = _dot(sel_any.astype(BF16), upper)
    ranks = [jnp.sum(jnp.where(h, before, 0.0), axis=0, keepdims=True) for h in onehots]
    ranks.append(jnp.full((pad_rows, tile), float(NO_ROW), F32))
    lrank_ref[...] = jnp.concatenate(ranks, axis=0).astype(jnp.int32)
    cnt = jnp.sum(sel_any, axis=1, keepdims=True)
    cnt_ref[...] = jnp.broadcast_to(cnt, cnt_ref.shape).astype(jnp.int32)


def _post(x_flat, mixed, w_out_b, ffn_norm_w, w_router_t, b_router_col):
    n = x_flat.shape[0]
    tt = TOKEN_TILE
    const = lambda shape: pl.BlockSpec(shape, lambda i: (0,) * len(shape))
    return pl.pallas_call(
        _post_kernel,
        grid=(n // tt,),
        in_specs=[
            pl.BlockSpec((tt, D_MODEL), lambda i: (i, 0)),
            pl.BlockSpec((tt, SSD_WIDTH + GLA_V_WIDTH), lambda i: (i, 0)),
            const(w_out_b.shape),
            const((1, D_MODEL)),
            const(w_router_t.shape),
            const((N_EXPERTS, 1)),
        ],
        out_specs=[
            pl.BlockSpec((tt, D_MODEL), lambda i: (i, 0)),
            pl.BlockSpec((tt, D_MODEL), lambda i: (i, 0)),
            pl.BlockSpec((8, tt), lambda i: (0, i)),
            pl.BlockSpec((8, tt), lambda i: (0, i)),
            pl.BlockSpec((8, tt), lambda i: (0, i)),
            pl.BlockSpec((N_EXPERTS, 128), lambda i: (i, 0)),
        ],
        out_shape=[
            jax.ShapeDtypeStruct((n, D_MODEL), F32),
            jax.ShapeDtypeStruct((n, D_MODEL), BF16),
            jax.ShapeDtypeStruct((8, n), jnp.int32),
            jax.ShapeDtypeStruct((8, n), F32),
            jax.ShapeDtypeStruct((8, n), jnp.int32),
            jax.ShapeDtypeStruct((n // tt * N_EXPERTS, 128), jnp.int32),
        ],
        compiler_params=pltpu.CompilerParams(
            dimension_semantics=("parallel",), vmem_limit_bytes=VMEM_LIMIT),
        name="post",
    )(x_flat, mixed, w_out_b, ffn_norm_w, w_router_t, b_router_col)


def _pack_pairs(x):
    lo = pltpu.bitcast(x[:, :HALF], jnp.uint32) >> 16
    hi = pltpu.bitcast(x[:, HALF:], jnp.uint32) & jnp.uint32(0xFFFF0000)
    return lo | hi


def _unpack_pairs(w):
    lo = pltpu.bitcast(w << 16, F32).astype(BF16)
    hi = pltpu.bitcast(w & jnp.uint32(0xFFFF0000), F32).astype(BF16)
    return lo, hi


def _sorted_positions(idx, lrank, offs_ref, base):
    pos = lrank
    for e in range(N_EXPERTS):
        pos = pos + jnp.where(idx == e, offs_ref[base + e], 0)
    return pos


def _segment_copies(ngrp_ref, offs_ref, dstart_ref, base, make_copy, wait):
    def per_expert(e, carry):
        o = offs_ref[base + e]
        d = dstart_ref[base + e]

        def group(j, carry2):
            cp = make_copy(pl.multiple_of(o + SEG_ALIGN * j, SEG_ALIGN),
                           pl.multiple_of(d + SEG_ALIGN * j, SEG_ALIGN))
            if wait:
                cp.wait()
            else:
                cp.start()
            return carry2

        lax.fori_loop(0, ngrp_ref[base + e], group, 0)
        return carry

    lax.fori_loop(0, N_EXPERTS, per_expert, 0)


def _dispatch_kernel(ngrp_ref, offs_ref, dstart_ref, idx_ref, lrank_ref, t_ref, buf_in_ref, buf_ref,
                     sorted_ref, sem):
    del buf_in_ref
    base = pl.program_id(0) * N_EXPERTS
    tt = t_ref.shape[0]
    pos = _sorted_positions(idx_ref[...], lrank_ref[...], offs_ref, base)
    t = t_ref[...]
    for r0 in range(0, SORT_ROWS, tt):
        prow = lax.broadcasted_iota(jnp.int32, (tt, tt), 0) + r0
        perm = jnp.zeros((tt, tt), F32)
        for kk in range(TOP_K):
            perm = jnp.where(prow == pos[kk:kk + 1, :], 1.0, perm)
        sorted_ref[r0:r0 + tt, :] = _pack_pairs(_dot(perm.astype(BF16), t))

    def make_copy(local, glob):
        return pltpu.make_async_copy(
            sorted_ref.at[pl.ds(local, SEG_ALIGN), :], buf_ref.at[pl.ds(glob, SEG_ALIGN), :], sem)

    _segment_copies(ngrp_ref, offs_ref, dstart_ref, base, make_copy, wait=False)
    _segment_copies(ngrp_ref, offs_ref, dstart_ref, base, make_copy, wait=True)


def _dispatch(seg, top_idx, lrank, t_b, rows):
    n = t_b.shape[0]
    tt = TOKEN_TILE
    buf0 = jnp.zeros((rows, HALF), jnp.uint32)
    grid_spec = pltpu.PrefetchScalarGridSpec(
        num_scalar_prefetch=3,
        grid=(n // tt,),
        in_specs=[
            pl.BlockSpec((8, tt), lambda i, *_: (0, i)),
            pl.BlockSpec((8, tt), lambda i, *_: (0, i)),
            pl.BlockSpec((tt, D_MODEL), lambda i, *_: (i, 0)),
            pl.BlockSpec(memory_space=pl.ANY),
        ],
        out_specs=pl.BlockSpec(memory_space=pl.ANY),
        scratch_shapes=[pltpu.VMEM((SORT_ROWS, HALF), jnp.uint32), pltpu.SemaphoreType.DMA(())],
    )
    return pl.pallas_call(
        _dispatch_kernel,
        grid_spec=grid_spec,
        out_shape=jax.ShapeDtypeStruct((rows, HALF), jnp.uint32),
        input_output_aliases={6: 0},
        compiler_params=pltpu.CompilerParams(
            dimension_semantics=("arbitrary",), vmem_limit_bytes=VMEM_LIMIT),
        name="dispatch",
    )(*seg, top_idx, lrank, t_b, buf0)


def _expert_kernel(be_ref, nused_ref, xp_ref, wgu_ref, bgu_ref, wd_ref, bd_ref, y_ref):
    del be_ref

    @pl.when(pl.program_id(0) >= nused_ref[0])
    def _():
        y_ref[...] = jnp.zeros_like(y_ref)

    @pl.when(pl.program_id(0) < nused_ref[0])
    def _():
        x_lo, x_hi = _unpack_pairs(xp_ref[...])
        hgu = _dot(x_lo, wgu_ref[:HALF, :]) + _dot(x_hi, wgu_ref[HALF:, :]) + bgu_ref[...]
        gate = jnp.minimum(hgu[:, :D_FF], SWIGLU_LIMIT)
        up = jnp.clip(hgu[:, D_FF:], -SWIGLU_LIMIT, SWIGLU_LIMIT)
        act = gate * jax.nn.sigmoid(SWIGLU_ALPHA * gate)
        y = _dot(((up + 1.0) * act).astype(BF16), wd_ref[...]) + bd_ref[...]
        y_ref[...] = _pack_pairs(y.astype(BF16).astype(F32))


def _experts(block_expert, n_used, buf, w_gu_b, b_gu, w_d_b, b_d):
    rows = buf.shape[0]
    n_blocks = rows // MOE_BLOCK
    row_map = lambda i, be, nu: (jnp.minimum(i, nu[0] - 1), 0)
    w_map = lambda i, be, nu: (be[jnp.minimum(i, nu[0] - 1)], 0, 0)
    grid_spec = pltpu.PrefetchScalarGridSpec(
        num_scalar_prefetch=2,
        grid=(n_blocks,),
        in_specs=[
            pl.BlockSpec((MOE_BLOCK, HALF), row_map),
            pl.BlockSpec((None, D_MODEL, 2 * D_FF), w_map),
            pl.BlockSpec((None, 1, 2 * D_FF), w_map),
            pl.BlockSpec((None, D_FF, D_MODEL), w_map),
            pl.BlockSpec((None, 1, D_MODEL), w_map),
        ],
        out_specs=pl.BlockSpec((MOE_BLOCK, HALF), lambda i, be, nu: (i, 0)),
    )
    return pl.pallas_call(
        _expert_kernel,
        grid_spec=grid_spec,
        out_shape=jax.ShapeDtypeStruct((rows, HALF), jnp.uint32),
        compiler_params=pltpu.CompilerParams(
            dimension_semantics=("arbitrary",), vmem_limit_bytes=VMEM_LIMIT),
        name="experts",
    )(block_expert, n_used, buf, w_gu_b, b_gu, w_d_b, b_d)


def _combine_kernel(ngrp_ref, offs_ref, dstart_ref, idx_ref, lrank_ref, gate_ref, h2_ref, fw_ref,
                    y_hbm_ref, out_ref, ys_ref, sem):
    base = pl.program_id(0) * N_EXPERTS
    tt = h2_ref.shape[0]

    @pl.when(pl.program_id(0) == 0)
    def _():
        ys_ref[...] = jnp.zeros_like(ys_ref)

    def make_copy(local, glob):
        return pltpu.make_async_copy(
            y_hbm_ref.at[pl.ds(glob, SEG_ALIGN), :], ys_ref.at[pl.ds(local, SEG_ALIGN), :], sem)

    _segment_copies(ngrp_ref, offs_ref, dstart_ref, base, make_copy, wait=False)

    pos = _sorted_positions(idx_ref[...], lrank_ref[...], offs_ref, base).astype(F32)
    zpad = jnp.zeros((128 - 8, tt), F32)
    pos_c = jnp.concatenate([pos, zpad], axis=0).T
    gate_c = jnp.concatenate([gate_ref[...], zpad], axis=0).T
    lane = lax.broadcasted_iota(jnp.int32, (tt, SORT_ROWS), 1).astype(F32)
    pg = jnp.zeros((tt, SORT_ROWS), F32)
    for kk in range(TOP_K):
        pg = jnp.where(lane == pos_c[:, kk:kk + 1], gate_c[:, kk:kk + 1], pg)
    p_hi, p_lo = _split2(pg)

    _segment_copies(ngrp_ref, offs_ref, dstart_ref, base, make_copy, wait=True)
    y_lo, y_hi = _unpack_pairs(ys_ref[...])
    ffn = jnp.concatenate(
        [_dot(p_hi, y_lo) + _dot(p_lo, y_lo), _dot(p_hi, y_hi) + _dot(p_lo, y_hi)], axis=1)
    acc = h2_ref[...] + ffn
    out_ref[...] = acc * lax.rsqrt(jnp.mean(acc * acc, axis=-1, keepdims=True) + EPS) * fw_ref[...]


def _combine(seg, top_idx, lrank, gates, h2, final_norm_w, y_buf):
    n = h2.shape[0]
    tt = TOKEN_TILE
    grid_spec = pltpu.PrefetchScalarGridSpec(
        num_scalar_prefetch=3,
        grid=(n // tt,),
        in_specs=[
            pl.BlockSpec((8, tt), lambda i, *_: (0, i)),
            pl.BlockSpec((8, tt), lambda i, *_: (0, i)),
            pl.BlockSpec((8, tt), lambda i, *_: (0, i)),
            pl.BlockSpec((tt, D_MODEL), lambda i, *_: (i, 0)),
            pl.BlockSpec((1, D_MODEL), lambda i, *_: (0, 0)),
            pl.BlockSpec(memory_space=pl.ANY),
        ],
        out_specs=pl.BlockSpec((tt, D_MODEL), lambda i, *_: (i, 0)),
        scratch_shapes=[pltpu.VMEM((SORT_ROWS, HALF), jnp.uint32), pltpu.SemaphoreType.DMA(())],
    )
    return pl.pallas_call(
        _combine_kernel,
        grid_spec=grid_spec,
        out_shape=jax.ShapeDtypeStruct((n, D_MODEL), F32),
        compiler_params=pltpu.CompilerParams(
            dimension_semantics=("arbitrary",), vmem_limit_bytes=VMEM_LIMIT),
        name="combine",
    )(*seg, top_idx, lrank, gates, h2, final_norm_w, y_buf)


def _pad_lanes(v, width):
    return jnp.pad(v, ((0, 0), (0, width - v.shape[1])))


def kernel(x, meta_tokens, mix_norm_w, w_in, conv_w, conv_b, dt_bias, a_log, d_skip, ssd_norm_w,
           w_decay_up, b_decay, gla_norm_w, w_out, ffn_norm_w, w_router, b_router, w_gate_up,
           b_gate_up, w_down, b_down, final_norm_w):
    batch, seq, d = x.shape
    assert d == D_MODEL and seq % TOKEN_TILE == 0
    assert mix_norm_w.shape[0] == 1, "single-layer block"
    nchunks = (FRONT_PAD + N_META + seq) // CHUNK

    n = batch * seq
    x_flat = x.reshape(n, D_MODEL)
    meta_tile = jnp.concatenate(
        [jnp.zeros((TOKEN_TILE - N_META, D_MODEL), x.dtype), meta_tokens.astype(x.dtype)], axis=0)

    wi = w_in[0]
    o_z, o_xbc = 0, SSD_WIDTH
    o_dt = o_xbc + XBC_WIDTH
    o_q = o_dt + SSD_HEADS
    o_k = o_q + GLA_K_WIDTH
    o_v = o_k + GLA_K_WIDTH
    o_g = o_v + GLA_V_WIDTH
    o_a = o_g + GLA_V_WIDTH
    w_misc = jnp.concatenate(
        [wi[:, o_dt:o_dt + SSD_HEADS], wi[:, o_a:o_a + GLA_RANK],
         jnp.zeros((D_MODEL, MISC_WIDTH - SSD_HEADS - GLA_RANK), wi.dtype)], axis=1)
    w_in_r = jnp.concatenate(
        [wi[:, o_z:o_dt], wi[:, o_q:o_a], w_misc], axis=1).astype(BF16)
    z, xbc, q, k, v, g, misc = _in_proj(x_flat, meta_tile, mix_norm_w[0][None, :], w_in_r)

    dtb = _pad_lanes(dt_bias[0][None, :].astype(F32), MISC_WIDTH)
    aneg = _pad_lanes(-jnp.exp(a_log[0].astype(F32))[None, :], MISC_WIDTH)
    dskip = jnp.repeat(d_skip[0].astype(F32), SSD_HEAD_DIM)[None, :]
    wdec = jnp.zeros((MISC_WIDTH, GLA_K_WIDTH), F32).at[SSD_HEADS:SSD_HEADS + GLA_RANK].set(w_decay_up[0])
    params = (conv_w[0], conv_b[0][None, :], dtb, aneg, dskip, ssd_norm_w[0][None, :],
              wdec.astype(BF16), b_decay[0][None, :], gla_norm_w[0][None, :])
    mixed = _mixer(z, xbc, q, k, v, g, misc, params, batch, nchunks)

    wr_hi, wr_lo = _split2(w_router[0].T.astype(F32))
    h2, t_b, top_idx, gates, lrank, tile_cnt = _post(
        x_flat, mixed, w_out[0].astype(BF16), ffn_norm_w[0][None, :],
        jnp.stack([wr_hi, wr_lo]), b_router[0][:, None])

    n_tiles = n // TOKEN_TILE
    tile_cnt = tile_cnt.reshape(n_tiles, N_EXPERTS, 128)[:, :, 0]
    seg_rows = (tile_cnt + SEG_ALIGN - 1) // SEG_ALIGN * SEG_ALIGN
    counts = jnp.sum(seg_rows, axis=0)
    padded = (counts + MOE_BLOCK - 1) // MOE_BLOCK * MOE_BLOCK
    pend = jnp.cumsum(padded)
    pstart = pend - padded
    dstart = pstart[None, :] + jnp.cumsum(seg_rows, axis=0) - seg_rows
    offs = jnp.cumsum(seg_rows, axis=1) - seg_rows
    seg = tuple(a.reshape(-1).astype(jnp.int32) for a in (seg_rows // SEG_ALIGN, offs, dstart))
    n_blocks = -(-(n * TOP_K + n_tiles * N_EXPERTS * (SEG_ALIGN - 1)) // MOE_BLOCK) + N_EXPERTS
    block_pos = jnp.arange(n_blocks, dtype=jnp.int32) * MOE_BLOCK
    block_expert = jnp.minimum(
        jnp.sum((pend[None, :] <= block_pos[:, None]).astype(jnp.int32), axis=1), N_EXPERTS - 1)
    n_used = (pend[-1:] // MOE_BLOCK).astype(jnp.int32)

    buf = _dispatch(seg, top_idx, lrank, t_b, n_blocks * MOE_BLOCK)
    y_buf = _experts(block_expert, n_used, buf, w_gate_up[0].astype(BF16), b_gate_up[0][:, None, :],
                     w_down[0].astype(BF16), b_down[0][:, None, :])
    out = _combine(seg, top_idx, lrank, gates, h2, final_norm_w[None, :], y_buf)
    return out.reshape(batch, seq, D_MODEL)
```

```python
import functools

import jax
import jax.numpy as jnp
from jax import lax
from jax.experimental import pallas as pl
from jax.experimental.pallas import tpu as pltpu

F32 = jnp.float32
BF16 = jnp.bfloat16

D_MODEL = 1024
N_META = 16
EPS = 1e-5
SSD_HEAD_DIM = 64
SSD_HEADS = 16
SSD_GROUPS = 2
SSD_STATE = 128
SSD_CONV = 4
SSD_WIDTH = SSD_HEADS * SSD_HEAD_DIM
XBC_WIDTH = SSD_WIDTH + 2 * SSD_GROUPS * SSD_STATE
GLA_HEADS = 4
GLA_KEY_DIM = 128
GLA_VAL_DIM = 256
GLA_K_WIDTH = GLA_HEADS * GLA_KEY_DIM
GLA_V_WIDTH = GLA_HEADS * GLA_VAL_DIM
GLA_RANK = 16
GLA_TAU = 16.0
GLA_SUB = 16
N_EXPERTS = 32
TOP_K = 4
D_FF = D_MODEL
SWIGLU_LIMIT = 7.0
SWIGLU_ALPHA = 1.702
MOE_BLOCK = 256

CHUNK = 128
FRONT_PAD = CHUNK - N_META
MISC_WIDTH = 128
IN_PAD_WIDTH = SSD_WIDTH + XBC_WIDTH + 2 * GLA_K_WIDTH + 2 * GLA_V_WIDTH + MISC_WIDTH
TOKEN_TILE = 256
HALF = D_MODEL // 2
NO_ROW = 1 << 12
SEG_ALIGN = 8
SORT_ROWS = -(-(TOP_K * TOKEN_TILE + N_EXPERTS * (SEG_ALIGN - 1)) // TOKEN_TILE) * TOKEN_TILE
SORT_GROUPS = SORT_ROWS // SEG_ALIGN
VMEM_LIMIT = 56 * 1024 * 1024


def _split2(x):
    hi = x.astype(BF16)
    lo = (x - hi.astype(F32)).astype(BF16)
    return hi, lo


def _split3(x):
    hi = x.astype(BF16)
    r = x - hi.astype(F32)
    mid = r.astype(BF16)
    lo = (r - mid.astype(F32)).astype(BF16)
    return hi, mid, lo


def _dot(a, b):
    return jnp.dot(a, b, preferred_element_type=F32)


def _dot_nt(a, b):
    return lax.dot_general(a, b, (((1,), (1,)), ((), ())), preferred_element_type=F32)


def _dot_tn(a, b):
    return lax.dot_general(a, b, (((0,), (0,)), ((), ())), preferred_element_type=F32)


def _silu(x):
    return x * jax.nn.sigmoid(x)


_IN_SECTIONS = (SSD_WIDTH, XBC_WIDTH, GLA_K_WIDTH, GLA_K_WIDTH, GLA_V_WIDTH, GLA_V_WIDTH, MISC_WIDTH)


def _in_proj_kernel(x_ref, meta_ref, nw_ref, w_ref, z_ref, xbc_ref, q_ref, k_ref, v_ref, g_ref, misc_ref):
    h = jnp.where(pl.program_id(0) < pl.num_programs(0) - 1, x_ref[...], meta_ref[...])
    u = h * lax.rsqrt(jnp.mean(h * h, axis=-1, keepdims=True) + EPS) * nw_ref[...]
    ub = u.astype(BF16)
    outs = (z_ref, xbc_ref, q_ref, k_ref, v_ref, g_ref, misc_ref)
    off = 0
    for o_ref, width in zip(outs, _IN_SECTIONS):
        o_ref[...] = _dot(ub, w_ref[:, off:off + width]).astype(o_ref.dtype)
        off += width


def _in_proj(x_flat, meta_tile, norm_w, w_in_r):
    tm = TOKEN_TILE
    nx = x_flat.shape[0] // tm
    rows = (nx + 1) * tm
    row_spec = lambda w: pl.BlockSpec((tm, w), lambda i: (i, 0))
    out_dtypes = (BF16,) * 6 + (F32,)
    return pl.pallas_call(
        _in_proj_kernel,
        grid=(nx + 1,),
        in_specs=[
            pl.BlockSpec((tm, D_MODEL), lambda i: (jnp.minimum(i, nx - 1), 0)),
            pl.BlockSpec((tm, D_MODEL), lambda i: (0, 0)),
            pl.BlockSpec((1, D_MODEL), lambda i: (0, 0)),
            pl.BlockSpec((D_MODEL, IN_PAD_WIDTH), lambda i: (0, 0)),
        ],
        out_specs=[row_spec(w) for w in _IN_SECTIONS],
        out_shape=[jax.ShapeDtypeStruct((rows, w), dt) for w, dt in zip(_IN_SECTIONS, out_dtypes)],
        compiler_params=pltpu.CompilerParams(
            dimension_semantics=("arbitrary",), vmem_limit_bytes=VMEM_LIMIT),
        name="in_proj",
    )(x_flat, meta_tile, norm_w, w_in_r)


def _block_rows(x, rows, width):
    parts = []
    for r in rows:
        if r is None:
            parts.append(jnp.zeros((width, x.shape[1]), x.dtype))
        else:
            parts.append(jnp.broadcast_to(x[r:r + 1, :], (width, x.shape[1])))
    return jnp.concatenate(parts, axis=0)


def _mixer_kernel(z_ref, xbc_ref, q_ref, k_ref, v_ref, g_ref, misc_ref,
                  convw_ref, convb_ref, dtb_ref, aneg_ref, dskip_ref, ssdw_ref,
                  wdec_ref, bdec_ref, glaw_ref,
                  out_ref,
                  xext_ref, sstate_ref, gstate_ref):
    c = pl.program_id(1)

    @pl.when(c == 0)
    def _():
        xext_ref[0:8, :] = jnp.zeros((8, XBC_WIDTH), F32)
        sstate_ref[...] = jnp.zeros_like(sstate_ref)
        gstate_ref[...] = jnp.zeros_like(gstate_ref)

    row = lax.broadcasted_iota(jnp.int32, (CHUNK, CHUNK), 0)
    col = lax.broadcasted_iota(jnp.int32, (CHUNK, CHUNK), 1)
    causal = row >= col
    tri = jnp.where(causal, 1.0, 0.0).astype(BF16)
    valid = jnp.logical_or(c > 0, row >= FRONT_PAD)

    xext_ref[8:8 + CHUNK, :] = xbc_ref[...].astype(F32)
    conv = convb_ref[...] + convw_ref[0:1, :] * xext_ref[5:5 + CHUNK, :]
    for kk in range(1, SSD_CONV):
        conv = conv + convw_ref[kk:kk + 1, :] * xext_ref[5 + kk:5 + kk + CHUNK, :]
    xext_ref[0:8, :] = xext_ref[CHUNK:CHUNK + 8, :]
    valid_w = jnp.logical_or(c > 0, lax.broadcasted_iota(jnp.int32, (CHUNK, XBC_WIDTH), 0) >= FRONT_PAD)
    act = jnp.where(valid_w, _silu(conv), 0.0)
    xs = act[:, :SSD_WIDTH]
    bmat = act[:, SSD_WIDTH:SSD_WIDTH + SSD_GROUPS * SSD_STATE].astype(BF16)
    cmat = act[:, SSD_WIDTH + SSD_GROUPS * SSD_STATE:].astype(BF16)

    misc = misc_ref[...]
    dt = jnp.where(valid, jax.nn.softplus(misc + dtb_ref[...]), 0.0)
    d_a = dt * aneg_ref[...]
    p0, p1, p2 = _split3(d_a)
    a_cs = _dot(tri, p0) + _dot(tri, p1) + _dot(tri, p2)
    a_cs_t = a_cs.T

    lo_half = col < SSD_HEAD_DIM
    hg = SSD_HEADS // SSD_GROUPS
    pairs_per_group = hg // 2
    y_parts = []
    for gi in range(SSD_GROUPS):
        b_g = bmat[:, gi * SSD_STATE:(gi + 1) * SSD_STATE]
        c_g = cmat[:, gi * SSD_STATE:(gi + 1) * SSD_STATE]
        c_g32 = c_g.astype(F32)
        cb = _dot_nt(c_g, b_g)
        w_parts = []
        cd_parts = []
        for pj in range(pairs_per_group):
            h0 = gi * hg + 2 * pj
            h1 = h0 + 1
            lanes = slice(h0 * SSD_HEAD_DIM, (h1 + 1) * SSD_HEAD_DIM)
            ab0 = jnp.broadcast_to(a_cs[:, h0:h0 + 1], (CHUNK, CHUNK))
            ab1 = jnp.broadcast_to(a_cs[:, h1:h1 + 1], (CHUNK, CHUNK))
            db0 = jnp.broadcast_to(dt[:, h0:h0 + 1], (CHUNK, CHUNK))
            db1 = jnp.broadcast_to(dt[:, h1:h1 + 1], (CHUNK, CHUNK))
            l0 = jnp.exp(jnp.where(causal, ab0 - a_cs_t[h0:h0 + 1, :], -jnp.inf))
            l1 = jnp.exp(jnp.where(causal, ab1 - a_cs_t[h1:h1 + 1, :], -jnp.inf))
            lhs = jnp.concatenate(
                [(cb * l0).astype(BF16), (c_g32 * jnp.exp(ab0)).astype(BF16),
                 (cb * l1).astype(BF16), (c_g32 * jnp.exp(ab1)).astype(BF16)], axis=1)
            xs_p = xs[:, lanes]
            dtx = xs_p * jnp.where(lo_half, db0, db1)
            st = sstate_ref[gi, :, pj * CHUNK:(pj + 1) * CHUNK]
            rhs = jnp.concatenate(
                [jnp.where(lo_half, dtx, 0.0).astype(BF16), jnp.where(lo_half, st, 0.0).astype(BF16),
                 jnp.where(lo_half, 0.0, dtx).astype(BF16), jnp.where(lo_half, 0.0, st).astype(BF16)],
                axis=0)
            y_p = _dot(lhs, rhs) + dskip_ref[:, lanes] * xs_p
            y_parts.append(y_p)
            ae = jnp.where(lo_half, ab0, ab1)
            a_last = ae[CHUNK - 1:CHUNK, :]
            w_parts.append((dtx * jnp.exp(a_last - ae)).astype(BF16))
            cd_parts.append(jnp.exp(a_last))
        upd = _dot_tn(b_g, jnp.concatenate(w_parts, axis=1))
        sstate_ref[gi] = jnp.concatenate(cd_parts, axis=1) * sstate_ref[gi] + upd
    y = jnp.concatenate(y_parts, axis=1)
    y = y * _silu(z_ref[...].astype(F32))
    gsz = SSD_WIDTH // SSD_GROUPS
    for gi in range(SSD_GROUPS):
        yg = y[:, gi * gsz:(gi + 1) * gsz]
        yn = yg * lax.rsqrt(jnp.mean(yg * yg, axis=-1, keepdims=True) + EPS)
        out_ref[:, gi * gsz:(gi + 1) * gsz] = (yn * ssdw_ref[:, gi * gsz:(gi + 1) * gsz]).astype(out_ref.dtype)

    m0, m1 = _split2(misc)
    pre = _dot(m0, wdec_ref[...]) + _dot(m1, wdec_ref[...]) + bdec_ref[...]
    log_a = jax.nn.log_sigmoid(pre) * (1.0 / GLA_TAU)
    g0, g1 = _split2(log_a)
    gcum = _dot(tri, g0) + _dot(tri, g1)
    roww = lax.broadcasted_iota(jnp.int32, (CHUNK, GLA_K_WIDTH), 0)
    qf = q_ref[...].astype(F32) * (GLA_KEY_DIM ** -0.5)
    kf = k_ref[...].astype(F32)
    nsub = CHUNK // GLA_SUB
    g_start = _block_rows(gcum, [None] + [GLA_SUB * i - 1 for i in range(1, nsub)], GLA_SUB)
    e0 = gcum - g_start
    q_lv = [(qf * jnp.exp(e0)).astype(BF16)]
    k_lv = [(kf * jnp.exp(-e0)).astype(BF16)]
    shifts = (4, 5, 6)
    for sh in shifts:
        b = 1 << sh
        bound = _block_rows(gcum, [2 * b * i + b - 1 for i in range(CHUNK // (2 * b))], 2 * b)
        second = ((roww >> sh) & 1) == 1
        x_l = jnp.exp(jnp.where(second, gcum - bound, bound - gcum))
        q_lv.append((qf * x_l).astype(BF16))
        k_lv.append((kf * x_l).astype(BF16))
    masks = [jnp.logical_and((row >> 4) == (col >> 4), causal)]
    for sh in shifts:
        same = (row >> (sh + 1)) == (col >> (sh + 1))
        m = jnp.logical_and(same, jnp.logical_and(((row >> sh) & 1) == 1, ((col >> sh) & 1) == 0))
        masks.append(m)
    g_last = gcum[CHUNK - 1:CHUNK, :]
    q_in = (qf * jnp.exp(gcum)).astype(BF16)
    k_end = (kf * jnp.exp(g_last - gcum)).astype(BF16)
    dec = jnp.exp(g_last)
    vb = v_ref[...]
    for hh in range(GLA_HEADS):
        kl = slice(hh * GLA_KEY_DIM, (hh + 1) * GLA_KEY_DIM)
        vl = slice(hh * GLA_VAL_DIM, (hh + 1) * GLA_VAL_DIM)
        scores = jnp.zeros((CHUNK, CHUNK), F32)
        for lv in range(len(masks)):
            scores = scores + jnp.where(masks[lv], _dot_nt(q_lv[lv][:, kl], k_lv[lv][:, kl]), 0.0)
        v_h = vb[:, vl]
        s_t = gstate_ref[hh]
        o = _dot(scores.astype(BF16), v_h) + _dot_nt(q_in[:, kl], s_t.astype(BF16))
        gstate_ref[hh] = dec[:, kl] * s_t + _dot_tn(v_h, k_end[:, kl])
        o = o * lax.rsqrt(jnp.mean(o * o, axis=-1, keepdims=True) + EPS) * glaw_ref[...]
        o = o * _silu(g_ref[:, vl].astype(F32))
        out_ref[:, SSD_WIDTH + hh * GLA_VAL_DIM:SSD_WIDTH + (hh + 1) * GLA_VAL_DIM] = o.astype(out_ref.dtype)


def _mixer(z, xbc, q, k, v, g, misc, params, batch, nchunks):
    widths = (SSD_WIDTH, XBC_WIDTH, GLA_K_WIDTH, GLA_K_WIDTH, GLA_V_WIDTH, GLA_V_WIDTH, MISC_WIDTH)
    meta_block = z.shape[0] // CHUNK - 1
    row_spec = lambda w: pl.BlockSpec(
        (CHUNK, w), lambda b, c: (jnp.where(c == 0, meta_block, b * (nchunks - 1) + c - 1), 0))
    par_spec = lambda p: pl.BlockSpec(p.shape, lambda b, c: (0,) * p.ndim)
    n_out = batch * (nchunks - 1) * CHUNK
    return pl.pallas_call(
        _mixer_kernel,
        grid=(batch, nchunks),
        in_specs=[row_spec(w) for w in widths] + [par_spec(p) for p in params],
        out_specs=pl.BlockSpec((CHUNK, SSD_WIDTH + GLA_V_WIDTH),
                               lambda b, c: (b * (nchunks - 1) + jnp.maximum(c - 1, 0), 0)),
        out_shape=jax.ShapeDtypeStruct((n_out, SSD_WIDTH + GLA_V_WIDTH), BF16),
        scratch_shapes=[
            pltpu.VMEM((CHUNK + 8, XBC_WIDTH), F32),
            pltpu.VMEM((SSD_GROUPS, SSD_STATE, SSD_WIDTH // SSD_GROUPS), F32),
            pltpu.VMEM((GLA_HEADS, GLA_VAL_DIM, GLA_KEY_DIM), F32),
        ],
        compiler_params=pltpu.CompilerParams(
            dimension_semantics=("arbitrary", "arbitrary"), vmem_limit_bytes=VMEM_LIMIT),
        name="mixer",
    )(z, xbc, q, k, v, g, misc, *params)


def _post_kernel(x_ref, mixed_ref, wout_ref, fnw_ref, wrt_ref, br_ref,
                 h2_ref, t_ref, idx_ref, gate_ref, lrank_ref, cnt_ref):
    h2 = x_ref[...] + _dot(mixed_ref[...], wout_ref[...])
    h2_ref[...] = h2
    t = h2 * lax.rsqrt(jnp.mean(h2 * h2, axis=-1, keepdims=True) + EPS) * fnw_ref[...]
    t_ref[...] = t.astype(BF16)

    t0, t1 = _split2(t)
    w0 = wrt_ref[0]
    w1 = wrt_ref[1]
    logits = _dot_nt(w0, t0) + _dot_nt(w0, t1) + _dot_nt(w1, t0) + br_ref[...]
    e_iota = lax.broadcasted_iota(jnp.int32, logits.shape, 0)
    work = logits
    sel_any = jnp.zeros(logits.shape, F32)
    tops = []
    idxs = []
    onehots = []
    for _ in range(TOP_K):
        m = jnp.max(work, axis=0, keepdims=True)
        idx = jnp.min(jnp.where(work == m, e_iota, N_EXPERTS), axis=0, keepdims=True)
        hit = e_iota == idx
        tops.append(m)
        idxs.append(idx)
        onehots.append(hit)
        sel_any = sel_any + jnp.where(hit, 1.0, 0.0)
        work = jnp.where(hit, -jnp.inf, work)
    exps = [jnp.exp(tv - tops[0]) for tv in tops]
    denom = exps[0] + exps[1] + exps[2] + exps[3]
    tile = logits.shape[1]
    pad_rows = 8 - TOP_K
    gate_ref[...] = jnp.concatenate([e / denom for e in exps] + [jnp.zeros((pad_rows, tile), F32)], axis=0)
    idx_ref[...] = jnp.concatenate(idxs + [jnp.zeros((pad_rows, tile), jnp.int32)], axis=0)

    r_i = lax.broadcasted_iota(jnp.int32, (tile, tile), 0)
    c_i = lax.broadcasted_iota(jnp.int32, (tile, tile), 1)
    upper = jnp.where(r_i < c_i, 1.0, 0.0).astype(BF16)
    before = _dot(sel_any.astype(BF16), upper)
    ranks = [jnp.sum(jnp.where(h, before, 0.0), axis=0, keepdims=True) for h in onehots]
    ranks.append(jnp.full((pad_rows, tile), float(NO_ROW), F32))
    lrank_ref[...] = jnp.concatenate(ranks, axis=0).astype(jnp.int32)
    cnt = jnp.sum(sel_any, axis=1, keepdims=True)
    cnt_ref[...] = jnp.broadcast_to(cnt, cnt_ref.shape).astype(jnp.int32)


def _post(x_flat, mixed, w_out_b, ffn_norm_w, w_router_t, b_router_col):
    n = x_flat.shape[0]
    tt = TOKEN_TILE
    const = lambda shape: pl.BlockSpec(shape, lambda i: (0,) * len(shape))
    return pl.pallas_call(
        _post_kernel,
        grid=(n // tt,),
        in_specs=[
            pl.BlockSpec((tt, D_MODEL), lambda i: (i, 0)),
            pl.BlockSpec((tt, SSD_WIDTH + GLA_V_WIDTH), lambda i: (i, 0)),
            const(w_out_b.shape),
            const((1, D_MODEL)),
            const(w_router_t.shape),
            const((N_EXPERTS, 1)),
        ],
        out_specs=[
            pl.BlockSpec((tt, D_MODEL), lambda i: (i, 0)),
            pl.BlockSpec((tt, D_MODEL), lambda i: (i, 0)),
            pl.BlockSpec((8, tt), lambda i: (0, i)),
            pl.BlockSpec((8, tt), lambda i: (0, i)),
            pl.BlockSpec((8, tt), lambda i: (0, i)),
            pl.BlockSpec((N_EXPERTS, 128), lambda i: (i, 0)),
        ],
        out_shape=[
            jax.ShapeDtypeStruct((n, D_MODEL), F32),
            jax.ShapeDtypeStruct((n, D_MODEL), BF16),
            jax.ShapeDtypeStruct((8, n), jnp.int32),
            jax.ShapeDtypeStruct((8, n), F32),
            jax.ShapeDtypeStruct((8, n), jnp.int32),
            jax.ShapeDtypeStruct((n // tt * N_EXPERTS, 128), jnp.int32),
        ],
        compiler_params=pltpu.CompilerParams(
            dimension_semantics=("parallel",), vmem_limit_bytes=VMEM_LIMIT),
        name="post",
    )(x_flat, mixed, w_out_b, ffn_norm_w, w_router_t, b_router_col)


def _pack_pairs(x):
    lo = pltpu.bitcast(x[:, :HALF], jnp.uint32) >> 16
    hi = pltpu.bitcast(x[:, HALF:], jnp.uint32) & jnp.uint32(0xFFFF0000)
    return lo | hi


def _unpack_pairs(w):
    lo = pltpu.bitcast(w << 16, F32).astype(BF16)
    hi = pltpu.bitcast(w & jnp.uint32(0xFFFF0000), F32).astype(BF16)
    return lo, hi


def _sorted_positions(idx, lrank, offs_ref, base):
    pos = lrank
    for e in range(N_EXPERTS):
        pos = pos + jnp.where(idx == e, offs_ref[base + e], 0)
    return pos


def _group_copies(dst_ref, make_copy):
    copies = [make_copy(g * SEG_ALIGN, pl.multiple_of(dst_ref[0, 0, g], SEG_ALIGN))
              for g in range(SORT_GROUPS)]
    for cp in copies:
        cp.start()
    return copies


def _dispatch_kernel(offs_ref, idx_ref, lrank_ref, t_ref, dst_ref, buf_in_ref, buf_ref,
                     sorted_ref, sem):
    del buf_in_ref
    base = pl.program_id(0) * N_EXPERTS
    tt = t_ref.shape[0]
    pos = _sorted_positions(idx_ref[...], lrank_ref[...], offs_ref, base)
    t = t_ref[...]
    for r0 in range(0, SORT_ROWS, tt):
        prow = lax.broadcasted_iota(jnp.int32, (tt, tt), 0) + r0
        perm = jnp.zeros((tt, tt), F32)
        for kk in range(TOP_K):
            perm = jnp.where(prow == pos[kk:kk + 1, :], 1.0, perm)
        sorted_ref[r0:r0 + tt, :] = _pack_pairs(_dot(perm.astype(BF16), t))

    def make_copy(local, glob):
        return pltpu.make_async_copy(
            sorted_ref.at[pl.ds(local, SEG_ALIGN), :], buf_ref.at[pl.ds(glob, SEG_ALIGN), :], sem)

    for cp in _group_copies(dst_ref, make_copy):
        cp.wait()


def _dispatch(offs, group_dst, top_idx, lrank, t_b, buf0):
    n = t_b.shape[0]
    tt = TOKEN_TILE
    grid_spec = pltpu.PrefetchScalarGridSpec(
        num_scalar_prefetch=1,
        grid=(n // tt,),
        in_specs=[
            pl.BlockSpec((8, tt), lambda i, *_: (0, i)),
            pl.BlockSpec((8, tt), lambda i, *_: (0, i)),
            pl.BlockSpec((tt, D_MODEL), lambda i, *_: (i, 0)),
            pl.BlockSpec((1, 1, SORT_GROUPS), lambda i, *_: (i, 0, 0), memory_space=pltpu.SMEM),
            pl.BlockSpec(memory_space=pl.ANY),
        ],
        out_specs=pl.BlockSpec(memory_space=pl.ANY),
        scratch_shapes=[pltpu.VMEM((SORT_ROWS, HALF), jnp.uint32), pltpu.SemaphoreType.DMA(())],
    )
    return pl.pallas_call(
        _dispatch_kernel,
        grid_spec=grid_spec,
        out_shape=jax.ShapeDtypeStruct(buf0.shape, jnp.uint32),
        input_output_aliases={5: 0},
        compiler_params=pltpu.CompilerParams(
            dimension_semantics=("arbitrary",), vmem_limit_bytes=VMEM_LIMIT),
        name="dispatch",
    )(offs, top_idx, lrank, t_b, group_dst, buf0)


def _expert_kernel(be_ref, nused_ref, xp_ref, wgu_ref, bgu_ref, wd_ref, bd_ref, y_ref,
                   wgu_b_ref, wd_b_ref):
    i = pl.program_id(0)
    used = i < nused_ref[0]

    @pl.when(jnp.logical_not(used))
    def _():
        y_ref[...] = jnp.zeros_like(y_ref)

    new_expert = jnp.logical_or(i == 0, be_ref[i] != be_ref[jnp.maximum(i - 1, 0)])

    @pl.when(jnp.logical_and(used, new_expert))
    def _():
        rows = 128
        for r0 in range(0, D_MODEL, rows):
            wgu_b_ref[r0:r0 + rows, :] = wgu_ref[r0:r0 + rows, :].astype(BF16)
        for r0 in range(0, D_FF, rows):
            wd_b_ref[r0:r0 + rows, :] = wd_ref[r0:r0 + rows, :].astype(BF16)

    @pl.when(used)
    def _():
        x_lo, x_hi = _unpack_pairs(xp_ref[...])
        hgu = _dot(x_lo, wgu_b_ref[:HALF, :]) + _dot(x_hi, wgu_b_ref[HALF:, :]) + bgu_ref[...]
        gate = jnp.minimum(hgu[:, :D_FF], SWIGLU_LIMIT)
        up = jnp.clip(hgu[:, D_FF:], -SWIGLU_LIMIT, SWIGLU_LIMIT)
        act = gate * jax.nn.sigmoid(SWIGLU_ALPHA * gate)
        y = _dot(((up + 1.0) * act).astype(BF16), wd_b_ref[...]) + bd_ref[...]
        y_ref[...] = _pack_pairs(y.astype(BF16).astype(F32))


def _experts(block_expert, n_used, buf, w_gu, b_gu, w_d, b_d):
    rows = buf.shape[0]
    n_blocks = rows // MOE_BLOCK
    row_map = lambda i, be, nu: (jnp.minimum(i, nu[0] - 1), 0)
    w_map = lambda i, be, nu: (be[jnp.minimum(i, nu[0] - 1)], 0, 0)
    grid_spec = pltpu.PrefetchScalarGridSpec(
        num_scalar_prefetch=2,
        grid=(n_blocks,),
        in_specs=[
            pl.BlockSpec((MOE_BLOCK, HALF), row_map),
            pl.BlockSpec((None, D_MODEL, 2 * D_FF), w_map),
            pl.BlockSpec((None, 1, 2 * D_FF), w_map),
            pl.BlockSpec((None, D_FF, D_MODEL), w_map),
            pl.BlockSpec((None, 1, D_MODEL), w_map),
        ],
        out_specs=pl.BlockSpec((MOE_BLOCK, HALF), lambda i, be, nu: (i, 0)),
        scratch_shapes=[pltpu.VMEM((D_MODEL, 2 * D_FF), BF16), pltpu.VMEM((D_FF, D_MODEL), BF16)],
    )
    return pl.pallas_call(
        _expert_kernel,
        grid_spec=grid_spec,
        out_shape=jax.ShapeDtypeStruct((rows, HALF), jnp.uint32),
        compiler_params=pltpu.CompilerParams(
            dimension_semantics=("arbitrary",), vmem_limit_bytes=VMEM_LIMIT),
        name="experts",
    )(block_expert, n_used, buf, w_gu, b_gu, w_d, b_d)


def _combine_kernel(offs_ref, idx_ref, lrank_ref, gate_ref, h2_ref, fw_ref, dst_ref,
                    y_hbm_ref, out_ref, ys_ref, sem):
    base = pl.program_id(0) * N_EXPERTS
    tt = h2_ref.shape[0]

    def make_copy(local, glob):
        return pltpu.make_async_copy(
            y_hbm_ref.at[pl.ds(glob, SEG_ALIGN), :], ys_ref.at[pl.ds(local, SEG_ALIGN), :], sem)

    copies = _group_copies(dst_ref, make_copy)

    pos = _sorted_positions(idx_ref[...], lrank_ref[...], offs_ref, base).astype(F32)
    zpad = jnp.zeros((128 - 8, tt), F32)
    pos_c = jnp.concatenate([pos, zpad], axis=0).T
    gate_c = jnp.concatenate([gate_ref[...], zpad], axis=0).T
    lane = lax.broadcasted_iota(jnp.int32, (tt, SORT_ROWS), 1).astype(F32)
    pg = jnp.zeros((tt, SORT_ROWS), F32)
    for kk in range(TOP_K):
        pg = jnp.where(lane == pos_c[:, kk:kk + 1], gate_c[:, kk:kk + 1], pg)
    p_hi, p_lo = _split2(pg)

    for cp in copies:
        cp.wait()
    y_lo, y_hi = _unpack_pairs(ys_ref[...])
    ffn = jnp.concatenate(
        [_dot(p_hi, y_lo) + _dot(p_lo, y_lo), _dot(p_hi, y_hi) + _dot(p_lo, y_hi)], axis=1)
    acc = h2_ref[...] + ffn
    out_ref[...] = acc * lax.rsqrt(jnp.mean(acc * acc, axis=-1, keepdims=True) + EPS) * fw_ref[...]


def _combine(offs, group_dst, top_idx, lrank, gates, h2, final_norm_w, y_buf):
    n = h2.shape[0]
    tt = TOKEN_TILE
    grid_spec = pltpu.PrefetchScalarGridSpec(
        num_scalar_prefetch=1,
        grid=(n // tt,),
        in_specs=[
            pl.BlockSpec((8, tt), lambda i, *_: (0, i)),
            pl.BlockSpec((8, tt), lambda i, *_: (0, i)),
            pl.BlockSpec((8, tt), lambda i, *_: (0, i)),
            pl.BlockSpec((tt, D_MODEL), lambda i, *_: (i, 0)),
            pl.BlockSpec((1, D_MODEL), lambda i, *_: (0, 0)),
            pl.BlockSpec((1, 1, SORT_GROUPS), lambda i, *_: (i, 0, 0), memory_space=pltpu.SMEM),
            pl.BlockSpec(memory_space=pl.ANY),
        ],
        out_specs=pl.BlockSpec((tt, D_MODEL), lambda i, *_: (i, 0)),
        scratch_shapes=[pltpu.VMEM((SORT_ROWS, HALF), jnp.uint32), pltpu.SemaphoreType.DMA(())],
    )
    return pl.pallas_call(
        _combine_kernel,
        grid_spec=grid_spec,
        out_shape=jax.ShapeDtypeStruct((n, D_MODEL), F32),
        compiler_params=pltpu.CompilerParams(
            dimension_semantics=("arbitrary",), vmem_limit_bytes=VMEM_LIMIT),
        name="combine",
    )(offs, top_idx, lrank, gates, h2, final_norm_w, group_dst, y_buf)


def _pad_lanes(v, width):
    return jnp.pad(v, ((0, 0), (0, width - v.shape[1])))


def kernel(x, meta_tokens, mix_norm_w, w_in, conv_w, conv_b, dt_bias, a_log, d_skip, ssd_norm_w,
           w_decay_up, b_decay, gla_norm_w, w_out, ffn_norm_w, w_router, b_router, w_gate_up,
           b_gate_up, w_down, b_down, final_norm_w):
    batch, seq, d = x.shape
    assert d == D_MODEL and seq % TOKEN_TILE == 0
    assert mix_norm_w.shape[0] == 1, "single-layer block"
    nchunks = (FRONT_PAD + N_META + seq) // CHUNK

    n = batch * seq
    x_flat = x.reshape(n, D_MODEL)
    meta_tile = jnp.concatenate(
        [jnp.zeros((TOKEN_TILE - N_META, D_MODEL), x.dtype), meta_tokens.astype(x.dtype)], axis=0)

    wi = w_in[0]
    o_z, o_xbc = 0, SSD_WIDTH
    o_dt = o_xbc + XBC_WIDTH
    o_q = o_dt + SSD_HEADS
    o_k = o_q + GLA_K_WIDTH
    o_v = o_k + GLA_K_WIDTH
    o_g = o_v + GLA_V_WIDTH
    o_a = o_g + GLA_V_WIDTH
    w_misc = jnp.concatenate(
        [wi[:, o_dt:o_dt + SSD_HEADS], wi[:, o_a:o_a + GLA_RANK],
         jnp.zeros((D_MODEL, MISC_WIDTH - SSD_HEADS - GLA_RANK), wi.dtype)], axis=1)
    w_in_r = jnp.concatenate(
        [wi[:, o_z:o_dt], wi[:, o_q:o_a], w_misc], axis=1).astype(BF16)
    z, xbc, q, k, v, g, misc = _in_proj(x_flat, meta_tile, mix_norm_w[0][None, :], w_in_r)

    dtb = _pad_lanes(dt_bias[0][None, :].astype(F32), MISC_WIDTH)
    aneg = _pad_lanes(-jnp.exp(a_log[0].astype(F32))[None, :], MISC_WIDTH)
    dskip = jnp.repeat(d_skip[0].astype(F32), SSD_HEAD_DIM)[None, :]
    wdec = jnp.zeros((MISC_WIDTH, GLA_K_WIDTH), F32).at[SSD_HEADS:SSD_HEADS + GLA_RANK].set(w_decay_up[0])
    params = (conv_w[0], conv_b[0][None, :], dtb, aneg, dskip, ssd_norm_w[0][None, :],
              wdec.astype(BF16), b_decay[0][None, :], gla_norm_w[0][None, :])
    mixed = _mixer(z, xbc, q, k, v, g, misc, params, batch, nchunks)

    wr_hi, wr_lo = _split2(w_router[0].T.astype(F32))
    h2, t_b, top_idx, gates, lrank, tile_cnt = _post(
        x_flat, mixed, w_out[0].astype(BF16), ffn_norm_w[0][None, :],
        jnp.stack([wr_hi, wr_lo]), b_router[0][:, None])

    n_tiles = n // TOKEN_TILE
    tile_cnt = tile_cnt.reshape(n_tiles, N_EXPERTS, 128)[:, :, 0]
    seg_rows = (tile_cnt + SEG_ALIGN - 1) // SEG_ALIGN * SEG_ALIGN
    counts = jnp.sum(seg_rows, axis=0)
    padded = (counts + MOE_BLOCK - 1) // MOE_BLOCK * MOE_BLOCK
    pend = jnp.cumsum(padded)
    pstart = pend - padded
    dstart = pstart[None, :] + jnp.cumsum(seg_rows, axis=0) - seg_rows
    seg_end = jnp.cumsum(seg_rows, axis=1)
    offs = seg_end - seg_rows
    n_blocks = -(-(n * TOP_K + n_tiles * N_EXPERTS * (SEG_ALIGN - 1)) // MOE_BLOCK) + N_EXPERTS
    block_pos = jnp.arange(n_blocks + SORT_ROWS // MOE_BLOCK, dtype=jnp.int32) * MOE_BLOCK
    block_expert = jnp.minimum(
        jnp.sum((pend[None, :] <= block_pos[:, None]).astype(jnp.int32), axis=1), N_EXPERTS - 1)
    n_used = (pend[-1:] // MOE_BLOCK).astype(jnp.int32)
    spare = n_blocks * MOE_BLOCK
    grow = jnp.arange(SORT_GROUPS, dtype=jnp.int32)[None, :, None] * SEG_ALIGN
    inside = (offs[:, None, :] <= grow) & (grow < seg_end[:, None, :])
    group_dst = jnp.sum(jnp.where(inside, dstart[:, None, :] + grow - offs[:, None, :], 0), axis=2)
    group_dst = jnp.where(jnp.any(inside, axis=2), group_dst, spare + grow[:, :, 0])
    group_dst = group_dst.astype(jnp.int32)[:, None, :]
    offs = offs.reshape(-1).astype(jnp.int32)
    total_rows = spare + SORT_ROWS

    buf = _dispatch(offs, group_dst, top_idx, lrank, t_b, jnp.zeros((total_rows, HALF), jnp.uint32))
    y_buf = _experts(block_expert, n_used, buf, w_gate_up[0], b_gate_up[0][:, None, :],
                     w_down[0], b_down[0][:, None, :])
    out = _combine(offs, group_dst, top_idx, lrank, gates, h2, final_norm_w[None, :], y_buf)
    return out.reshape(batch, seq, D_MODEL)
```

```python
import functools

import jax
import jax.numpy as jnp
from jax import lax
from jax.experimental import pallas as pl
from jax.experimental.pallas import tpu as pltpu

F32 = jnp.float32
BF16 = jnp.bfloat16

D_MODEL = 1024
N_META = 16
EPS = 1e-5
SSD_HEAD_DIM = 64
SSD_HEADS = 16
SSD_GROUPS = 2
SSD_STATE = 128
SSD_CONV = 4
SSD_WIDTH = SSD_HEADS * SSD_HEAD_DIM
XBC_WIDTH = SSD_WIDTH + 2 * SSD_GROUPS * SSD_STATE
GLA_HEADS = 4
GLA_KEY_DIM = 128
GLA_VAL_DIM = 256
GLA_K_WIDTH = GLA_HEADS * GLA_KEY_DIM
GLA_V_WIDTH = GLA_HEADS * GLA_VAL_DIM
GLA_RANK = 16
GLA_TAU = 16.0
GLA_SUB = 16
N_EXPERTS = 32
TOP_K = 4
D_FF = D_MODEL
SWIGLU_LIMIT = 7.0
SWIGLU_ALPHA = 1.702
MOE_BLOCK = 256

CHUNK = 128
FRONT_PAD = CHUNK - N_META
MISC_WIDTH = 128
IN_PAD_WIDTH = SSD_WIDTH + XBC_WIDTH + 2 * GLA_K_WIDTH + 2 * GLA_V_WIDTH + MISC_WIDTH
TOKEN_TILE = 256
HALF = D_MODEL // 2
NO_ROW = 1 << 12
SEG_ALIGN = 8
SORT_ROWS = -(-(TOP_K * TOKEN_TILE + N_EXPERTS * (SEG_ALIGN - 1)) // TOKEN_TILE) * TOKEN_TILE
SORT_GROUPS = SORT_ROWS // SEG_ALIGN
VMEM_LIMIT = 56 * 1024 * 1024


def _split2(x):
    hi = x.astype(BF16)
    lo = (x - hi.astype(F32)).astype(BF16)
    return hi, lo


def _split3(x):
    hi = x.astype(BF16)
    r = x - hi.astype(F32)
    mid = r.astype(BF16)
    lo = (r - mid.astype(F32)).astype(BF16)
    return hi, mid, lo


def _dot(a, b):
    return jnp.dot(a, b, preferred_element_type=F32)


def _dot_nt(a, b):
    return lax.dot_general(a, b, (((1,), (1,)), ((), ())), preferred_element_type=F32)


def _dot_tn(a, b):
    return lax.dot_general(a, b, (((0,), (0,)), ((), ())), preferred_element_type=F32)


def _silu(x):
    return x * jax.nn.sigmoid(x)


_IN_SECTIONS = (SSD_WIDTH, XBC_WIDTH, GLA_K_WIDTH, GLA_K_WIDTH, GLA_V_WIDTH, GLA_V_WIDTH, MISC_WIDTH)


def _in_proj_kernel(x_ref, meta_ref, nw_ref, w_ref, z_ref, xbc_ref, q_ref, k_ref, v_ref, g_ref, misc_ref):
    h = jnp.where(pl.program_id(0) < pl.num_programs(0) - 1, x_ref[...], meta_ref[...])
    u = h * lax.rsqrt(jnp.mean(h * h, axis=-1, keepdims=True) + EPS) * nw_ref[...]
    ub = u.astype(BF16)
    outs = (z_ref, xbc_ref, q_ref, k_ref, v_ref, g_ref, misc_ref)
    off = 0
    for o_ref, width in zip(outs, _IN_SECTIONS):
        o_ref[...] = _dot(ub, w_ref[:, off:off + width]).astype(o_ref.dtype)
        off += width


def _in_proj(x_flat, meta_tile, norm_w, w_in_r):
    tm = TOKEN_TILE
    nx = x_flat.shape[0] // tm
    rows = (nx + 1) * tm
    row_spec = lambda w: pl.BlockSpec((tm, w), lambda i: (i, 0))
    out_dtypes = (BF16,) * 6 + (F32,)
    return pl.pallas_call(
        _in_proj_kernel,
        grid=(nx + 1,),
        in_specs=[
            pl.BlockSpec((tm, D_MODEL), lambda i: (jnp.minimum(i, nx - 1), 0)),
            pl.BlockSpec((tm, D_MODEL), lambda i: (0, 0)),
            pl.BlockSpec((1, D_MODEL), lambda i: (0, 0)),
            pl.BlockSpec((D_MODEL, IN_PAD_WIDTH), lambda i: (0, 0)),
        ],
        out_specs=[row_spec(w) for w in _IN_SECTIONS],
        out_shape=[jax.ShapeDtypeStruct((rows, w), dt) for w, dt in zip(_IN_SECTIONS, out_dtypes)],
        compiler_params=pltpu.CompilerParams(
            dimension_semantics=("arbitrary",), vmem_limit_bytes=VMEM_LIMIT),
        name="in_proj",
    )(x_flat, meta_tile, norm_w, w_in_r)


def _block_rows(x, rows, width):
    parts = []
    for r in rows:
        if r is None:
            parts.append(jnp.zeros((width, x.shape[1]), x.dtype))
        else:
            parts.append(jnp.broadcast_to(x[r:r + 1, :], (width, x.shape[1])))
    return jnp.concatenate(parts, axis=0)


def _mixer_kernel(z_ref, xbc_ref, q_ref, k_ref, v_ref, g_ref, misc_ref,
                  convw_ref, convb_ref, dtb_ref, aneg_ref, dskip_ref, ssdw_ref,
                  wdec_ref, bdec_ref, glaw_ref,
                  out_ref,
                  xext_ref, sstate_ref, gstate_ref):
    c = pl.program_id(1)

    @pl.when(c == 0)
    def _():
        xext_ref[0:8, :] = jnp.zeros((8, XBC_WIDTH), F32)
        sstate_ref[...] = jnp.zeros_like(sstate_ref)
        gstate_ref[...] = jnp.zeros_like(gstate_ref)

    row = lax.broadcasted_iota(jnp.int32, (CHUNK, CHUNK), 0)
    col = lax.broadcasted_iota(jnp.int32, (CHUNK, CHUNK), 1)
    causal = row >= col
    tri = jnp.where(causal, 1.0, 0.0).astype(BF16)
    valid = jnp.logical_or(c > 0, row >= FRONT_PAD)

    xext_ref[8:8 + CHUNK, :] = xbc_ref[...].astype(F32)
    conv = convb_ref[...] + convw_ref[0:1, :] * xext_ref[5:5 + CHUNK, :]
    for kk in range(1, SSD_CONV):
        conv = conv + convw_ref[kk:kk + 1, :] * xext_ref[5 + kk:5 + kk + CHUNK, :]
    xext_ref[0:8, :] = xext_ref[CHUNK:CHUNK + 8, :]
    valid_w = jnp.logical_or(c > 0, lax.broadcasted_iota(jnp.int32, (CHUNK, XBC_WIDTH), 0) >= FRONT_PAD)
    act = jnp.where(valid_w, _silu(conv), 0.0)
    xs = act[:, :SSD_WIDTH]
    bmat = act[:, SSD_WIDTH:SSD_WIDTH + SSD_GROUPS * SSD_STATE].astype(BF16)
    cmat = act[:, SSD_WIDTH + SSD_GROUPS * SSD_STATE:].astype(BF16)

    misc = misc_ref[...]
    dt = jnp.where(valid, jax.nn.softplus(misc + dtb_ref[...]), 0.0)
    d_a = dt * aneg_ref[...]
    p0, p1, p2 = _split3(d_a)
    a_cs = _dot(tri, p0) + _dot(tri, p1) + _dot(tri, p2)
    a_cs_t = a_cs.T

    lo_half = col < SSD_HEAD_DIM
    hg = SSD_HEADS // SSD_GROUPS
    pairs_per_group = hg // 2
    y_parts = []
    for gi in range(SSD_GROUPS):
        b_g = bmat[:, gi * SSD_STATE:(gi + 1) * SSD_STATE]
        c_g = cmat[:, gi * SSD_STATE:(gi + 1) * SSD_STATE]
        c_g32 = c_g.astype(F32)
        cb = _dot_nt(c_g, b_g)
        w_parts = []
        cd_parts = []
        for pj in range(pairs_per_group):
            h0 = gi * hg + 2 * pj
            h1 = h0 + 1
            lanes = slice(h0 * SSD_HEAD_DIM, (h1 + 1) * SSD_HEAD_DIM)
            ab0 = jnp.broadcast_to(a_cs[:, h0:h0 + 1], (CHUNK, CHUNK))
            ab1 = jnp.broadcast_to(a_cs[:, h1:h1 + 1], (CHUNK, CHUNK))
            db0 = jnp.broadcast_to(dt[:, h0:h0 + 1], (CHUNK, CHUNK))
            db1 = jnp.broadcast_to(dt[:, h1:h1 + 1], (CHUNK, CHUNK))
            l0 = jnp.exp(jnp.where(causal, ab0 - a_cs_t[h0:h0 + 1, :], -jnp.inf))
            l1 = jnp.exp(jnp.where(causal, ab1 - a_cs_t[h1:h1 + 1, :], -jnp.inf))
            lhs = jnp.concatenate(
                [(cb * l0).astype(BF16), (c_g32 * jnp.exp(ab0)).astype(BF16),
                 (cb * l1).astype(BF16), (c_g32 * jnp.exp(ab1)).astype(BF16)], axis=1)
            xs_p = xs[:, lanes]
            dtx = xs_p * jnp.where(lo_half, db0, db1)
            st = sstate_ref[gi, :, pj * CHUNK:(pj + 1) * CHUNK]
            rhs = jnp.concatenate(
                [jnp.where(lo_half, dtx, 0.0).astype(BF16), jnp.where(lo_half, st, 0.0).astype(BF16),
                 jnp.where(lo_half, 0.0, dtx).astype(BF16), jnp.where(lo_half, 0.0, st).astype(BF16)],
                axis=0)
            y_p = _dot(lhs, rhs) + dskip_ref[:, lanes] * xs_p
            y_parts.append(y_p)
            ae = jnp.where(lo_half, ab0, ab1)
            a_last = ae[CHUNK - 1:CHUNK, :]
            w_parts.append((dtx * jnp.exp(a_last - ae)).astype(BF16))
            cd_parts.append(jnp.exp(a_last))
        upd = _dot_tn(b_g, jnp.concatenate(w_parts, axis=1))
        sstate_ref[gi] = jnp.concatenate(cd_parts, axis=1) * sstate_ref[gi] + upd
    y = jnp.concatenate(y_parts, axis=1)
    y = y * _silu(z_ref[...].astype(F32))
    gsz = SSD_WIDTH // SSD_GROUPS
    for gi in range(SSD_GROUPS):
        yg = y[:, gi * gsz:(gi + 1) * gsz]
        yn = yg * lax.rsqrt(jnp.mean(yg * yg, axis=-1, keepdims=True) + EPS)
        out_ref[:, gi * gsz:(gi + 1) * gsz] = (yn * ssdw_ref[:, gi * gsz:(gi + 1) * gsz]).astype(out_ref.dtype)

    m0, m1 = _split2(misc)
    pre = _dot(m0, wdec_ref[...]) + _dot(m1, wdec_ref[...]) + bdec_ref[...]
    log_a = jax.nn.log_sigmoid(pre) * (1.0 / GLA_TAU)
    g0, g1 = _split2(log_a)
    gcum = _dot(tri, g0) + _dot(tri, g1)
    roww = lax.broadcasted_iota(jnp.int32, (CHUNK, GLA_K_WIDTH), 0)
    qf = q_ref[...].astype(F32) * (GLA_KEY_DIM ** -0.5)
    kf = k_ref[...].astype(F32)
    nsub = CHUNK // GLA_SUB
    g_start = _block_rows(gcum, [None] + [GLA_SUB * i - 1 for i in range(1, nsub)], GLA_SUB)
    e0 = gcum - g_start
    q_lv = [(qf * jnp.exp(e0)).astype(BF16)]
    k_lv = [(kf * jnp.exp(-e0)).astype(BF16)]
    shifts = (4, 5, 6)
    for sh in shifts:
        b = 1 << sh
        bound = _block_rows(gcum, [2 * b * i + b - 1 for i in range(CHUNK // (2 * b))], 2 * b)
        second = ((roww >> sh) & 1) == 1
        x_l = jnp.exp(jnp.where(second, gcum - bound, bound - gcum))
        q_lv.append((qf * x_l).astype(BF16))
        k_lv.append((kf * x_l).astype(BF16))
    masks = [jnp.logical_and((row >> 4) == (col >> 4), causal)]
    for sh in shifts:
        same = (row >> (sh + 1)) == (col >> (sh + 1))
        m = jnp.logical_and(same, jnp.logical_and(((row >> sh) & 1) == 1, ((col >> sh) & 1) == 0))
        masks.append(m)
    g_last = gcum[CHUNK - 1:CHUNK, :]
    q_in = (qf * jnp.exp(gcum)).astype(BF16)
    k_end = (kf * jnp.exp(g_last - gcum)).astype(BF16)
    dec = jnp.exp(g_last)
    vb = v_ref[...]
    for hh in range(GLA_HEADS):
        kl = slice(hh * GLA_KEY_DIM, (hh + 1) * GLA_KEY_DIM)
        vl = slice(hh * GLA_VAL_DIM, (hh + 1) * GLA_VAL_DIM)
        scores = jnp.zeros((CHUNK, CHUNK), F32)
        for lv in range(len(masks)):
            scores = scores + jnp.where(masks[lv], _dot_nt(q_lv[lv][:, kl], k_lv[lv][:, kl]), 0.0)
        v_h = vb[:, vl]
        s_t = gstate_ref[hh]
        o = _dot(scores.astype(BF16), v_h) + _dot_nt(q_in[:, kl], s_t.astype(BF16))
        gstate_ref[hh] = dec[:, kl] * s_t + _dot_tn(v_h, k_end[:, kl])
        o = o * lax.rsqrt(jnp.mean(o * o, axis=-1, keepdims=True) + EPS) * glaw_ref[...]
        o = o * _silu(g_ref[:, vl].astype(F32))
        out_ref[:, SSD_WIDTH + hh * GLA_VAL_DIM:SSD_WIDTH + (hh + 1) * GLA_VAL_DIM] = o.astype(out_ref.dtype)


def _mixer(z, xbc, q, k, v, g, misc, params, batch, nchunks):
    widths = (SSD_WIDTH, XBC_WIDTH, GLA_K_WIDTH, GLA_K_WIDTH, GLA_V_WIDTH, GLA_V_WIDTH, MISC_WIDTH)
    meta_block = z.shape[0] // CHUNK - 1
    row_spec = lambda w: pl.BlockSpec(
        (CHUNK, w), lambda b, c: (jnp.where(c == 0, meta_block, b * (nchunks - 1) + c - 1), 0))
    par_spec = lambda p: pl.BlockSpec(p.shape, lambda b, c: (0,) * p.ndim)
    n_out = batch * (nchunks - 1) * CHUNK
    return pl.pallas_call(
        _mixer_kernel,
        grid=(batch, nchunks),
        in_specs=[row_spec(w) for w in widths] + [par_spec(p) for p in params],
        out_specs=pl.BlockSpec((CHUNK, SSD_WIDTH + GLA_V_WIDTH),
                               lambda b, c: (b * (nchunks - 1) + jnp.maximum(c - 1, 0), 0)),
        out_shape=jax.ShapeDtypeStruct((n_out, SSD_WIDTH + GLA_V_WIDTH), BF16),
        scratch_shapes=[
            pltpu.VMEM((CHUNK + 8, XBC_WIDTH), F32),
            pltpu.VMEM((SSD_GROUPS, SSD_STATE, SSD_WIDTH // SSD_GROUPS), F32),
            pltpu.VMEM((GLA_HEADS, GLA_VAL_DIM, GLA_KEY_DIM), F32),
        ],
        compiler_params=pltpu.CompilerParams(
            dimension_semantics=("arbitrary", "arbitrary"), vmem_limit_bytes=VMEM_LIMIT),
        name="mixer",
    )(z, xbc, q, k, v, g, misc, *params)


def _post_kernel(x_ref, mixed_ref, wout_ref, fnw_ref, wrt_ref, br_ref,
                 h2_ref, t_ref, idx_ref, gate_ref, lrank_ref, cnt_ref, zero_ref):
    h2 = x_ref[...] + _dot(mixed_ref[...], wout_ref[...])
    h2_ref[...] = h2
    t = h2 * lax.rsqrt(jnp.mean(h2 * h2, axis=-1, keepdims=True) + EPS) * fnw_ref[...]
    t_ref[...] = t.astype(BF16)

    t0, t1 = _split2(t)
    w0 = wrt_ref[0]
    w1 = wrt_ref[1]
    logits = _dot_nt(w0, t0) + _dot_nt(w0, t1) + _dot_nt(w1, t0) + br_ref[...]
    e_iota = lax.broadcasted_iota(jnp.int32, logits.shape, 0)
    work = logits
    sel_any = jnp.zeros(logits.shape, F32)
    tops = []
    idxs = []
    onehots = []
    for _ in range(TOP_K):
        m = jnp.max(work, axis=0, keepdims=True)
        idx = jnp.min(jnp.where(work == m, e_iota, N_EXPERTS), axis=0, keepdims=True)
        hit = e_iota == idx
        tops.append(m)
        idxs.append(idx)
        onehots.append(hit)
        sel_any = sel_any + jnp.where(hit, 1.0, 0.0)
        work = jnp.where(hit, -jnp.inf, work)
    exps = [jnp.exp(tv - tops[0]) for tv in tops]
    denom = exps[0] + exps[1] + exps[2] + exps[3]
    tile = logits.shape[1]
    pad_rows = 8 - TOP_K
    gate_ref[...] = jnp.concatenate([e / denom for e in exps] + [jnp.zeros((pad_rows, tile), F32)], axis=0)
    idx_ref[...] = jnp.concatenate(idxs + [jnp.zeros((pad_rows, tile), jnp.int32)], axis=0)

    r_i = lax.broadcasted_iota(jnp.int32, (tile, tile), 0)
    c_i = lax.broadcasted_iota(jnp.int32, (tile, tile), 1)
    upper = jnp.where(r_i < c_i, 1.0, 0.0).astype(BF16)
    before = _dot(sel_any.astype(BF16), upper)
    ranks = [jnp.sum(jnp.where(h, before, 0.0), axis=0, keepdims=True) for h in onehots]
    ranks.append(jnp.full((pad_rows, tile), float(NO_ROW), F32))
    lrank_ref[...] = jnp.concatenate(ranks, axis=0).astype(jnp.int32)
    cnt = jnp.sum(sel_any, axis=1, keepdims=True)
    cnt_ref[...] = jnp.broadcast_to(cnt, cnt_ref.shape).astype(jnp.int32)
    zero_ref[...] = jnp.zeros_like(zero_ref)


def _post(x_flat, mixed, w_out_b, ffn_norm_w, w_router_t, b_router_col, zero_rows):
    n = x_flat.shape[0]
    tt = TOKEN_TILE
    const = lambda shape: pl.BlockSpec(shape, lambda i: (0,) * len(shape))
    return pl.pallas_call(
        _post_kernel,
        grid=(n // tt,),
        in_specs=[
            pl.BlockSpec((tt, D_MODEL), lambda i: (i, 0)),
            pl.BlockSpec((tt, SSD_WIDTH + GLA_V_WIDTH), lambda i: (i, 0)),
            const(w_out_b.shape),
            const((1, D_MODEL)),
            const(w_router_t.shape),
            const((N_EXPERTS, 1)),
        ],
        out_specs=[
            pl.BlockSpec((tt, D_MODEL), lambda i: (i, 0)),
            pl.BlockSpec((tt, D_MODEL), lambda i: (i, 0)),
            pl.BlockSpec((8, tt), lambda i: (0, i)),
            pl.BlockSpec((8, tt), lambda i: (0, i)),
            pl.BlockSpec((8, tt), lambda i: (0, i)),
            pl.BlockSpec((N_EXPERTS, 128), lambda i: (i, 0)),
            pl.BlockSpec((zero_rows, HALF), lambda i: (i, 0)),
        ],
        out_shape=[
            jax.ShapeDtypeStruct((n, D_MODEL), F32),
            jax.ShapeDtypeStruct((n, D_MODEL), BF16),
            jax.ShapeDtypeStruct((8, n), jnp.int32),
            jax.ShapeDtypeStruct((8, n), F32),
            jax.ShapeDtypeStruct((8, n), jnp.int32),
            jax.ShapeDtypeStruct((n // tt * N_EXPERTS, 128), jnp.int32),
            jax.ShapeDtypeStruct((n // tt * zero_rows, HALF), jnp.uint32),
        ],
        compiler_params=pltpu.CompilerParams(
            dimension_semantics=("parallel",), vmem_limit_bytes=VMEM_LIMIT),
        name="post",
    )(x_flat, mixed, w_out_b, ffn_norm_w, w_router_t, b_router_col)


def _pack_pairs(x):
    lo = pltpu.bitcast(x[:, :HALF], jnp.uint32) >> 16
    hi = pltpu.bitcast(x[:, HALF:], jnp.uint32) & jnp.uint32(0xFFFF0000)
    return lo | hi


def _unpack_pairs(w):
    lo = pltpu.bitcast(w << 16, F32).astype(BF16)
    hi = pltpu.bitcast(w & jnp.uint32(0xFFFF0000), F32).astype(BF16)
    return lo, hi


def _sorted_positions(idx, lrank, offs_ref, base):
    pos = lrank
    for e in range(N_EXPERTS):
        pos = pos + jnp.where(idx == e, offs_ref[base + e], 0)
    return pos


def _group_copies(dst_ref, make_copy):
    return [make_copy(g * SEG_ALIGN, pl.multiple_of(dst_ref[0, 0, g], SEG_ALIGN))
            for g in range(SORT_GROUPS)]


def _dispatch_kernel(offs_ref, idx_ref, lrank_ref, t_ref, dst_ref, dst_prev_ref, buf_in_ref, buf_ref,
                     sorted_ref, sem):
    del buf_in_ref
    i = pl.program_id(0)
    slot = i % 2
    base = i * N_EXPERTS
    tt = t_ref.shape[0]
    pos = _sorted_positions(idx_ref[...], lrank_ref[...], offs_ref, base)
    t = t_ref[...]
    for r0 in range(0, SORT_ROWS, tt):
        prow = lax.broadcasted_iota(jnp.int32, (tt, tt), 0) + r0
        perm = jnp.zeros((tt, tt), F32)
        for kk in range(TOP_K):
            perm = jnp.where(prow == pos[kk:kk + 1, :], 1.0, perm)
        sorted_ref[slot, r0:r0 + tt, :] = _pack_pairs(_dot(perm.astype(BF16), t))

    def copies(table_ref, which):
        def make_copy(local, glob):
            return pltpu.make_async_copy(
                sorted_ref.at[which, pl.ds(local, SEG_ALIGN), :],
                buf_ref.at[pl.ds(glob, SEG_ALIGN), :], sem.at[which])
        return _group_copies(table_ref, make_copy)

    for cp in copies(dst_ref, slot):
        cp.start()

    @pl.when(i > 0)
    def _():
        for cp in copies(dst_prev_ref, 1 - slot):
            cp.wait()

    @pl.when(i == pl.num_programs(0) - 1)
    def _():
        for cp in copies(dst_ref, slot):
            cp.wait()


def _dispatch(offs, group_dst, top_idx, lrank, t_b, buf0):
    n = t_b.shape[0]
    tt = TOKEN_TILE
    table = lambda index: pl.BlockSpec((1, 1, SORT_GROUPS), index, memory_space=pltpu.SMEM)
    grid_spec = pltpu.PrefetchScalarGridSpec(
        num_scalar_prefetch=1,
        grid=(n // tt,),
        in_specs=[
            pl.BlockSpec((8, tt), lambda i, *_: (0, i)),
            pl.BlockSpec((8, tt), lambda i, *_: (0, i)),
            pl.BlockSpec((tt, D_MODEL), lambda i, *_: (i, 0)),
            table(lambda i, *_: (i, 0, 0)),
            table(lambda i, *_: (jnp.maximum(i - 1, 0), 0, 0)),
            pl.BlockSpec(memory_space=pl.ANY),
        ],
        out_specs=pl.BlockSpec(memory_space=pl.ANY),
        scratch_shapes=[pltpu.VMEM((2, SORT_ROWS, HALF), jnp.uint32), pltpu.SemaphoreType.DMA((2,))],
    )
    return pl.pallas_call(
        _dispatch_kernel,
        grid_spec=grid_spec,
        out_shape=jax.ShapeDtypeStruct(buf0.shape, jnp.uint32),
        input_output_aliases={6: 0},
        compiler_params=pltpu.CompilerParams(
            dimension_semantics=("arbitrary",), vmem_limit_bytes=VMEM_LIMIT),
        name="dispatch",
    )(offs, top_idx, lrank, t_b, group_dst, group_dst, buf0)


def _expert_kernel(be_ref, nused_ref, xp_ref, wgu_ref, bgu_ref, wd_ref, bd_ref, y_ref,
                   wgu_b_ref, wd_b_ref):
    i = pl.program_id(0)
    used = i < nused_ref[0]

    @pl.when(jnp.logical_not(used))
    def _():
        y_ref[...] = jnp.zeros_like(y_ref)

    new_expert = jnp.logical_or(i == 0, be_ref[i] != be_ref[jnp.maximum(i - 1, 0)])

    @pl.when(jnp.logical_and(used, new_expert))
    def _():
        rows = 128
        for r0 in range(0, D_MODEL, rows):
            wgu_b_ref[r0:r0 + rows, :] = wgu_ref[r0:r0 + rows, :].astype(BF16)
        for r0 in range(0, D_FF, rows):
            wd_b_ref[r0:r0 + rows, :] = wd_ref[r0:r0 + rows, :].astype(BF16)

    @pl.when(used)
    def _():
        x_lo, x_hi = _unpack_pairs(xp_ref[...])
        hgu = _dot(x_lo, wgu_b_ref[:HALF, :]) + _dot(x_hi, wgu_b_ref[HALF:, :]) + bgu_ref[...]
        gate = jnp.minimum(hgu[:, :D_FF], SWIGLU_LIMIT)
        up = jnp.clip(hgu[:, D_FF:], -SWIGLU_LIMIT, SWIGLU_LIMIT)
        act = gate * jax.nn.sigmoid(SWIGLU_ALPHA * gate)
        y = _dot(((up + 1.0) * act).astype(BF16), wd_b_ref[...]) + bd_ref[...]
        y_ref[...] = _pack_pairs(y.astype(BF16).astype(F32))


def _experts(block_expert, n_used, buf, w_gu, b_gu, w_d, b_d):
    n_blocks = block_expert.shape[0]
    rows = n_blocks * MOE_BLOCK
    row_map = lambda i, be, nu: (jnp.minimum(i, nu[0] - 1), 0)
    w_map = lambda i, be, nu: (be[jnp.minimum(i, nu[0] - 1)], 0, 0)
    grid_spec = pltpu.PrefetchScalarGridSpec(
        num_scalar_prefetch=2,
        grid=(n_blocks,),
        in_specs=[
            pl.BlockSpec((MOE_BLOCK, HALF), row_map),
            pl.BlockSpec((None, D_MODEL, 2 * D_FF), w_map),
            pl.BlockSpec((None, 1, 2 * D_FF), w_map),
            pl.BlockSpec((None, D_FF, D_MODEL), w_map),
            pl.BlockSpec((None, 1, D_MODEL), w_map),
        ],
        out_specs=pl.BlockSpec((MOE_BLOCK, HALF), lambda i, be, nu: (i, 0)),
        scratch_shapes=[pltpu.VMEM((D_MODEL, 2 * D_FF), BF16), pltpu.VMEM((D_FF, D_MODEL), BF16)],
    )
    return pl.pallas_call(
        _expert_kernel,
        grid_spec=grid_spec,
        out_shape=jax.ShapeDtypeStruct((rows, HALF), jnp.uint32),
        compiler_params=pltpu.CompilerParams(
            dimension_semantics=("arbitrary",), vmem_limit_bytes=VMEM_LIMIT),
        name="experts",
    )(block_expert, n_used, buf, w_gu, b_gu, w_d, b_d)


def _combine_kernel(offs_ref, idx_ref, lrank_ref, gate_ref, h2_ref, fw_ref, dst_ref, dst_next_ref,
                    y_hbm_ref, out_ref, ys_ref, sem):
    i = pl.program_id(0)
    slot = i % 2
    base = i * N_EXPERTS
    tt = h2_ref.shape[0]

    def copies(table_ref, which):
        def make_copy(local, glob):
            return pltpu.make_async_copy(
                y_hbm_ref.at[pl.ds(glob, SEG_ALIGN), :],
                ys_ref.at[which, pl.ds(local, SEG_ALIGN), :], sem.at[which])
        return _group_copies(table_ref, make_copy)

    @pl.when(i == 0)
    def _():
        for cp in copies(dst_ref, slot):
            cp.start()

    @pl.when(i < pl.num_programs(0) - 1)
    def _():
        for cp in copies(dst_next_ref, 1 - slot):
            cp.start()

    pos = _sorted_positions(idx_ref[...], lrank_ref[...], offs_ref, base).astype(F32)
    zpad = jnp.zeros((128 - 8, tt), F32)
    pos_c = jnp.concatenate([pos, zpad], axis=0).T
    gate_c = jnp.concatenate([gate_ref[...], zpad], axis=0).T
    lane = lax.broadcasted_iota(jnp.int32, (tt, SORT_ROWS), 1).astype(F32)
    pg = jnp.zeros((tt, SORT_ROWS), F32)
    for kk in range(TOP_K):
        pg = jnp.where(lane == pos_c[:, kk:kk + 1], gate_c[:, kk:kk + 1], pg)
    p_hi, p_lo = _split2(pg)

    for cp in copies(dst_ref, slot):
        cp.wait()
    y_lo, y_hi = _unpack_pairs(ys_ref[slot])
    ffn = jnp.concatenate(
        [_dot(p_hi, y_lo) + _dot(p_lo, y_lo), _dot(p_hi, y_hi) + _dot(p_lo, y_hi)], axis=1)
    acc = h2_ref[...] + ffn
    out_ref[...] = acc * lax.rsqrt(jnp.mean(acc * acc, axis=-1, keepdims=True) + EPS) * fw_ref[...]


def _combine(offs, group_dst, top_idx, lrank, gates, h2, final_norm_w, y_buf):
    n = h2.shape[0]
    tt = TOKEN_TILE
    grid_spec = pltpu.PrefetchScalarGridSpec(
        num_scalar_prefetch=1,
        grid=(n // tt,),
        in_specs=[
            pl.BlockSpec((8, tt), lambda i, *_: (0, i)),
            pl.BlockSpec((8, tt), lambda i, *_: (0, i)),
            pl.BlockSpec((8, tt), lambda i, *_: (0, i)),
            pl.BlockSpec((tt, D_MODEL), lambda i, *_: (i, 0)),
            pl.BlockSpec((1, D_MODEL), lambda i, *_: (0, 0)),
            pl.BlockSpec((1, 1, SORT_GROUPS), lambda i, *_: (i, 0, 0), memory_space=pltpu.SMEM),
            pl.BlockSpec((1, 1, SORT_GROUPS), lambda i, *_: (jnp.minimum(i + 1, n // tt - 1), 0, 0),
                         memory_space=pltpu.SMEM),
            pl.BlockSpec(memory_space=pl.ANY),
        ],
        out_specs=pl.BlockSpec((tt, D_MODEL), lambda i, *_: (i, 0)),
        scratch_shapes=[pltpu.VMEM((2, SORT_ROWS, HALF), jnp.uint32), pltpu.SemaphoreType.DMA((2,))],
    )
    return pl.pallas_call(
        _combine_kernel,
        grid_spec=grid_spec,
        out_shape=jax.ShapeDtypeStruct((n, D_MODEL), F32),
        compiler_params=pltpu.CompilerParams(
            dimension_semantics=("arbitrary",), vmem_limit_bytes=VMEM_LIMIT),
        name="combine",
    )(offs, top_idx, lrank, gates, h2, final_norm_w, group_dst, group_dst, y_buf)


def _pad_lanes(v, width):
    return jnp.pad(v, ((0, 0), (0, width - v.shape[1])))


def kernel(x, meta_tokens, mix_norm_w, w_in, conv_w, conv_b, dt_bias, a_log, d_skip, ssd_norm_w,
           w_decay_up, b_decay, gla_norm_w, w_out, ffn_norm_w, w_router, b_router, w_gate_up,
           b_gate_up, w_down, b_down, final_norm_w):
    batch, seq, d = x.shape
    assert d == D_MODEL and seq % TOKEN_TILE == 0
    assert mix_norm_w.shape[0] == 1, "single-layer block"
    nchunks = (FRONT_PAD + N_META + seq) // CHUNK

    n = batch * seq
    x_flat = x.reshape(n, D_MODEL)
    meta_tile = jnp.concatenate(
        [jnp.zeros((TOKEN_TILE - N_META, D_MODEL), x.dtype), meta_tokens.astype(x.dtype)], axis=0)

    wi = w_in[0]
    o_z, o_xbc = 0, SSD_WIDTH
    o_dt = o_xbc + XBC_WIDTH
    o_q = o_dt + SSD_HEADS
    o_k = o_q + GLA_K_WIDTH
    o_v = o_k + GLA_K_WIDTH
    o_g = o_v + GLA_V_WIDTH
    o_a = o_g + GLA_V_WIDTH
    w_misc = jnp.concatenate(
        [wi[:, o_dt:o_dt + SSD_HEADS], wi[:, o_a:o_a + GLA_RANK],
         jnp.zeros((D_MODEL, MISC_WIDTH - SSD_HEADS - GLA_RANK), wi.dtype)], axis=1)
    w_in_r = jnp.concatenate(
        [wi[:, o_z:o_dt], wi[:, o_q:o_a], w_misc], axis=1).astype(BF16)
    z, xbc, q, k, v, g, misc = _in_proj(x_flat, meta_tile, mix_norm_w[0][None, :], w_in_r)

    dtb = _pad_lanes(dt_bias[0][None, :].astype(F32), MISC_WIDTH)
    aneg = _pad_lanes(-jnp.exp(a_log[0].astype(F32))[None, :], MISC_WIDTH)
    dskip = jnp.repeat(d_skip[0].astype(F32), SSD_HEAD_DIM)[None, :]
    wdec = jnp.zeros((MISC_WIDTH, GLA_K_WIDTH), F32).at[SSD_HEADS:SSD_HEADS + GLA_RANK].set(w_decay_up[0])
    params = (conv_w[0], conv_b[0][None, :], dtb, aneg, dskip, ssd_norm_w[0][None, :],
              wdec.astype(BF16), b_decay[0][None, :], gla_norm_w[0][None, :])
    mixed = _mixer(z, xbc, q, k, v, g, misc, params, batch, nchunks)

    wr_hi, wr_lo = _split2(w_router[0].T.astype(F32))
    n_tiles = n // TOKEN_TILE
    n_blocks = -(-(n * TOP_K + n_tiles * N_EXPERTS * (SEG_ALIGN - 1)) // MOE_BLOCK) + N_EXPERTS
    spare = n_blocks * MOE_BLOCK
    total_rows = spare + 2 * SORT_ROWS
    zero_rows = -(-total_rows // (n_tiles * SEG_ALIGN)) * SEG_ALIGN
    h2, t_b, top_idx, gates, lrank, tile_cnt, buf0 = _post(
        x_flat, mixed, w_out[0].astype(BF16), ffn_norm_w[0][None, :],
        jnp.stack([wr_hi, wr_lo]), b_router[0][:, None], zero_rows)

    tile_cnt = tile_cnt.reshape(n_tiles, N_EXPERTS, 128)[:, :, 0]
    seg_rows = (tile_cnt + SEG_ALIGN - 1) // SEG_ALIGN * SEG_ALIGN
    counts = jnp.sum(seg_rows, axis=0)
    padded = (counts + MOE_BLOCK - 1) // MOE_BLOCK * MOE_BLOCK
    pend = jnp.cumsum(padded)
    pstart = pend - padded
    dstart = pstart[None, :] + jnp.cumsum(seg_rows, axis=0) - seg_rows
    seg_end = jnp.cumsum(seg_rows, axis=1)
    offs = seg_end - seg_rows
    block_pos = jnp.arange(total_rows // MOE_BLOCK, dtype=jnp.int32) * MOE_BLOCK
    block_expert = jnp.minimum(
        jnp.sum((pend[None, :] <= block_pos[:, None]).astype(jnp.int32), axis=1), N_EXPERTS - 1)
    n_used = (pend[-1:] // MOE_BLOCK).astype(jnp.int32)
    grow = jnp.arange(SORT_GROUPS, dtype=jnp.int32)[None, :, None] * SEG_ALIGN
    inside = (offs[:, None, :] <= grow) & (grow < seg_end[:, None, :])
    group_dst = jnp.sum(jnp.where(inside, dstart[:, None, :] + grow - offs[:, None, :], 0), axis=2)
    parity = (jnp.arange(n_tiles, dtype=jnp.int32) % 2)[:, None]
    group_dst = jnp.where(jnp.any(inside, axis=2), group_dst, spare + parity * SORT_ROWS + grow[:, :, 0])
    group_dst = group_dst.astype(jnp.int32)[:, None, :]
    offs = offs.reshape(-1).astype(jnp.int32)

    buf = _dispatch(offs, group_dst, top_idx, lrank, t_b, buf0)
    y_buf = _experts(block_expert, n_used, buf, w_gate_up[0], b_gate_up[0][:, None, :],
                     w_down[0], b_down[0][:, None, :])
    out = _combine(offs, group_dst, top_idx, lrank, gates, h2, final_norm_w[None, :], y_buf)
    return out.reshape(batch, seq, D_MODEL)
```

```python
import functools

import jax
import jax.numpy as jnp
from jax import lax
from jax.experimental import pallas as pl
from jax.experimental.pallas import tpu as pltpu

F32 = jnp.float32
BF16 = jnp.bfloat16

D_MODEL = 1024
N_META = 16
EPS = 1e-5
SSD_HEAD_DIM = 64
SSD_HEADS = 16
SSD_GROUPS = 2
SSD_STATE = 128
SSD_CONV = 4
SSD_WIDTH = SSD_HEADS * SSD_HEAD_DIM
XBC_WIDTH = SSD_WIDTH + 2 * SSD_GROUPS * SSD_STATE
GLA_HEADS = 4
GLA_KEY_DIM = 128
GLA_VAL_DIM = 256
GLA_K_WIDTH = GLA_HEADS * GLA_KEY_DIM
GLA_V_WIDTH = GLA_HEADS * GLA_VAL_DIM
GLA_RANK = 16
GLA_TAU = 16.0
GLA_SUB = 16
N_EXPERTS = 32
TOP_K = 4
D_FF = D_MODEL
SWIGLU_LIMIT = 7.0
SWIGLU_ALPHA = 1.702
MOE_BLOCK = 512

CHUNK = 128
FRONT_PAD = CHUNK - N_META
MISC_WIDTH = 128
IN_PAD_WIDTH = SSD_WIDTH + XBC_WIDTH + 2 * GLA_K_WIDTH + 2 * GLA_V_WIDTH + MISC_WIDTH
TOKEN_TILE = 256
HALF = D_MODEL // 2
NO_ROW = 1 << 12
SEG_ALIGN = 8
SORT_ROWS = -(-(TOP_K * TOKEN_TILE + N_EXPERTS * (SEG_ALIGN - 1)) // TOKEN_TILE) * TOKEN_TILE
SORT_GROUPS = SORT_ROWS // SEG_ALIGN
VMEM_LIMIT = 56 * 1024 * 1024


def _split2(x):
    hi = x.astype(BF16)
    lo = (x - hi.astype(F32)).astype(BF16)
    return hi, lo


def _split3(x):
    hi = x.astype(BF16)
    r = x - hi.astype(F32)
    mid = r.astype(BF16)
    lo = (r - mid.astype(F32)).astype(BF16)
    return hi, mid, lo


def _dot(a, b):
    return jnp.dot(a, b, preferred_element_type=F32)


def _dot_nt(a, b):
    return lax.dot_general(a, b, (((1,), (1,)), ((), ())), preferred_element_type=F32)


def _dot_tn(a, b):
    return lax.dot_general(a, b, (((0,), (0,)), ((), ())), preferred_element_type=F32)


def _silu(x):
    return x * jax.nn.sigmoid(x)


_IN_SECTIONS = (SSD_WIDTH, XBC_WIDTH, GLA_K_WIDTH, GLA_K_WIDTH, GLA_V_WIDTH, GLA_V_WIDTH, MISC_WIDTH)


def _in_proj_kernel(x_ref, meta_ref, nw_ref, w_ref, z_ref, xbc_ref, q_ref, k_ref, v_ref, g_ref, misc_ref):
    h = jnp.where(pl.program_id(0) < pl.num_programs(0) - 1, x_ref[...], meta_ref[...])
    u = h * lax.rsqrt(jnp.mean(h * h, axis=-1, keepdims=True) + EPS) * nw_ref[...]
    ub = u.astype(BF16)
    outs = (z_ref, xbc_ref, q_ref, k_ref, v_ref, g_ref, misc_ref)
    off = 0
    for o_ref, width in zip(outs, _IN_SECTIONS):
        o_ref[...] = _dot(ub, w_ref[:, off:off + width]).astype(o_ref.dtype)
        off += width


def _in_proj(x_flat, meta_tile, norm_w, w_in_r):
    tm = TOKEN_TILE
    nx = x_flat.shape[0] // tm
    rows = (nx + 1) * tm
    row_spec = lambda w: pl.BlockSpec((tm, w), lambda i: (i, 0))
    out_dtypes = (BF16,) * 6 + (F32,)
    return pl.pallas_call(
        _in_proj_kernel,
        grid=(nx + 1,),
        in_specs=[
            pl.BlockSpec((tm, D_MODEL), lambda i: (jnp.minimum(i, nx - 1), 0)),
            pl.BlockSpec((tm, D_MODEL), lambda i: (0, 0)),
            pl.BlockSpec((1, D_MODEL), lambda i: (0, 0)),
            pl.BlockSpec((D_MODEL, IN_PAD_WIDTH), lambda i: (0, 0)),
        ],
        out_specs=[row_spec(w) for w in _IN_SECTIONS],
        out_shape=[jax.ShapeDtypeStruct((rows, w), dt) for w, dt in zip(_IN_SECTIONS, out_dtypes)],
        compiler_params=pltpu.CompilerParams(
            dimension_semantics=("arbitrary",), vmem_limit_bytes=VMEM_LIMIT),
        name="in_proj",
    )(x_flat, meta_tile, norm_w, w_in_r)


def _block_rows(x, rows, width):
    parts = []
    for r in rows:
        if r is None:
            parts.append(jnp.zeros((width, x.shape[1]), x.dtype))
        else:
            parts.append(jnp.broadcast_to(x[r:r + 1, :], (width, x.shape[1])))
    return jnp.concatenate(parts, axis=0)


def _mixer_kernel(z_ref, xbc_ref, q_ref, k_ref, v_ref, g_ref, misc_ref,
                  convw_ref, convb_ref, dtb_ref, aneg_ref, dskip_ref, ssdw_ref,
                  wdec_ref, bdec_ref, glaw_ref,
                  out_ref,
                  xext_ref, sstate_ref, gstate_ref):
    c = pl.program_id(1)

    @pl.when(c == 0)
    def _():
        xext_ref[0:8, :] = jnp.zeros((8, XBC_WIDTH), F32)
        sstate_ref[...] = jnp.zeros_like(sstate_ref)
        gstate_ref[...] = jnp.zeros_like(gstate_ref)

    row = lax.broadcasted_iota(jnp.int32, (CHUNK, CHUNK), 0)
    col = lax.broadcasted_iota(jnp.int32, (CHUNK, CHUNK), 1)
    causal = row >= col
    tri = jnp.where(causal, 1.0, 0.0).astype(BF16)
    valid = jnp.logical_or(c > 0, row >= FRONT_PAD)

    xext_ref[8:8 + CHUNK, :] = xbc_ref[...].astype(F32)
    conv = convb_ref[...] + convw_ref[0:1, :] * xext_ref[5:5 + CHUNK, :]
    for kk in range(1, SSD_CONV):
        conv = conv + convw_ref[kk:kk + 1, :] * xext_ref[5 + kk:5 + kk + CHUNK, :]
    xext_ref[0:8, :] = xext_ref[CHUNK:CHUNK + 8, :]
    valid_w = jnp.logical_or(c > 0, lax.broadcasted_iota(jnp.int32, (CHUNK, XBC_WIDTH), 0) >= FRONT_PAD)
    act = jnp.where(valid_w, _silu(conv), 0.0)
    xs = act[:, :SSD_WIDTH]
    bmat = act[:, SSD_WIDTH:SSD_WIDTH + SSD_GROUPS * SSD_STATE].astype(BF16)
    cmat = act[:, SSD_WIDTH + SSD_GROUPS * SSD_STATE:].astype(BF16)

    misc = misc_ref[...]
    dt = jnp.where(valid, jax.nn.softplus(misc + dtb_ref[...]), 0.0)
    d_a = dt * aneg_ref[...]
    p0, p1, p2 = _split3(d_a)
    a_cs = _dot(tri, p0) + _dot(tri, p1) + _dot(tri, p2)
    a_cs_t = a_cs.T

    lo_half = col < SSD_HEAD_DIM
    hg = SSD_HEADS // SSD_GROUPS
    pairs_per_group = hg // 2
    y_parts = []
    for gi in range(SSD_GROUPS):
        b_g = bmat[:, gi * SSD_STATE:(gi + 1) * SSD_STATE]
        c_g = cmat[:, gi * SSD_STATE:(gi + 1) * SSD_STATE]
        c_g32 = c_g.astype(F32)
        cb = _dot_nt(c_g, b_g)
        w_parts = []
        cd_parts = []
        for pj in range(pairs_per_group):
            h0 = gi * hg + 2 * pj
            h1 = h0 + 1
            lanes = slice(h0 * SSD_HEAD_DIM, (h1 + 1) * SSD_HEAD_DIM)
            ab0 = jnp.broadcast_to(a_cs[:, h0:h0 + 1], (CHUNK, CHUNK))
            ab1 = jnp.broadcast_to(a_cs[:, h1:h1 + 1], (CHUNK, CHUNK))
            db0 = jnp.broadcast_to(dt[:, h0:h0 + 1], (CHUNK, CHUNK))
            db1 = jnp.broadcast_to(dt[:, h1:h1 + 1], (CHUNK, CHUNK))
            l0 = jnp.exp(jnp.where(causal, ab0 - a_cs_t[h0:h0 + 1, :], -jnp.inf))
            l1 = jnp.exp(jnp.where(causal, ab1 - a_cs_t[h1:h1 + 1, :], -jnp.inf))
            lhs = jnp.concatenate(
                [(cb * l0).astype(BF16), (c_g32 * jnp.exp(ab0)).astype(BF16),
                 (cb * l1).astype(BF16), (c_g32 * jnp.exp(ab1)).astype(BF16)], axis=1)
            xs_p = xs[:, lanes]
            dtx = xs_p * jnp.where(lo_half, db0, db1)
            st = sstate_ref[gi, :, pj * CHUNK:(pj + 1) * CHUNK]
            rhs = jnp.concatenate(
                [jnp.where(lo_half, dtx, 0.0).astype(BF16), jnp.where(lo_half, st, 0.0).astype(BF16),
                 jnp.where(lo_half, 0.0, dtx).astype(BF16), jnp.where(lo_half, 0.0, st).astype(BF16)],
                axis=0)
            y_p = _dot(lhs, rhs) + dskip_ref[:, lanes] * xs_p
            y_parts.append(y_p)
            ae = jnp.where(lo_half, ab0, ab1)
            a_last = ae[CHUNK - 1:CHUNK, :]
            w_parts.append((dtx * jnp.exp(a_last - ae)).astype(BF16))
            cd_parts.append(jnp.exp(a_last))
        upd = _dot_tn(b_g, jnp.concatenate(w_parts, axis=1))
        sstate_ref[gi] = jnp.concatenate(cd_parts, axis=1) * sstate_ref[gi] + upd
    y = jnp.concatenate(y_parts, axis=1)
    y = y * _silu(z_ref[...].astype(F32))
    gsz = SSD_WIDTH // SSD_GROUPS
    for gi in range(SSD_GROUPS):
        yg = y[:, gi * gsz:(gi + 1) * gsz]
        yn = yg * lax.rsqrt(jnp.mean(yg * yg, axis=-1, keepdims=True) + EPS)
        out_ref[:, gi * gsz:(gi + 1) * gsz] = (yn * ssdw_ref[:, gi * gsz:(gi + 1) * gsz]).astype(out_ref.dtype)

    m0, m1 = _split2(misc)
    pre = _dot(m0, wdec_ref[...]) + _dot(m1, wdec_ref[...]) + bdec_ref[...]
    log_a = jax.nn.log_sigmoid(pre) * (1.0 / GLA_TAU)
    g0, g1 = _split2(log_a)
    gcum = _dot(tri, g0) + _dot(tri, g1)
    roww = lax.broadcasted_iota(jnp.int32, (CHUNK, GLA_K_WIDTH), 0)
    qf = q_ref[...].astype(F32) * (GLA_KEY_DIM ** -0.5)
    kf = k_ref[...].astype(F32)
    nsub = CHUNK // GLA_SUB
    g_start = _block_rows(gcum, [None] + [GLA_SUB * i - 1 for i in range(1, nsub)], GLA_SUB)
    e0 = gcum - g_start
    q_lv = [(qf * jnp.exp(e0)).astype(BF16)]
    k_lv = [(kf * jnp.exp(-e0)).astype(BF16)]
    shifts = (4, 5, 6)
    for sh in shifts:
        b = 1 << sh
        bound = _block_rows(gcum, [2 * b * i + b - 1 for i in range(CHUNK // (2 * b))], 2 * b)
        second = ((roww >> sh) & 1) == 1
        x_l = jnp.exp(jnp.where(second, gcum - bound, bound - gcum))
        q_lv.append((qf * x_l).astype(BF16))
        k_lv.append((kf * x_l).astype(BF16))
    masks = [jnp.logical_and((row >> 4) == (col >> 4), causal)]
    for sh in shifts:
        same = (row >> (sh + 1)) == (col >> (sh + 1))
        m = jnp.logical_and(same, jnp.logical_and(((row >> sh) & 1) == 1, ((col >> sh) & 1) == 0))
        masks.append(m)
    g_last = gcum[CHUNK - 1:CHUNK, :]
    q_in = (qf * jnp.exp(gcum)).astype(BF16)
    k_end = (kf * jnp.exp(g_last - gcum)).astype(BF16)
    dec = jnp.exp(g_last)
    vb = v_ref[...]
    for hh in range(GLA_HEADS):
        kl = slice(hh * GLA_KEY_DIM, (hh + 1) * GLA_KEY_DIM)
        vl = slice(hh * GLA_VAL_DIM, (hh + 1) * GLA_VAL_DIM)
        scores = jnp.zeros((CHUNK, CHUNK), F32)
        for lv in range(len(masks)):
            scores = scores + jnp.where(masks[lv], _dot_nt(q_lv[lv][:, kl], k_lv[lv][:, kl]), 0.0)
        v_h = vb[:, vl]
        s_t = gstate_ref[hh]
        o = _dot(scores.astype(BF16), v_h) + _dot_nt(q_in[:, kl], s_t.astype(BF16))
        gstate_ref[hh] = dec[:, kl] * s_t + _dot_tn(v_h, k_end[:, kl])
        o = o * lax.rsqrt(jnp.mean(o * o, axis=-1, keepdims=True) + EPS) * glaw_ref[...]
        o = o * _silu(g_ref[:, vl].astype(F32))
        out_ref[:, SSD_WIDTH + hh * GLA_VAL_DIM:SSD_WIDTH + (hh + 1) * GLA_VAL_DIM] = o.astype(out_ref.dtype)


def _mixer(z, xbc, q, k, v, g, misc, params, batch, nchunks):
    widths = (SSD_WIDTH, XBC_WIDTH, GLA_K_WIDTH, GLA_K_WIDTH, GLA_V_WIDTH, GLA_V_WIDTH, MISC_WIDTH)
    meta_block = z.shape[0] // CHUNK - 1
    row_spec = lambda w: pl.BlockSpec(
        (CHUNK, w), lambda b, c: (jnp.where(c == 0, meta_block, b * (nchunks - 1) + c - 1), 0))
    par_spec = lambda p: pl.BlockSpec(p.shape, lambda b, c: (0,) * p.ndim)
    n_out = batch * (nchunks - 1) * CHUNK
    return pl.pallas_call(
        _mixer_kernel,
        grid=(batch, nchunks),
        in_specs=[row_spec(w) for w in widths] + [par_spec(p) for p in params],
        out_specs=pl.BlockSpec((CHUNK, SSD_WIDTH + GLA_V_WIDTH),
                               lambda b, c: (b * (nchunks - 1) + jnp.maximum(c - 1, 0), 0)),
        out_shape=jax.ShapeDtypeStruct((n_out, SSD_WIDTH + GLA_V_WIDTH), BF16),
        scratch_shapes=[
            pltpu.VMEM((CHUNK + 8, XBC_WIDTH), F32),
            pltpu.VMEM((SSD_GROUPS, SSD_STATE, SSD_WIDTH // SSD_GROUPS), F32),
            pltpu.VMEM((GLA_HEADS, GLA_VAL_DIM, GLA_KEY_DIM), F32),
        ],
        compiler_params=pltpu.CompilerParams(
            dimension_semantics=("arbitrary", "arbitrary"), vmem_limit_bytes=VMEM_LIMIT),
        name="mixer",
    )(z, xbc, q, k, v, g, misc, *params)


def _post_kernel(x_ref, mixed_ref, wout_ref, fnw_ref, wrt_ref, br_ref,
                 h2_ref, t_ref, idx_ref, gate_ref, lrank_ref, cnt_ref, zero_ref, tprev_ref):
    @pl.when(pl.program_id(0) == 0)
    def _():
        tprev_ref[...] = jnp.zeros_like(tprev_ref)

    t_prev = tprev_ref[...]
    h2 = x_ref[...] + _dot(mixed_ref[...], wout_ref[...])
    h2_ref[...] = h2
    t = h2 * lax.rsqrt(jnp.mean(h2 * h2, axis=-1, keepdims=True) + EPS) * fnw_ref[...]
    t_ref[...] = t.astype(BF16)
    tprev_ref[...] = t

    t0, t1 = _split2(t_prev)
    w0 = wrt_ref[0]
    w1 = wrt_ref[1]
    logits = _dot_nt(w0, t0) + _dot_nt(w0, t1) + _dot_nt(w1, t0) + br_ref[...]
    e_iota = lax.broadcasted_iota(jnp.int32, logits.shape, 0)
    work = logits
    sel_any = jnp.zeros(logits.shape, F32)
    tops = []
    idxs = []
    onehots = []
    for _ in range(TOP_K):
        m = jnp.max(work, axis=0, keepdims=True)
        idx = jnp.min(jnp.where(work == m, e_iota, N_EXPERTS), axis=0, keepdims=True)
        hit = e_iota == idx
        tops.append(m)
        idxs.append(idx)
        onehots.append(hit)
        sel_any = sel_any + jnp.where(hit, 1.0, 0.0)
        work = jnp.where(hit, -jnp.inf, work)
    exps = [jnp.exp(tv - tops[0]) for tv in tops]
    denom = exps[0] + exps[1] + exps[2] + exps[3]
    tile = logits.shape[1]
    pad_rows = 8 - TOP_K
    gate_ref[...] = jnp.concatenate([e / denom for e in exps] + [jnp.zeros((pad_rows, tile), F32)], axis=0)
    idx_ref[...] = jnp.concatenate(idxs + [jnp.zeros((pad_rows, tile), jnp.int32)], axis=0)

    r_i = lax.broadcasted_iota(jnp.int32, (tile, tile), 0)
    c_i = lax.broadcasted_iota(jnp.int32, (tile, tile), 1)
    upper = jnp.where(r_i < c_i, 1.0, 0.0).astype(BF16)
    before = _dot(sel_any.astype(BF16), upper)
    ranks = [jnp.sum(jnp.where(h, before, 0.0), axis=0, keepdims=True) for h in onehots]
    ranks.append(jnp.full((pad_rows, tile), float(NO_ROW), F32))
    lrank_ref[...] = jnp.concatenate(ranks, axis=0).astype(jnp.int32)
    cnt = jnp.sum(sel_any, axis=1, keepdims=True)
    cnt_ref[...] = jnp.broadcast_to(cnt, cnt_ref.shape).astype(jnp.int32)
    zero_ref[...] = jnp.zeros_like(zero_ref)


def _post(x_flat, mixed, w_out_b, ffn_norm_w, w_router_t, b_router_col, zero_rows):
    n = x_flat.shape[0]
    tt = TOKEN_TILE
    n_tiles = n // tt
    const = lambda shape: pl.BlockSpec(shape, lambda i: (0,) * len(shape))
    cur = lambda i: jnp.minimum(i, n_tiles - 1)
    prev = lambda i: jnp.maximum(i - 1, 0)
    return pl.pallas_call(
        _post_kernel,
        grid=(n_tiles + 1,),
        in_specs=[
            pl.BlockSpec((tt, D_MODEL), lambda i: (cur(i), 0)),
            pl.BlockSpec((tt, SSD_WIDTH + GLA_V_WIDTH), lambda i: (cur(i), 0)),
            const(w_out_b.shape),
            const((1, D_MODEL)),
            const(w_router_t.shape),
            const((N_EXPERTS, 1)),
        ],
        out_specs=[
            pl.BlockSpec((tt, D_MODEL), lambda i: (cur(i), 0)),
            pl.BlockSpec((tt, D_MODEL), lambda i: (cur(i), 0)),
            pl.BlockSpec((8, tt), lambda i: (0, prev(i))),
            pl.BlockSpec((8, tt), lambda i: (0, prev(i))),
            pl.BlockSpec((8, tt), lambda i: (0, prev(i))),
            pl.BlockSpec((N_EXPERTS, 128), lambda i: (prev(i), 0)),
            pl.BlockSpec((zero_rows, HALF), lambda i: (i, 0)),
        ],
        out_shape=[
            jax.ShapeDtypeStruct((n, D_MODEL), F32),
            jax.ShapeDtypeStruct((n, D_MODEL), BF16),
            jax.ShapeDtypeStruct((8, n), jnp.int32),
            jax.ShapeDtypeStruct((8, n), F32),
            jax.ShapeDtypeStruct((8, n), jnp.int32),
            jax.ShapeDtypeStruct((n_tiles * N_EXPERTS, 128), jnp.int32),
            jax.ShapeDtypeStruct(((n_tiles + 1) * zero_rows, HALF), jnp.uint32),
        ],
        scratch_shapes=[pltpu.VMEM((tt, D_MODEL), F32)],
        compiler_params=pltpu.CompilerParams(
            dimension_semantics=("arbitrary",), vmem_limit_bytes=VMEM_LIMIT),
        name="post",
    )(x_flat, mixed, w_out_b, ffn_norm_w, w_router_t, b_router_col)


def _pack_pairs(x):
    lo = pltpu.bitcast(x[:, :HALF], jnp.uint32) >> 16
    hi = pltpu.bitcast(x[:, HALF:], jnp.uint32) & jnp.uint32(0xFFFF0000)
    return lo | hi


def _unpack_pairs(w):
    lo = pltpu.bitcast(w << 16, F32).astype(BF16)
    hi = pltpu.bitcast(w & jnp.uint32(0xFFFF0000), F32).astype(BF16)
    return lo, hi


def _sorted_positions(idx, lrank, offs_ref, base):
    pos = lrank
    for e in range(N_EXPERTS):
        pos = pos + jnp.where(idx == e, offs_ref[base + e], 0)
    return pos


def _group_copies(dst_ref, make_copy):
    return [make_copy(g * SEG_ALIGN, pl.multiple_of(dst_ref[0, 0, g], SEG_ALIGN))
            for g in range(SORT_GROUPS)]


def _dispatch_kernel(offs_ref, idx_ref, lrank_ref, t_ref, dst_ref, dst_prev_ref, buf_in_ref, buf_ref,
                     sorted_ref, sem):
    del buf_in_ref
    i = pl.program_id(0)
    slot = i % 2
    base = i * N_EXPERTS
    tt = t_ref.shape[0]
    pos = _sorted_positions(idx_ref[...], lrank_ref[...], offs_ref, base)
    t = t_ref[...]
    for r0 in range(0, SORT_ROWS, tt):
        prow = lax.broadcasted_iota(jnp.int32, (tt, tt), 0) + r0
        perm = jnp.zeros((tt, tt), F32)
        for kk in range(TOP_K):
            perm = jnp.where(prow == pos[kk:kk + 1, :], 1.0, perm)
        sorted_ref[slot, r0:r0 + tt, :] = _pack_pairs(_dot(perm.astype(BF16), t))

    def copies(table_ref, which):
        def make_copy(local, glob):
            return pltpu.make_async_copy(
                sorted_ref.at[which, pl.ds(local, SEG_ALIGN), :],
                buf_ref.at[pl.ds(glob, SEG_ALIGN), :], sem.at[which])
        return _group_copies(table_ref, make_copy)

    for cp in copies(dst_ref, slot):
        cp.start()

    @pl.when(i > 0)
    def _():
        for cp in copies(dst_prev_ref, 1 - slot):
            cp.wait()

    @pl.when(i == pl.num_programs(0) - 1)
    def _():
        for cp in copies(dst_ref, slot):
            cp.wait()


def _dispatch(offs, group_dst, top_idx, lrank, t_b, buf0):
    n = t_b.shape[0]
    tt = TOKEN_TILE
    table = lambda index: pl.BlockSpec((1, 1, SORT_GROUPS), index, memory_space=pltpu.SMEM)
    grid_spec = pltpu.PrefetchScalarGridSpec(
        num_scalar_prefetch=1,
        grid=(n // tt,),
        in_specs=[
            pl.BlockSpec((8, tt), lambda i, *_: (0, i)),
            pl.BlockSpec((8, tt), lambda i, *_: (0, i)),
            pl.BlockSpec((tt, D_MODEL), lambda i, *_: (i, 0)),
            table(lambda i, *_: (i, 0, 0)),
            table(lambda i, *_: (jnp.maximum(i - 1, 0), 0, 0)),
            pl.BlockSpec(memory_space=pl.ANY),
        ],
        out_specs=pl.BlockSpec(memory_space=pl.ANY),
        scratch_shapes=[pltpu.VMEM((2, SORT_ROWS, HALF), jnp.uint32), pltpu.SemaphoreType.DMA((2,))],
    )
    return pl.pallas_call(
        _dispatch_kernel,
        grid_spec=grid_spec,
        out_shape=jax.ShapeDtypeStruct(buf0.shape, jnp.uint32),
        input_output_aliases={6: 0},
        compiler_params=pltpu.CompilerParams(
            dimension_semantics=("arbitrary",), vmem_limit_bytes=VMEM_LIMIT),
        name="dispatch",
    )(offs, top_idx, lrank, t_b, group_dst, group_dst, buf0)


def _expert_kernel(be_ref, nused_ref, xp_ref, wgu_ref, bgu_ref, wd_ref, bd_ref, y_ref,
                   wgu_b_ref, wd_b_ref):
    i = pl.program_id(0)
    used = i < nused_ref[0]

    @pl.when(jnp.logical_not(used))
    def _():
        y_ref[...] = jnp.zeros_like(y_ref)

    new_expert = jnp.logical_or(i == 0, be_ref[i] != be_ref[jnp.maximum(i - 1, 0)])

    @pl.when(jnp.logical_and(used, new_expert))
    def _():
        rows = 128
        for r0 in range(0, D_MODEL, rows):
            wgu_b_ref[r0:r0 + rows, :] = wgu_ref[r0:r0 + rows, :].astype(BF16)
        for r0 in range(0, D_FF, rows):
            wd_b_ref[r0:r0 + rows, :] = wd_ref[r0:r0 + rows, :].astype(BF16)

    @pl.when(used)
    def _():
        x_lo, x_hi = _unpack_pairs(xp_ref[...])
        hgu = _dot(x_lo, wgu_b_ref[:HALF, :]) + _dot(x_hi, wgu_b_ref[HALF:, :]) + bgu_ref[...]
        gate = jnp.minimum(hgu[:, :D_FF], SWIGLU_LIMIT)
        up = jnp.clip(hgu[:, D_FF:], -SWIGLU_LIMIT, SWIGLU_LIMIT)
        act = gate * jax.nn.sigmoid(SWIGLU_ALPHA * gate)
        y = _dot(((up + 1.0) * act).astype(BF16), wd_b_ref[...]) + bd_ref[...]
        y_ref[...] = _pack_pairs(y.astype(BF16).astype(F32))


def _experts(block_expert, n_used, buf, w_gu, b_gu, w_d, b_d):
    n_blocks = block_expert.shape[0]
    rows = n_blocks * MOE_BLOCK
    row_map = lambda i, be, nu: (jnp.minimum(i, nu[0] - 1), 0)
    w_map = lambda i, be, nu: (be[jnp.minimum(i, nu[0] - 1)], 0, 0)
    grid_spec = pltpu.PrefetchScalarGridSpec(
        num_scalar_prefetch=2,
        grid=(n_blocks,),
        in_specs=[
            pl.BlockSpec((MOE_BLOCK, HALF), row_map),
            pl.BlockSpec((None, D_MODEL, 2 * D_FF), w_map),
            pl.BlockSpec((None, 1, 2 * D_FF), w_map),
            pl.BlockSpec((None, D_FF, D_MODEL), w_map),
            pl.BlockSpec((None, 1, D_MODEL), w_map),
        ],
        out_specs=pl.BlockSpec((MOE_BLOCK, HALF), lambda i, be, nu: (i, 0)),
        scratch_shapes=[pltpu.VMEM((D_MODEL, 2 * D_FF), BF16), pltpu.VMEM((D_FF, D_MODEL), BF16)],
    )
    return pl.pallas_call(
        _expert_kernel,
        grid_spec=grid_spec,
        out_shape=jax.ShapeDtypeStruct((rows, HALF), jnp.uint32),
        compiler_params=pltpu.CompilerParams(
            dimension_semantics=("arbitrary",), vmem_limit_bytes=VMEM_LIMIT),
        name="experts",
    )(block_expert, n_used, buf, w_gu, b_gu, w_d, b_d)


def _combine_kernel(offs_ref, idx_ref, lrank_ref, gate_ref, h2_ref, fw_ref, dst_ref, dst_next_ref,
                    y_hbm_ref, out_ref, ys_ref, sem):
    i = pl.program_id(0)
    slot = i % 2
    base = i * N_EXPERTS
    tt = h2_ref.shape[0]

    def copies(table_ref, which):
        def make_copy(local, glob):
            return pltpu.make_async_copy(
                y_hbm_ref.at[pl.ds(glob, SEG_ALIGN), :],
                ys_ref.at[which, pl.ds(local, SEG_ALIGN), :], sem.at[which])
        return _group_copies(table_ref, make_copy)

    @pl.when(i == 0)
    def _():
        for cp in copies(dst_ref, slot):
            cp.start()

    @pl.when(i < pl.num_programs(0) - 1)
    def _():
        for cp in copies(dst_next_ref, 1 - slot):
            cp.start()

    pos = _sorted_positions(idx_ref[...], lrank_ref[...], offs_ref, base).astype(F32)
    zpad = jnp.zeros((128 - 8, tt), F32)
    pos_c = jnp.concatenate([pos, zpad], axis=0).T
    gate_c = jnp.concatenate([gate_ref[...], zpad], axis=0).T
    lane = lax.broadcasted_iota(jnp.int32, (tt, SORT_ROWS), 1).astype(F32)
    pg = jnp.zeros((tt, SORT_ROWS), F32)
    for kk in range(TOP_K):
        pg = jnp.where(lane == pos_c[:, kk:kk + 1], gate_c[:, kk:kk + 1], pg)
    p_hi, p_lo = _split2(pg)

    for cp in copies(dst_ref, slot):
        cp.wait()
    y_lo, y_hi = _unpack_pairs(ys_ref[slot])
    ffn = jnp.concatenate(
        [_dot(p_hi, y_lo) + _dot(p_lo, y_lo), _dot(p_hi, y_hi) + _dot(p_lo, y_hi)], axis=1)
    acc = h2_ref[...] + ffn
    out_ref[...] = acc * lax.rsqrt(jnp.mean(acc * acc, axis=-1, keepdims=True) + EPS) * fw_ref[...]


def _combine(offs, group_dst, top_idx, lrank, gates, h2, final_norm_w, y_buf):
    n = h2.shape[0]
    tt = TOKEN_TILE
    grid_spec = pltpu.PrefetchScalarGridSpec(
        num_scalar_prefetch=1,
        grid=(n // tt,),
        in_specs=[
            pl.BlockSpec((8, tt), lambda i, *_: (0, i)),
            pl.BlockSpec((8, tt), lambda i, *_: (0, i)),
            pl.BlockSpec((8, tt), lambda i, *_: (0, i)),
            pl.BlockSpec((tt, D_MODEL), lambda i, *_: (i, 0)),
            pl.BlockSpec((1, D_MODEL), lambda i, *_: (0, 0)),
            pl.BlockSpec((1, 1, SORT_GROUPS), lambda i, *_: (i, 0, 0), memory_space=pltpu.SMEM),
            pl.BlockSpec((1, 1, SORT_GROUPS), lambda i, *_: (jnp.minimum(i + 1, n // tt - 1), 0, 0),
                         memory_space=pltpu.SMEM),
            pl.BlockSpec(memory_space=pl.ANY),
        ],
        out_specs=pl.BlockSpec((tt, D_MODEL), lambda i, *_: (i, 0)),
        scratch_shapes=[pltpu.VMEM((2, SORT_ROWS, HALF), jnp.uint32), pltpu.SemaphoreType.DMA((2,))],
    )
    return pl.pallas_call(
        _combine_kernel,
        grid_spec=grid_spec,
        out_shape=jax.ShapeDtypeStruct((n, D_MODEL), F32),
        compiler_params=pltpu.CompilerParams(
            dimension_semantics=("arbitrary",), vmem_limit_bytes=VMEM_LIMIT),
        name="combine",
    )(offs, top_idx, lrank, gates, h2, final_norm_w, group_dst, group_dst, y_buf)


def _pad_lanes(v, width):
    return jnp.pad(v, ((0, 0), (0, width - v.shape[1])))


def kernel(x, meta_tokens, mix_norm_w, w_in, conv_w, conv_b, dt_bias, a_log, d_skip, ssd_norm_w,
           w_decay_up, b_decay, gla_norm_w, w_out, ffn_norm_w, w_router, b_router, w_gate_up,
           b_gate_up, w_down, b_down, final_norm_w):
    batch, seq, d = x.shape
    assert d == D_MODEL and seq % TOKEN_TILE == 0
    assert mix_norm_w.shape[0] == 1, "single-layer block"
    nchunks = (FRONT_PAD + N_META + seq) // CHUNK

    n = batch * seq
    x_flat = x.reshape(n, D_MODEL)
    meta_tile = jnp.concatenate(
        [jnp.zeros((TOKEN_TILE - N_META, D_MODEL), x.dtype), meta_tokens.astype(x.dtype)], axis=0)

    wi = w_in[0]
    o_z, o_xbc = 0, SSD_WIDTH
    o_dt = o_xbc + XBC_WIDTH
    o_q = o_dt + SSD_HEADS
    o_k = o_q + GLA_K_WIDTH
    o_v = o_k + GLA_K_WIDTH
    o_g = o_v + GLA_V_WIDTH
    o_a = o_g + GLA_V_WIDTH
    w_misc = jnp.concatenate(
        [wi[:, o_dt:o_dt + SSD_HEADS], wi[:, o_a:o_a + GLA_RANK],
         jnp.zeros((D_MODEL, MISC_WIDTH - SSD_HEADS - GLA_RANK), wi.dtype)], axis=1)
    w_in_r = jnp.concatenate(
        [wi[:, o_z:o_dt], wi[:, o_q:o_a], w_misc], axis=1).astype(BF16)
    z, xbc, q, k, v, g, misc = _in_proj(x_flat, meta_tile, mix_norm_w[0][None, :], w_in_r)

    dtb = _pad_lanes(dt_bias[0][None, :].astype(F32), MISC_WIDTH)
    aneg = _pad_lanes(-jnp.exp(a_log[0].astype(F32))[None, :], MISC_WIDTH)
    dskip = jnp.repeat(d_skip[0].astype(F32), SSD_HEAD_DIM)[None, :]
    wdec = jnp.zeros((MISC_WIDTH, GLA_K_WIDTH), F32).at[SSD_HEADS:SSD_HEADS + GLA_RANK].set(w_decay_up[0])
    params = (conv_w[0], conv_b[0][None, :], dtb, aneg, dskip, ssd_norm_w[0][None, :],
              wdec.astype(BF16), b_decay[0][None, :], gla_norm_w[0][None, :])
    mixed = _mixer(z, xbc, q, k, v, g, misc, params, batch, nchunks)

    wr_hi, wr_lo = _split2(w_router[0].T.astype(F32))
    n_tiles = n // TOKEN_TILE
    n_blocks = -(-(n * TOP_K + n_tiles * N_EXPERTS * (SEG_ALIGN - 1)) // MOE_BLOCK) + N_EXPERTS
    spare = n_blocks * MOE_BLOCK
    total_rows = spare + 2 * SORT_ROWS
    zero_rows = -(-total_rows // ((n_tiles + 1) * SEG_ALIGN)) * SEG_ALIGN
    h2, t_b, top_idx, gates, lrank, tile_cnt, buf0 = _post(
        x_flat, mixed, w_out[0].astype(BF16), ffn_norm_w[0][None, :],
        jnp.stack([wr_hi, wr_lo]), b_router[0][:, None], zero_rows)

    tile_cnt = tile_cnt.reshape(n_tiles, N_EXPERTS, 128)[:, :, 0]
    seg_rows = (tile_cnt + SEG_ALIGN - 1) // SEG_ALIGN * SEG_ALIGN
    counts = jnp.sum(seg_rows, axis=0)
    padded = (counts + MOE_BLOCK - 1) // MOE_BLOCK * MOE_BLOCK
    pend = jnp.cumsum(padded)
    pstart = pend - padded
    dstart = pstart[None, :] + jnp.cumsum(seg_rows, axis=0) - seg_rows
    seg_end = jnp.cumsum(seg_rows, axis=1)
    offs = seg_end - seg_rows
    block_pos = jnp.arange(total_rows // MOE_BLOCK, dtype=jnp.int32) * MOE_BLOCK
    block_expert = jnp.minimum(
        jnp.sum((pend[None, :] <= block_pos[:, None]).astype(jnp.int32), axis=1), N_EXPERTS - 1)
    n_used = (pend[-1:] // MOE_BLOCK).astype(jnp.int32)
    grow = jnp.arange(SORT_GROUPS, dtype=jnp.int32)[None, :, None] * SEG_ALIGN
    inside = (offs[:, None, :] <= grow) & (grow < seg_end[:, None, :])
    group_dst = jnp.sum(jnp.where(inside, dstart[:, None, :] + grow - offs[:, None, :], 0), axis=2)
    parity = (jnp.arange(n_tiles, dtype=jnp.int32) % 2)[:, None]
    group_dst = jnp.where(jnp.any(inside, axis=2), group_dst, spare + parity * SORT_ROWS + grow[:, :, 0])
    group_dst = group_dst.astype(jnp.int32)[:, None, :]
    offs = offs.reshape(-1).astype(jnp.int32)

    buf = _dispatch(offs, group_dst, top_idx, lrank, t_b, buf0)
    y_buf = _experts(block_expert, n_used, buf, w_gate_up[0], b_gate_up[0][:, None, :],
                     w_down[0], b_down[0][:, None, :])
    out = _combine(offs, group_dst, top_idx, lrank, gates, h2, final_norm_w[None, :], y_buf)
    return out.reshape(batch, seq, D_MODEL)
```

```python
import functools

import jax
import jax.numpy as jnp
from jax import lax
from jax.experimental import pallas as pl
from jax.experimental.pallas import tpu as pltpu

F32 = jnp.float32
BF16 = jnp.bfloat16

D_MODEL = 1024
N_META = 16
EPS = 1e-5
SSD_HEAD_DIM = 64
SSD_HEADS = 16
SSD_GROUPS = 2
SSD_STATE = 128
SSD_CONV = 4
SSD_WIDTH = SSD_HEADS * SSD_HEAD_DIM
XBC_WIDTH = SSD_WIDTH + 2 * SSD_GROUPS * SSD_STATE
GLA_HEADS = 4
GLA_KEY_DIM = 128
GLA_VAL_DIM = 256
GLA_K_WIDTH = GLA_HEADS * GLA_KEY_DIM
GLA_V_WIDTH = GLA_HEADS * GLA_VAL_DIM
GLA_RANK = 16
GLA_TAU = 16.0
GLA_SUB = 16
N_EXPERTS = 32
TOP_K = 4
D_FF = D_MODEL
SWIGLU_LIMIT = 7.0
SWIGLU_ALPHA = 1.702
MOE_BLOCK = 512

CHUNK = 128
FRONT_PAD = CHUNK - N_META
MISC_WIDTH = 128
IN_PAD_WIDTH = SSD_WIDTH + XBC_WIDTH + 2 * GLA_K_WIDTH + 2 * GLA_V_WIDTH + MISC_WIDTH
TOKEN_TILE = 256
HALF = D_MODEL // 2
NO_ROW = 1 << 12
SEG_ALIGN = 8
SORT_ROWS = -(-(TOP_K * TOKEN_TILE + N_EXPERTS * (SEG_ALIGN - 1)) // TOKEN_TILE) * TOKEN_TILE
SORT_GROUPS = SORT_ROWS // SEG_ALIGN
VMEM_LIMIT = 56 * 1024 * 1024


def _split2(x):
    hi = x.astype(BF16)
    lo = (x - hi.astype(F32)).astype(BF16)
    return hi, lo


def _split3(x):
    hi = x.astype(BF16)
    r = x - hi.astype(F32)
    mid = r.astype(BF16)
    lo = (r - mid.astype(F32)).astype(BF16)
    return hi, mid, lo


def _dot(a, b):
    return jnp.dot(a, b, preferred_element_type=F32)


def _dot_nt(a, b):
    return lax.dot_general(a, b, (((1,), (1,)), ((), ())), preferred_element_type=F32)


def _dot_tn(a, b):
    return lax.dot_general(a, b, (((0,), (0,)), ((), ())), preferred_element_type=F32)


def _silu(x):
    return x * jax.nn.sigmoid(x)


_IN_SECTIONS = (SSD_WIDTH, XBC_WIDTH, GLA_K_WIDTH, GLA_K_WIDTH, GLA_V_WIDTH, GLA_V_WIDTH, MISC_WIDTH)
_GATE_SECTIONS = (0, 5)


def _in_proj_kernel(x_ref, meta_ref, nw_ref, w_ref, z_ref, xbc_ref, q_ref, k_ref, v_ref, g_ref, misc_ref):
    h = jnp.where(pl.program_id(0) < pl.num_programs(0) - 1, x_ref[...], meta_ref[...])
    u = h * lax.rsqrt(jnp.mean(h * h, axis=-1, keepdims=True) + EPS) * nw_ref[...]
    ub = u.astype(BF16)
    outs = (z_ref, xbc_ref, q_ref, k_ref, v_ref, g_ref, misc_ref)
    off = 0
    for section, (o_ref, width) in enumerate(zip(outs, _IN_SECTIONS)):
        p = _dot(ub, w_ref[:, off:off + width])
        if section in _GATE_SECTIONS:
            p = _silu(p)
        o_ref[...] = p.astype(o_ref.dtype)
        off += width


def _in_proj(x_flat, meta_tile, norm_w, w_in_r):
    tm = TOKEN_TILE
    nx = x_flat.shape[0] // tm
    rows = (nx + 1) * tm
    row_spec = lambda w: pl.BlockSpec((tm, w), lambda i: (i, 0))
    out_dtypes = (BF16,) * 6 + (F32,)
    return pl.pallas_call(
        _in_proj_kernel,
        grid=(nx + 1,),
        in_specs=[
            pl.BlockSpec((tm, D_MODEL), lambda i: (jnp.minimum(i, nx - 1), 0)),
            pl.BlockSpec((tm, D_MODEL), lambda i: (0, 0)),
            pl.BlockSpec((1, D_MODEL), lambda i: (0, 0)),
            pl.BlockSpec((D_MODEL, IN_PAD_WIDTH), lambda i: (0, 0)),
        ],
        out_specs=[row_spec(w) for w in _IN_SECTIONS],
        out_shape=[jax.ShapeDtypeStruct((rows, w), dt) for w, dt in zip(_IN_SECTIONS, out_dtypes)],
        compiler_params=pltpu.CompilerParams(
            dimension_semantics=("arbitrary",), vmem_limit_bytes=VMEM_LIMIT),
        name="in_proj",
    )(x_flat, meta_tile, norm_w, w_in_r)


def _block_rows(x, rows, width):
    parts = []
    for r in rows:
        if r is None:
            parts.append(jnp.zeros((width, x.shape[1]), x.dtype))
        else:
            parts.append(jnp.broadcast_to(x[r:r + 1, :], (width, x.shape[1])))
    return jnp.concatenate(parts, axis=0)


N_SEQ_INPUTS = 7
N_MIXER_PARAMS = 9


def _mixer_kernel(*refs, nseq):
    seq_refs = [refs[s * N_SEQ_INPUTS:(s + 1) * N_SEQ_INPUTS] for s in range(nseq)]
    rest = refs[nseq * N_SEQ_INPUTS:]
    param_refs = rest[:N_MIXER_PARAMS]
    out_ref, xext_ref, sstate_ref, gstate_ref = rest[N_MIXER_PARAMS:]
    c = pl.program_id(1)

    @pl.when(c == 0)
    def _():
        xext_ref[:, 0:8, :] = jnp.zeros((nseq, 8, XBC_WIDTH), BF16)
        sstate_ref[...] = jnp.zeros_like(sstate_ref)
        gstate_ref[...] = jnp.zeros_like(gstate_ref)

    for s in range(nseq):
        _mixer_chunk(c, *seq_refs[s], *param_refs, out_ref.at[s], xext_ref.at[s], sstate_ref.at[s],
                     gstate_ref.at[s])


def _mixer_chunk(c, z_ref, xbc_ref, q_ref, k_ref, v_ref, g_ref, misc_ref,
                 convw_ref, convb_ref, dtb_ref, aneg_ref, dskip_ref, ssdw_ref,
                 wdec_ref, bdec_ref, glaw_ref,
                 out_ref,
                 xext_ref, sstate_ref, gstate_ref):

    row = lax.broadcasted_iota(jnp.int32, (CHUNK, CHUNK), 0)
    col = lax.broadcasted_iota(jnp.int32, (CHUNK, CHUNK), 1)
    causal = row >= col
    tri = jnp.where(causal, 1.0, 0.0).astype(BF16)
    valid = jnp.logical_or(c > 0, row >= FRONT_PAD)

    xcur = xbc_ref[...]
    xext_ref[8:8 + CHUNK, :] = xcur
    xall = xext_ref[...]
    srow = lax.broadcasted_iota(jnp.int32, (CHUNK, CHUNK + 8), 0)
    scol = lax.broadcasted_iota(jnp.int32, (CHUNK, CHUNK + 8), 1)
    conv = convb_ref[...] + convw_ref[SSD_CONV - 1:SSD_CONV, :] * xcur.astype(F32)
    for kk in range(SSD_CONV - 1):
        shift = jnp.where(scol == srow + 5 + kk, 1.0, 0.0).astype(BF16)
        conv = conv + convw_ref[kk:kk + 1, :] * _dot(shift, xall)
    xext_ref[0:8, :] = xext_ref[CHUNK:CHUNK + 8, :]
    valid_w = jnp.logical_or(c > 0, lax.broadcasted_iota(jnp.int32, (CHUNK, XBC_WIDTH), 0) >= FRONT_PAD)
    act = jnp.where(valid_w, _silu(conv), 0.0)
    xs = act[:, :SSD_WIDTH]
    bmat = act[:, SSD_WIDTH:SSD_WIDTH + SSD_GROUPS * SSD_STATE].astype(BF16)
    cmat = act[:, SSD_WIDTH + SSD_GROUPS * SSD_STATE:].astype(BF16)

    misc = misc_ref[...]
    dt = jnp.where(valid, jax.nn.softplus(misc + dtb_ref[...]), 0.0)
    d_a = dt * aneg_ref[...]
    p0, p1, p2 = _split3(d_a)
    a_cs = _dot(tri, p0) + _dot(tri, p1) + _dot(tri, p2)
    a_cs_t = a_cs.T

    lo_half = col < SSD_HEAD_DIM
    hg = SSD_HEADS // SSD_GROUPS
    pairs_per_group = hg // 2
    y_parts = []
    for gi in range(SSD_GROUPS):
        b_g = bmat[:, gi * SSD_STATE:(gi + 1) * SSD_STATE]
        c_g = cmat[:, gi * SSD_STATE:(gi + 1) * SSD_STATE]
        c_g32 = c_g.astype(F32)
        cb = _dot_nt(c_g, b_g)
        w_parts = []
        cd_parts = []
        for pj in range(pairs_per_group):
            h0 = gi * hg + 2 * pj
            h1 = h0 + 1
            lanes = slice(h0 * SSD_HEAD_DIM, (h1 + 1) * SSD_HEAD_DIM)
            ab0 = jnp.broadcast_to(a_cs[:, h0:h0 + 1], (CHUNK, CHUNK))
            ab1 = jnp.broadcast_to(a_cs[:, h1:h1 + 1], (CHUNK, CHUNK))
            db0 = jnp.broadcast_to(dt[:, h0:h0 + 1], (CHUNK, CHUNK))
            db1 = jnp.broadcast_to(dt[:, h1:h1 + 1], (CHUNK, CHUNK))
            l0 = jnp.exp(jnp.where(causal, ab0 - a_cs_t[h0:h0 + 1, :], -jnp.inf))
            l1 = jnp.exp(jnp.where(causal, ab1 - a_cs_t[h1:h1 + 1, :], -jnp.inf))
            lhs = jnp.concatenate(
                [(cb * l0).astype(BF16), (c_g32 * jnp.exp(ab0)).astype(BF16),
                 (cb * l1).astype(BF16), (c_g32 * jnp.exp(ab1)).astype(BF16)], axis=1)
            xs_p = xs[:, lanes]
            dtx = xs_p * jnp.where(lo_half, db0, db1)
            st = sstate_ref[gi, :, pj * CHUNK:(pj + 1) * CHUNK]
            rhs = jnp.concatenate(
                [jnp.where(lo_half, dtx, 0.0).astype(BF16), jnp.where(lo_half, st, 0.0).astype(BF16),
                 jnp.where(lo_half, 0.0, dtx).astype(BF16), jnp.where(lo_half, 0.0, st).astype(BF16)],
                axis=0)
            y_p = _dot(lhs, rhs) + dskip_ref[:, lanes] * xs_p
            y_parts.append(y_p)
            ae = jnp.where(lo_half, ab0, ab1)
            a_last = ae[CHUNK - 1:CHUNK, :]
            w_parts.append((dtx * jnp.exp(a_last - ae)).astype(BF16))
            cd_parts.append(jnp.exp(a_last))
        upd = _dot_tn(b_g, jnp.concatenate(w_parts, axis=1))
        sstate_ref[gi] = jnp.concatenate(cd_parts, axis=1) * sstate_ref[gi] + upd
    y = jnp.concatenate(y_parts, axis=1)
    y = y * z_ref[...].astype(F32)
    gsz = SSD_WIDTH // SSD_GROUPS
    for gi in range(SSD_GROUPS):
        yg = y[:, gi * gsz:(gi + 1) * gsz]
        yn = yg * lax.rsqrt(jnp.mean(yg * yg, axis=-1, keepdims=True) + EPS)
        out_ref[:, gi * gsz:(gi + 1) * gsz] = (yn * ssdw_ref[:, gi * gsz:(gi + 1) * gsz]).astype(out_ref.dtype)

    m0, m1 = _split2(misc)
    pre = _dot(m0, wdec_ref[...]) + _dot(m1, wdec_ref[...]) + bdec_ref[...]
    log_a = jax.nn.log_sigmoid(pre) * (1.0 / GLA_TAU)
    g0, g1 = _split2(log_a)
    gcum = _dot(tri, g0) + _dot(tri, g1)
    roww = lax.broadcasted_iota(jnp.int32, (CHUNK, GLA_K_WIDTH), 0)
    qf = q_ref[...].astype(F32) * (GLA_KEY_DIM ** -0.5)
    kf = k_ref[...].astype(F32)
    nsub = CHUNK // GLA_SUB
    g_start = _block_rows(gcum, [None] + [GLA_SUB * i - 1 for i in range(1, nsub)], GLA_SUB)
    e0 = gcum - g_start
    q_lv = [(qf * jnp.exp(e0)).astype(BF16)]
    k_lv = [(kf * jnp.exp(-e0)).astype(BF16)]
    shifts = (4, 5, 6)
    for sh in shifts:
        b = 1 << sh
        bound = _block_rows(gcum, [2 * b * i + b - 1 for i in range(CHUNK // (2 * b))], 2 * b)
        second = ((roww >> sh) & 1) == 1
        x_l = jnp.exp(jnp.where(second, gcum - bound, bound - gcum))
        q_lv.append((qf * x_l).astype(BF16))
        k_lv.append((kf * x_l).astype(BF16))
    masks = [jnp.logical_and((row >> 4) == (col >> 4), causal)]
    for sh in shifts:
        same = (row >> (sh + 1)) == (col >> (sh + 1))
        m = jnp.logical_and(same, jnp.logical_and(((row >> sh) & 1) == 1, ((col >> sh) & 1) == 0))
        masks.append(m)
    g_last = gcum[CHUNK - 1:CHUNK, :]
    q_in = (qf * jnp.exp(gcum)).astype(BF16)
    k_end = (kf * jnp.exp(g_last - gcum)).astype(BF16)
    dec = jnp.exp(g_last)
    vb = v_ref[...]
    for hh in range(GLA_HEADS):
        kl = slice(hh * GLA_KEY_DIM, (hh + 1) * GLA_KEY_DIM)
        vl = slice(hh * GLA_VAL_DIM, (hh + 1) * GLA_VAL_DIM)
        scores = jnp.zeros((CHUNK, CHUNK), F32)
        for lv in range(len(masks)):
            scores = scores + jnp.where(masks[lv], _dot_nt(q_lv[lv][:, kl], k_lv[lv][:, kl]), 0.0)
        v_h = vb[:, vl]
        s_t = gstate_ref[hh]
        o = _dot(scores.astype(BF16), v_h) + _dot_nt(q_in[:, kl], s_t.astype(BF16))
        gstate_ref[hh] = dec[:, kl] * s_t + _dot_tn(v_h, k_end[:, kl])
        o = o * lax.rsqrt(jnp.mean(o * o, axis=-1, keepdims=True) + EPS) * glaw_ref[...]
        o = o * g_ref[:, vl].astype(F32)
        out_ref[:, SSD_WIDTH + hh * GLA_VAL_DIM:SSD_WIDTH + (hh + 1) * GLA_VAL_DIM] = o.astype(out_ref.dtype)


def _mixer(z, xbc, q, k, v, g, misc, params, batch, nchunks):
    widths = (SSD_WIDTH, XBC_WIDTH, GLA_K_WIDTH, GLA_K_WIDTH, GLA_V_WIDTH, GLA_V_WIDTH, MISC_WIDTH)
    assert len(params) == N_MIXER_PARAMS
    nseq = 2 if batch % 2 == 0 else 1
    meta_block = z.shape[0] // CHUNK - 1

    def row_spec(w, s):
        return pl.BlockSpec(
            (CHUNK, w),
            lambda bp, c: (jnp.where(c == 0, meta_block, (bp * nseq + s) * (nchunks - 1) + c - 1), 0))

    par_spec = lambda p: pl.BlockSpec(p.shape, lambda bp, c: (0,) * p.ndim)
    seq = (nchunks - 1) * CHUNK
    width = SSD_WIDTH + GLA_V_WIDTH
    out = pl.pallas_call(
        functools.partial(_mixer_kernel, nseq=nseq),
        grid=(batch // nseq, nchunks),
        in_specs=[row_spec(w, s) for s in range(nseq) for w in widths] + [par_spec(p) for p in params],
        out_specs=pl.BlockSpec((nseq, CHUNK, width), lambda bp, c: (bp, jnp.maximum(c - 1, 0), 0)),
        out_shape=jax.ShapeDtypeStruct((batch, seq, width), BF16),
        scratch_shapes=[
            pltpu.VMEM((nseq, CHUNK + 8, XBC_WIDTH), BF16),
            pltpu.VMEM((nseq, SSD_GROUPS, SSD_STATE, SSD_WIDTH // SSD_GROUPS), F32),
            pltpu.VMEM((nseq, GLA_HEADS, GLA_VAL_DIM, GLA_KEY_DIM), F32),
        ],
        compiler_params=pltpu.CompilerParams(
            dimension_semantics=("arbitrary", "arbitrary"), vmem_limit_bytes=VMEM_LIMIT),
        name="mixer",
    )(*([z, xbc, q, k, v, g, misc] * nseq), *params)
    return out.reshape(batch * seq, width)


def _post_kernel(x_ref, mixed_ref, wout_ref, fnw_ref, wrt_ref, br_ref,
                 h2_ref, t_ref, idx_ref, gate_ref, lrank_ref, cnt_ref, zero_ref, tprev_ref):
    @pl.when(pl.program_id(0) == 0)
    def _():
        tprev_ref[...] = jnp.zeros_like(tprev_ref)

    t_prev = tprev_ref[...]
    h2 = x_ref[...] + _dot(mixed_ref[...], wout_ref[...])
    h2_ref[...] = h2
    t = h2 * lax.rsqrt(jnp.mean(h2 * h2, axis=-1, keepdims=True) + EPS) * fnw_ref[...]
    t_ref[...] = t.astype(BF16)
    tprev_ref[...] = t

    t0, t1 = _split2(t_prev)
    w0 = wrt_ref[0]
    w1 = wrt_ref[1]
    logits = _dot_nt(w0, t0) + _dot_nt(w0, t1) + _dot_nt(w1, t0) + br_ref[...]
    e_iota = lax.broadcasted_iota(jnp.int32, logits.shape, 0)
    work = logits
    sel_any = jnp.zeros(logits.shape, F32)
    tops = []
    idxs = []
    onehots = []
    for _ in range(TOP_K):
        m = jnp.max(work, axis=0, keepdims=True)
        idx = jnp.min(jnp.where(work == m, e_iota, N_EXPERTS), axis=0, keepdims=True)
        hit = e_iota == idx
        tops.append(m)
        idxs.append(idx)
        onehots.append(hit)
        sel_any = sel_any + jnp.where(hit, 1.0, 0.0)
        work = jnp.where(hit, -jnp.inf, work)
    exps = [jnp.exp(tv - tops[0]) for tv in tops]
    denom = exps[0] + exps[1] + exps[2] + exps[3]
    tile = logits.shape[1]
    pad_rows = 8 - TOP_K
    gate_ref[...] = jnp.concatenate([e / denom for e in exps] + [jnp.zeros((pad_rows, tile), F32)], axis=0)
    idx_ref[...] = jnp.concatenate(idxs + [jnp.zeros((pad_rows, tile), jnp.int32)], axis=0)

    r_i = lax.broadcasted_iota(jnp.int32, (tile, tile), 0)
    c_i = lax.broadcasted_iota(jnp.int32, (tile, tile), 1)
    upper = jnp.where(r_i < c_i, 1.0, 0.0).astype(BF16)
    before = _dot(sel_any.astype(BF16), upper)
    ranks = [jnp.sum(jnp.where(h, before, 0.0), axis=0, keepdims=True) for h in onehots]
    ranks.append(jnp.full((pad_rows, tile), float(NO_ROW), F32))
    lrank_ref[...] = jnp.concatenate(ranks, axis=0).astype(jnp.int32)
    cnt = jnp.sum(sel_any, axis=1, keepdims=True)
    cnt_ref[...] = jnp.broadcast_to(cnt, cnt_ref.shape).astype(jnp.int32)
    zero_ref[...] = jnp.zeros_like(zero_ref)


def _post(x_flat, mixed, w_out_b, ffn_norm_w, w_router_t, b_router_col, zero_rows):
    n = x_flat.shape[0]
    tt = TOKEN_TILE
    n_tiles = n // tt
    const = lambda shape: pl.BlockSpec(shape, lambda i: (0,) * len(shape))
    cur = lambda i: jnp.minimum(i, n_tiles - 1)
    prev = lambda i: jnp.maximum(i - 1, 0)
    return pl.pallas_call(
        _post_kernel,
        grid=(n_tiles + 1,),
        in_specs=[
            pl.BlockSpec((tt, D_MODEL), lambda i: (cur(i), 0)),
            pl.BlockSpec((tt, SSD_WIDTH + GLA_V_WIDTH), lambda i: (cur(i), 0)),
            const(w_out_b.shape),
            const((1, D_MODEL)),
            const(w_router_t.shape),
            const((N_EXPERTS, 1)),
        ],
        out_specs=[
            pl.BlockSpec((tt, D_MODEL), lambda i: (cur(i), 0)),
            pl.BlockSpec((tt, D_MODEL), lambda i: (cur(i), 0)),
            pl.BlockSpec((8, tt), lambda i: (0, prev(i))),
            pl.BlockSpec((8, tt), lambda i: (0, prev(i))),
            pl.BlockSpec((8, tt), lambda i: (0, prev(i))),
            pl.BlockSpec((N_EXPERTS, 128), lambda i: (prev(i), 0)),
            pl.BlockSpec((zero_rows, HALF), lambda i: (i, 0)),
        ],
        out_shape=[
            jax.ShapeDtypeStruct((n, D_MODEL), F32),
            jax.ShapeDtypeStruct((n, D_MODEL), BF16),
            jax.ShapeDtypeStruct((8, n), jnp.int32),
            jax.ShapeDtypeStruct((8, n), F32),
            jax.ShapeDtypeStruct((8, n), jnp.int32),
            jax.ShapeDtypeStruct((n_tiles * N_EXPERTS, 128), jnp.int32),
            jax.ShapeDtypeStruct(((n_tiles + 1) * zero_rows, HALF), jnp.uint32),
        ],
        scratch_shapes=[pltpu.VMEM((tt, D_MODEL), F32)],
        compiler_params=pltpu.CompilerParams(
            dimension_semantics=("arbitrary",), vmem_limit_bytes=VMEM_LIMIT),
        name="post",
    )(x_flat, mixed, w_out_b, ffn_norm_w, w_router_t, b_router_col)


def _pack_pairs(x):
    lo = pltpu.bitcast(x[:, :HALF], jnp.uint32) >> 16
    hi = pltpu.bitcast(x[:, HALF:], jnp.uint32) & jnp.uint32(0xFFFF0000)
    return lo | hi


def _unpack_pairs(w):
    lo = pltpu.bitcast(w << 16, F32).astype(BF16)
    hi = pltpu.bitcast(w & jnp.uint32(0xFFFF0000), F32).astype(BF16)
    return lo, hi


def _sorted_positions(idx, lrank, offs_ref, base):
    pos = lrank
    for e in range(N_EXPERTS):
        pos = pos + jnp.where(idx == e, offs_ref[base + e], 0)
    return pos


def _group_copies(dst_ref, make_copy):
    return [make_copy(g * SEG_ALIGN, pl.multiple_of(dst_ref[0, 0, g], SEG_ALIGN))
            for g in range(SORT_GROUPS)]


def _dispatch_kernel(offs_ref, idx_ref, lrank_ref, t_ref, dst_ref, dst_prev_ref, buf_in_ref, buf_ref,
                     sorted_ref, sem):
    del buf_in_ref
    i = pl.program_id(0)
    slot = i % 2
    base = i * N_EXPERTS
    tt = t_ref.shape[0]
    pos = _sorted_positions(idx_ref[...], lrank_ref[...], offs_ref, base)
    t = t_ref[...]
    for r0 in range(0, SORT_ROWS, tt):
        prow = lax.broadcasted_iota(jnp.int32, (tt, tt), 0) + r0
        perm = jnp.zeros((tt, tt), F32)
        for kk in range(TOP_K):
            perm = jnp.where(prow == pos[kk:kk + 1, :], 1.0, perm)
        sorted_ref[slot, r0:r0 + tt, :] = _pack_pairs(_dot(perm.astype(BF16), t))

    def copies(table_ref, which):
        def make_copy(local, glob):
            return pltpu.make_async_copy(
                sorted_ref.at[which, pl.ds(local, SEG_ALIGN), :],
                buf_ref.at[pl.ds(glob, SEG_ALIGN), :], sem.at[which])
        return _group_copies(table_ref, make_copy)

    for cp in copies(dst_ref, slot):
        cp.start()

    @pl.when(i > 0)
    def _():
        for cp in copies(dst_prev_ref, 1 - slot):
            cp.wait()

    @pl.when(i == pl.num_programs(0) - 1)
    def _():
        for cp in copies(dst_ref, slot):
            cp.wait()


def _dispatch(offs, group_dst, top_idx, lrank, t_b, buf0):
    n = t_b.shape[0]
    tt = TOKEN_TILE
    table = lambda index: pl.BlockSpec((1, 1, SORT_GROUPS), index, memory_space=pltpu.SMEM)
    grid_spec = pltpu.PrefetchScalarGridSpec(
        num_scalar_prefetch=1,
        grid=(n // tt,),
        in_specs=[
            pl.BlockSpec((8, tt), lambda i, *_: (0, i)),
            pl.BlockSpec((8, tt), lambda i, *_: (0, i)),
            pl.BlockSpec((tt, D_MODEL), lambda i, *_: (i, 0)),
            table(lambda i, *_: (i, 0, 0)),
            table(lambda i, *_: (jnp.maximum(i - 1, 0), 0, 0)),
            pl.BlockSpec(memory_space=pl.ANY),
        ],
        out_specs=pl.BlockSpec(memory_space=pl.ANY),
        scratch_shapes=[pltpu.VMEM((2, SORT_ROWS, HALF), jnp.uint32), pltpu.SemaphoreType.DMA((2,))],
    )
    return pl.pallas_call(
        _dispatch_kernel,
        grid_spec=grid_spec,
        out_shape=jax.ShapeDtypeStruct(buf0.shape, jnp.uint32),
        input_output_aliases={6: 0},
        compiler_params=pltpu.CompilerParams(
            dimension_semantics=("arbitrary",), vmem_limit_bytes=VMEM_LIMIT),
        name="dispatch",
    )(offs, top_idx, lrank, t_b, group_dst, group_dst, buf0)


def _expert_kernel(be_ref, nused_ref, xp_ref, wgu_ref, bgu_ref, wd_ref, bd_ref, y_ref,
                   wgu_b_ref, wd_b_ref):
    i = pl.program_id(0)
    used = i < nused_ref[0]

    @pl.when(jnp.logical_not(used))
    def _():
        y_ref[...] = jnp.zeros_like(y_ref)

    new_expert = jnp.logical_or(i == 0, be_ref[i] != be_ref[jnp.maximum(i - 1, 0)])

    @pl.when(jnp.logical_and(used, new_expert))
    def _():
        rows = 128
        for r0 in range(0, D_MODEL, rows):
            wgu_b_ref[r0:r0 + rows, :] = wgu_ref[r0:r0 + rows, :].astype(BF16)
        for r0 in range(0, D_FF, rows):
            wd_b_ref[r0:r0 + rows, :] = wd_ref[r0:r0 + rows, :].astype(BF16)

    @pl.when(used)
    def _():
        x_lo, x_hi = _unpack_pairs(xp_ref[...])
        hgu = _dot(x_lo, wgu_b_ref[:HALF, :]) + _dot(x_hi, wgu_b_ref[HALF:, :]) + bgu_ref[...]
        gate = jnp.minimum(hgu[:, :D_FF], SWIGLU_LIMIT)
        up = jnp.clip(hgu[:, D_FF:], -SWIGLU_LIMIT, SWIGLU_LIMIT)
        act = gate * jax.nn.sigmoid(SWIGLU_ALPHA * gate)
        y = _dot(((up + 1.0) * act).astype(BF16), wd_b_ref[...]) + bd_ref[...]
        y_ref[...] = _pack_pairs(y.astype(BF16).astype(F32))


def _experts(block_expert, n_used, buf, w_gu, b_gu, w_d, b_d):
    n_blocks = block_expert.shape[0]
    rows = n_blocks * MOE_BLOCK
    row_map = lambda i, be, nu: (jnp.minimum(i, nu[0] - 1), 0)
    w_map = lambda i, be, nu: (be[jnp.minimum(i, nu[0] - 1)], 0, 0)
    grid_spec = pltpu.PrefetchScalarGridSpec(
        num_scalar_prefetch=2,
        grid=(n_blocks,),
        in_specs=[
            pl.BlockSpec((MOE_BLOCK, HALF), row_map),
            pl.BlockSpec((None, D_MODEL, 2 * D_FF), w_map),
            pl.BlockSpec((None, 1, 2 * D_FF), w_map),
            pl.BlockSpec((None, D_FF, D_MODEL), w_map),
            pl.BlockSpec((None, 1, D_MODEL), w_map),
        ],
        out_specs=pl.BlockSpec((MOE_BLOCK, HALF), lambda i, be, nu: (i, 0)),
        scratch_shapes=[pltpu.VMEM((D_MODEL, 2 * D_FF), BF16), pltpu.VMEM((D_FF, D_MODEL), BF16)],
    )
    return pl.pallas_call(
        _expert_kernel,
        grid_spec=grid_spec,
        out_shape=jax.ShapeDtypeStruct((rows, HALF), jnp.uint32),
        compiler_params=pltpu.CompilerParams(
            dimension_semantics=("arbitrary",), vmem_limit_bytes=VMEM_LIMIT),
        name="experts",
    )(block_expert, n_used, buf, w_gu, b_gu, w_d, b_d)


def _combine_kernel(offs_ref, idx_ref, lrank_ref, gate_ref, h2_ref, fw_ref, dst_ref, dst_next_ref,
                    y_hbm_ref, out_ref, ys_ref, sem):
    i = pl.program_id(0)
    slot = i % 2
    base = i * N_EXPERTS
    tt = h2_ref.shape[0]

    def copies(table_ref, which):
        def make_copy(local, glob):
            return pltpu.make_async_copy(
                y_hbm_ref.at[pl.ds(glob, SEG_ALIGN), :],
                ys_ref.at[which, pl.ds(local, SEG_ALIGN), :], sem.at[which])
        return _group_copies(table_ref, make_copy)

    @pl.when(i == 0)
    def _():
        for cp in copies(dst_ref, slot):
            cp.start()

    @pl.when(i < pl.num_programs(0) - 1)
    def _():
        for cp in copies(dst_next_ref, 1 - slot):
            cp.start()

    pos = _sorted_positions(idx_ref[...], lrank_ref[...], offs_ref, base).astype(F32)
    zpad = jnp.zeros((128 - 8, tt), F32)
    pos_c = jnp.concatenate([pos, zpad], axis=0).T
    gate_c = jnp.concatenate([gate_ref[...], zpad], axis=0).T
    lane = lax.broadcasted_iota(jnp.int32, (tt, SORT_ROWS), 1).astype(F32)
    pg = jnp.zeros((tt, SORT_ROWS), F32)
    for kk in range(TOP_K):
        pg = jnp.where(lane == pos_c[:, kk:kk + 1], gate_c[:, kk:kk + 1], pg)
    p_hi, p_lo = _split2(pg)

    for cp in copies(dst_ref, slot):
        cp.wait()
    y_lo, y_hi = _unpack_pairs(ys_ref[slot])
    ffn = jnp.concatenate(
        [_dot(p_hi, y_lo) + _dot(p_lo, y_lo), _dot(p_hi, y_hi) + _dot(p_lo, y_hi)], axis=1)
    acc = h2_ref[...] + ffn
    out_ref[...] = acc * lax.rsqrt(jnp.mean(acc * acc, axis=-1, keepdims=True) + EPS) * fw_ref[...]


def _combine(offs, group_dst, top_idx, lrank, gates, h2, final_norm_w, y_buf):
    n = h2.shape[0]
    tt = TOKEN_TILE
    grid_spec = pltpu.PrefetchScalarGridSpec(
        num_scalar_prefetch=1,
        grid=(n // tt,),
        in_specs=[
            pl.BlockSpec((8, tt), lambda i, *_: (0, i)),
            pl.BlockSpec((8, tt), lambda i, *_: (0, i)),
            pl.BlockSpec((8, tt), lambda i, *_: (0, i)),
            pl.BlockSpec((tt, D_MODEL), lambda i, *_: (i, 0)),
            pl.BlockSpec((1, D_MODEL), lambda i, *_: (0, 0)),
            pl.BlockSpec((1, 1, SORT_GROUPS), lambda i, *_: (i, 0, 0), memory_space=pltpu.SMEM),
            pl.BlockSpec((1, 1, SORT_GROUPS), lambda i, *_: (jnp.minimum(i + 1, n // tt - 1), 0, 0),
                         memory_space=pltpu.SMEM),
            pl.BlockSpec(memory_space=pl.ANY),
        ],
        out_specs=pl.BlockSpec((tt, D_MODEL), lambda i, *_: (i, 0)),
        scratch_shapes=[pltpu.VMEM((2, SORT_ROWS, HALF), jnp.uint32), pltpu.SemaphoreType.DMA((2,))],
    )
    return pl.pallas_call(
        _combine_kernel,
        grid_spec=grid_spec,
        out_shape=jax.ShapeDtypeStruct((n, D_MODEL), F32),
        compiler_params=pltpu.CompilerParams(
            dimension_semantics=("arbitrary",), vmem_limit_bytes=VMEM_LIMIT),
        name="combine",
    )(offs, top_idx, lrank, gates, h2, final_norm_w, group_dst, group_dst, y_buf)


def _pad_lanes(v, width):
    return jnp.pad(v, ((0, 0), (0, width - v.shape[1])))


def kernel(x, meta_tokens, mix_norm_w, w_in, conv_w, conv_b, dt_bias, a_log, d_skip, ssd_norm_w,
           w_decay_up, b_decay, gla_norm_w, w_out, ffn_norm_w, w_router, b_router, w_gate_up,
           b_gate_up, w_down, b_down, final_norm_w):
    batch, seq, d = x.shape
    assert d == D_MODEL and seq % TOKEN_TILE == 0
    assert mix_norm_w.shape[0] == 1, "single-layer block"
    nchunks = (FRONT_PAD + N_META + seq) // CHUNK

    n = batch * seq
    x_flat = x.reshape(n, D_MODEL)
    meta_tile = jnp.concatenate(
        [jnp.zeros((TOKEN_TILE - N_META, D_MODEL), x.dtype), meta_tokens.astype(x.dtype)], axis=0)

    wi = w_in[0]
    o_z, o_xbc = 0, SSD_WIDTH
    o_dt = o_xbc + XBC_WIDTH
    o_q = o_dt + SSD_HEADS
    o_k = o_q + GLA_K_WIDTH
    o_v = o_k + GLA_K_WIDTH
    o_g = o_v + GLA_V_WIDTH
    o_a = o_g + GLA_V_WIDTH
    w_misc = jnp.concatenate(
        [wi[:, o_dt:o_dt + SSD_HEADS], wi[:, o_a:o_a + GLA_RANK],
         jnp.zeros((D_MODEL, MISC_WIDTH - SSD_HEADS - GLA_RANK), wi.dtype)], axis=1)
    w_in_r = jnp.concatenate(
        [wi[:, o_z:o_dt], wi[:, o_q:o_a], w_misc], axis=1).astype(BF16)
    z, xbc, q, k, v, g, misc = _in_proj(x_flat, meta_tile, mix_norm_w[0][None, :], w_in_r)

    dtb = _pad_lanes(dt_bias[0][None, :].astype(F32), MISC_WIDTH)
    aneg = _pad_lanes(-jnp.exp(a_log[0].astype(F32))[None, :], MISC_WIDTH)
    dskip = jnp.repeat(d_skip[0].astype(F32), SSD_HEAD_DIM)[None, :]
    wdec = jnp.zeros((MISC_WIDTH, GLA_K_WIDTH), F32).at[SSD_HEADS:SSD_HEADS + GLA_RANK].set(w_decay_up[0])
    params = (conv_w[0], conv_b[0][None, :], dtb, aneg, dskip, ssd_norm_w[0][None, :],
              wdec.astype(BF16), b_decay[0][None, :], gla_norm_w[0][None, :])
    mixed = _mixer(z, xbc, q, k, v, g, misc, params, batch, nchunks)

    wr_hi, wr_lo = _split2(w_router[0].T.astype(F32))
    n_tiles = n // TOKEN_TILE
    n_blocks = -(-(n * TOP_K + n_tiles * N_EXPERTS * (SEG_ALIGN - 1)) // MOE_BLOCK) + N_EXPERTS
    spare = n_blocks * MOE_BLOCK
    total_rows = spare + 2 * SORT_ROWS
    zero_rows = -(-total_rows // ((n_tiles + 1) * SEG_ALIGN)) * SEG_ALIGN
    h2, t_b, top_idx, gates, lrank, tile_cnt, buf0 = _post(
        x_flat, mixed, w_out[0].astype(BF16), ffn_norm_w[0][None, :],
        jnp.stack([wr_hi, wr_lo]), b_router[0][:, None], zero_rows)

    tile_cnt = tile_cnt.reshape(n_tiles, N_EXPERTS, 128)[:, :, 0]
    seg_rows = (tile_cnt + SEG_ALIGN - 1) // SEG_ALIGN * SEG_ALIGN
    counts = jnp.sum(seg_rows, axis=0)
    padded = (counts + MOE_BLOCK - 1) // MOE_BLOCK * MOE_BLOCK
    pend = jnp.cumsum(padded)
    pstart = pend - padded
    dstart = pstart[None, :] + jnp.cumsum(seg_rows, axis=0) - seg_rows
    seg_end = jnp.cumsum(seg_rows, axis=1)
    offs = seg_end - seg_rows
    block_pos = jnp.arange(total_rows // MOE_BLOCK, dtype=jnp.int32) * MOE_BLOCK
    block_expert = jnp.minimum(
        jnp.sum((pend[None, :] <= block_pos[:, None]).astype(jnp.int32), axis=1), N_EXPERTS - 1)
    n_used = (pend[-1:] // MOE_BLOCK).astype(jnp.int32)
    grow = jnp.arange(SORT_GROUPS, dtype=jnp.int32)[None, :, None] * SEG_ALIGN
    inside = (offs[:, None, :] <= grow) & (grow < seg_end[:, None, :])
    group_dst = jnp.sum(jnp.where(inside, dstart[:, None, :] + grow - offs[:, None, :], 0), axis=2)
    parity = (jnp.arange(n_tiles, dtype=jnp.int32) % 2)[:, None]
    group_dst = jnp.where(jnp.any(inside, axis=2), group_dst, spare + parity * SORT_ROWS + grow[:, :, 0])
    group_dst = group_dst.astype(jnp.int32)[:, None, :]
    offs = offs.reshape(-1).astype(jnp.int32)

    buf = _dispatch(offs, group_dst, top_idx, lrank, t_b, buf0)
    y_buf = _experts(block_expert, n_used, buf, w_gate_up[0], b_gate_up[0][:, None, :],
                     w_down[0], b_down[0][:, None, :])
    out = _combine(offs, group_dst, top_idx, lrank, gates, h2, final_norm_w[None, :], y_buf)
    return out.reshape(batch, seq, D_MODEL)
```

```python
import functools

import jax
import jax.numpy as jnp
from jax import lax
from jax.experimental import pallas as pl
from jax.experimental.pallas import tpu as pltpu

F32 = jnp.float32
BF16 = jnp.bfloat16

D_MODEL = 1024
N_META = 16
EPS = 1e-5
SSD_HEAD_DIM = 64
SSD_HEADS = 16
SSD_GROUPS = 2
SSD_STATE = 128
SSD_CONV = 4
SSD_WIDTH = SSD_HEADS * SSD_HEAD_DIM
XBC_WIDTH = SSD_WIDTH + 2 * SSD_GROUPS * SSD_STATE
GLA_HEADS = 4
GLA_KEY_DIM = 128
GLA_VAL_DIM = 256
GLA_K_WIDTH = GLA_HEADS * GLA_KEY_DIM
GLA_V_WIDTH = GLA_HEADS * GLA_VAL_DIM
GLA_RANK = 16
GLA_TAU = 16.0
GLA_SUB = 16
N_EXPERTS = 32
TOP_K = 4
D_FF = D_MODEL
SWIGLU_LIMIT = 7.0
SWIGLU_ALPHA = 1.702
MOE_BLOCK = 512
EXPERT_SUBBLOCKS = 2

CHUNK = 128
FRONT_PAD = CHUNK - N_META
MISC_WIDTH = 128
IN_PAD_WIDTH = SSD_WIDTH + XBC_WIDTH + 2 * GLA_K_WIDTH + 2 * GLA_V_WIDTH + MISC_WIDTH
TOKEN_TILE = 256
HALF = D_MODEL // 2
NO_ROW = 1 << 12
SEG_ALIGN = 8
SORT_ROWS = -(-(TOP_K * TOKEN_TILE + N_EXPERTS * (SEG_ALIGN - 1)) // TOKEN_TILE) * TOKEN_TILE
SORT_GROUPS = SORT_ROWS // SEG_ALIGN
VMEM_LIMIT = 56 * 1024 * 1024


def _split2(x):
    hi = x.astype(BF16)
    lo = (x - hi.astype(F32)).astype(BF16)
    return hi, lo


def _split3(x):
    hi = x.astype(BF16)
    r = x - hi.astype(F32)
    mid = r.astype(BF16)
    lo = (r - mid.astype(F32)).astype(BF16)
    return hi, mid, lo


def _dot(a, b):
    return jnp.dot(a, b, preferred_element_type=F32)


def _dot_nt(a, b):
    return lax.dot_general(a, b, (((1,), (1,)), ((), ())), preferred_element_type=F32)


def _dot_tn(a, b):
    return lax.dot_general(a, b, (((0,), (0,)), ((), ())), preferred_element_type=F32)


def _silu(x):
    return x * jax.nn.sigmoid(x)


_IN_SECTIONS = (SSD_WIDTH, XBC_WIDTH, GLA_K_WIDTH, GLA_K_WIDTH, GLA_V_WIDTH, GLA_V_WIDTH, MISC_WIDTH)
_GATE_SECTIONS = (0, 5)


def _in_proj_kernel(x_ref, meta_ref, nw_ref, w_ref, z_ref, xbc_ref, q_ref, k_ref, v_ref, g_ref, misc_ref):
    h = jnp.where(pl.program_id(0) < pl.num_programs(0) - 1, x_ref[...], meta_ref[...])
    u = h * lax.rsqrt(jnp.mean(h * h, axis=-1, keepdims=True) + EPS) * nw_ref[...]
    ub = u.astype(BF16)
    outs = (z_ref, xbc_ref, q_ref, k_ref, v_ref, g_ref, misc_ref)
    off = 0
    for section, (o_ref, width) in enumerate(zip(outs, _IN_SECTIONS)):
        p = _dot(ub, w_ref[:, off:off + width])
        if section in _GATE_SECTIONS:
            p = _silu(p)
        o_ref[...] = p.astype(o_ref.dtype)
        off += width


def _in_proj(x_flat, meta_tile, norm_w, w_in_r):
    tm = TOKEN_TILE
    nx = x_flat.shape[0] // tm
    rows = (nx + 1) * tm
    row_spec = lambda w: pl.BlockSpec((tm, w), lambda i: (i, 0))
    out_dtypes = (BF16,) * 6 + (F32,)
    return pl.pallas_call(
        _in_proj_kernel,
        grid=(nx + 1,),
        in_specs=[
            pl.BlockSpec((tm, D_MODEL), lambda i: (jnp.minimum(i, nx - 1), 0)),
            pl.BlockSpec((tm, D_MODEL), lambda i: (0, 0)),
            pl.BlockSpec((1, D_MODEL), lambda i: (0, 0)),
            pl.BlockSpec((D_MODEL, IN_PAD_WIDTH), lambda i: (0, 0)),
        ],
        out_specs=[row_spec(w) for w in _IN_SECTIONS],
        out_shape=[jax.ShapeDtypeStruct((rows, w), dt) for w, dt in zip(_IN_SECTIONS, out_dtypes)],
        compiler_params=pltpu.CompilerParams(
            dimension_semantics=("arbitrary",), vmem_limit_bytes=VMEM_LIMIT),
        name="in_proj",
    )(x_flat, meta_tile, norm_w, w_in_r)


def _block_rows(x, rows, width):
    parts = []
    for r in rows:
        if r is None:
            parts.append(jnp.zeros((width, x.shape[1]), x.dtype))
        else:
            parts.append(jnp.broadcast_to(x[r:r + 1, :], (width, x.shape[1])))
    return jnp.concatenate(parts, axis=0)


N_SEQ_INPUTS = 7
N_MIXER_PARAMS = 9


def _mixer_kernel(*refs, nseq):
    seq_refs = [refs[s * N_SEQ_INPUTS:(s + 1) * N_SEQ_INPUTS] for s in range(nseq)]
    rest = refs[nseq * N_SEQ_INPUTS:]
    param_refs = rest[:N_MIXER_PARAMS]
    out_ref, xext_ref, sstate_ref, gstate_ref = rest[N_MIXER_PARAMS:]
    c = pl.program_id(1)

    @pl.when(c == 0)
    def _():
        xext_ref[:, 0:8, :] = jnp.zeros((nseq, 8, XBC_WIDTH), BF16)
        sstate_ref[...] = jnp.zeros_like(sstate_ref)
        gstate_ref[...] = jnp.zeros_like(gstate_ref)

    for s in range(nseq):
        _mixer_chunk(c, *seq_refs[s], *param_refs, out_ref.at[s], xext_ref.at[s], sstate_ref.at[s],
                     gstate_ref.at[s])


def _mixer_chunk(c, z_ref, xbc_ref, q_ref, k_ref, v_ref, g_ref, misc_ref,
                 convw_ref, convb_ref, dtb_ref, aneg_ref, dskip_ref, ssdw_ref,
                 wdec_ref, bdec_ref, glaw_ref,
                 out_ref,
                 xext_ref, sstate_ref, gstate_ref):

    row = lax.broadcasted_iota(jnp.int32, (CHUNK, CHUNK), 0)
    col = lax.broadcasted_iota(jnp.int32, (CHUNK, CHUNK), 1)
    causal = row >= col
    tri = jnp.where(causal, 1.0, 0.0).astype(BF16)
    valid = jnp.logical_or(c > 0, row >= FRONT_PAD)

    xcur = xbc_ref[...]
    xext_ref[8:8 + CHUNK, :] = xcur
    xall = xext_ref[...]
    srow = lax.broadcasted_iota(jnp.int32, (CHUNK, CHUNK + 8), 0)
    scol = lax.broadcasted_iota(jnp.int32, (CHUNK, CHUNK + 8), 1)
    conv = convb_ref[...] + convw_ref[SSD_CONV - 1:SSD_CONV, :] * xcur.astype(F32)
    for kk in range(SSD_CONV - 1):
        shift = jnp.where(scol == srow + 5 + kk, 1.0, 0.0).astype(BF16)
        conv = conv + convw_ref[kk:kk + 1, :] * _dot(shift, xall)
    xext_ref[0:8, :] = xext_ref[CHUNK:CHUNK + 8, :]
    valid_w = jnp.logical_or(c > 0, lax.broadcasted_iota(jnp.int32, (CHUNK, XBC_WIDTH), 0) >= FRONT_PAD)
    act = jnp.where(valid_w, _silu(conv), 0.0)
    xs = act[:, :SSD_WIDTH]
    bmat = act[:, SSD_WIDTH:SSD_WIDTH + SSD_GROUPS * SSD_STATE].astype(BF16)
    cmat = act[:, SSD_WIDTH + SSD_GROUPS * SSD_STATE:].astype(BF16)

    misc = misc_ref[...]
    dt = jnp.where(valid, jax.nn.softplus(misc + dtb_ref[...]), 0.0)
    d_a = dt * aneg_ref[...]
    p0, p1, p2 = _split3(d_a)
    a_cs = _dot(tri, p0) + _dot(tri, p1) + _dot(tri, p2)
    a_cs_t = a_cs.T

    lo_half = col < SSD_HEAD_DIM
    hg = SSD_HEADS // SSD_GROUPS
    pairs_per_group = hg // 2
    y_parts = []
    for gi in range(SSD_GROUPS):
        b_g = bmat[:, gi * SSD_STATE:(gi + 1) * SSD_STATE]
        c_g = cmat[:, gi * SSD_STATE:(gi + 1) * SSD_STATE]
        c_g32 = c_g.astype(F32)
        cb = _dot_nt(c_g, b_g)
        w_parts = []
        cd_parts = []
        for pj in range(pairs_per_group):
            h0 = gi * hg + 2 * pj
            h1 = h0 + 1
            lanes = slice(h0 * SSD_HEAD_DIM, (h1 + 1) * SSD_HEAD_DIM)
            ab0 = jnp.broadcast_to(a_cs[:, h0:h0 + 1], (CHUNK, CHUNK))
            ab1 = jnp.broadcast_to(a_cs[:, h1:h1 + 1], (CHUNK, CHUNK))
            db0 = jnp.broadcast_to(dt[:, h0:h0 + 1], (CHUNK, CHUNK))
            db1 = jnp.broadcast_to(dt[:, h1:h1 + 1], (CHUNK, CHUNK))
            l0 = jnp.exp(jnp.where(causal, ab0 - a_cs_t[h0:h0 + 1, :], -jnp.inf))
            l1 = jnp.exp(jnp.where(causal, ab1 - a_cs_t[h1:h1 + 1, :], -jnp.inf))
            lhs = jnp.concatenate(
                [(cb * l0).astype(BF16), (c_g32 * jnp.exp(ab0)).astype(BF16),
                 (cb * l1).astype(BF16), (c_g32 * jnp.exp(ab1)).astype(BF16)], axis=1)
            xs_p = xs[:, lanes]
            dtx = xs_p * jnp.where(lo_half, db0, db1)
            st = sstate_ref[gi, :, pj * CHUNK:(pj + 1) * CHUNK]
            rhs = jnp.concatenate(
                [jnp.where(lo_half, dtx, 0.0).astype(BF16), jnp.where(lo_half, st, 0.0).astype(BF16),
                 jnp.where(lo_half, 0.0, dtx).astype(BF16), jnp.where(lo_half, 0.0, st).astype(BF16)],
                axis=0)
            y_p = _dot(lhs, rhs) + dskip_ref[:, lanes] * xs_p
            y_parts.append(y_p)
            ae = jnp.where(lo_half, ab0, ab1)
            a_last = ae[CHUNK - 1:CHUNK, :]
            w_parts.append((dtx * jnp.exp(a_last - ae)).astype(BF16))
            cd_parts.append(jnp.exp(a_last))
        upd = _dot_tn(b_g, jnp.concatenate(w_parts, axis=1))
        sstate_ref[gi] = jnp.concatenate(cd_parts, axis=1) * sstate_ref[gi] + upd
    y = jnp.concatenate(y_parts, axis=1)
    y = y * z_ref[...].astype(F32)
    gsz = SSD_WIDTH // SSD_GROUPS
    for gi in range(SSD_GROUPS):
        yg = y[:, gi * gsz:(gi + 1) * gsz]
        yn = yg * lax.rsqrt(jnp.mean(yg * yg, axis=-1, keepdims=True) + EPS)
        out_ref[:, gi * gsz:(gi + 1) * gsz] = (yn * ssdw_ref[:, gi * gsz:(gi + 1) * gsz]).astype(out_ref.dtype)

    m0, m1 = _split2(misc)
    pre = _dot(m0, wdec_ref[...]) + _dot(m1, wdec_ref[...]) + bdec_ref[...]
    log_a = jax.nn.log_sigmoid(pre) * (1.0 / GLA_TAU)
    g0, g1 = _split2(log_a)
    gcum = _dot(tri, g0) + _dot(tri, g1)
    roww = lax.broadcasted_iota(jnp.int32, (CHUNK, GLA_K_WIDTH), 0)
    qf = q_ref[...].astype(F32) * (GLA_KEY_DIM ** -0.5)
    kf = k_ref[...].astype(F32)
    nsub = CHUNK // GLA_SUB
    g_start = _block_rows(gcum, [None] + [GLA_SUB * i - 1 for i in range(1, nsub)], GLA_SUB)
    e0 = gcum - g_start
    q_lv = [(qf * jnp.exp(e0)).astype(BF16)]
    k_lv = [(kf * jnp.exp(-e0)).astype(BF16)]
    shifts = (4, 5, 6)
    for sh in shifts:
        b = 1 << sh
        bound = _block_rows(gcum, [2 * b * i + b - 1 for i in range(CHUNK // (2 * b))], 2 * b)
        second = ((roww >> sh) & 1) == 1
        x_l = jnp.exp(jnp.where(second, gcum - bound, bound - gcum))
        q_lv.append((qf * x_l).astype(BF16))
        k_lv.append((kf * x_l).astype(BF16))
    masks = [jnp.logical_and((row >> 4) == (col >> 4), causal)]
    for sh in shifts:
        same = (row >> (sh + 1)) == (col >> (sh + 1))
        m = jnp.logical_and(same, jnp.logical_and(((row >> sh) & 1) == 1, ((col >> sh) & 1) == 0))
        masks.append(m)
    g_last = gcum[CHUNK - 1:CHUNK, :]
    q_in = (qf * jnp.exp(gcum)).astype(BF16)
    k_end = (kf * jnp.exp(g_last - gcum)).astype(BF16)
    dec = jnp.exp(g_last)
    vb = v_ref[...]
    for hh in range(GLA_HEADS):
        kl = slice(hh * GLA_KEY_DIM, (hh + 1) * GLA_KEY_DIM)
        vl = slice(hh * GLA_VAL_DIM, (hh + 1) * GLA_VAL_DIM)
        scores = jnp.zeros((CHUNK, CHUNK), F32)
        for lv in range(len(masks)):
            scores = scores + jnp.where(masks[lv], _dot_nt(q_lv[lv][:, kl], k_lv[lv][:, kl]), 0.0)
        v_h = vb[:, vl]
        s_t = gstate_ref[hh]
        o = _dot(scores.astype(BF16), v_h) + _dot_nt(q_in[:, kl], s_t.astype(BF16))
        gstate_ref[hh] = dec[:, kl] * s_t + _dot_tn(v_h, k_end[:, kl])
        o = o * lax.rsqrt(jnp.mean(o * o, axis=-1, keepdims=True) + EPS) * glaw_ref[...]
        o = o * g_ref[:, vl].astype(F32)
        out_ref[:, SSD_WIDTH + hh * GLA_VAL_DIM:SSD_WIDTH + (hh + 1) * GLA_VAL_DIM] = o.astype(out_ref.dtype)


def _mixer(z, xbc, q, k, v, g, misc, params, batch, nchunks):
    widths = (SSD_WIDTH, XBC_WIDTH, GLA_K_WIDTH, GLA_K_WIDTH, GLA_V_WIDTH, GLA_V_WIDTH, MISC_WIDTH)
    assert len(params) == N_MIXER_PARAMS
    nseq = 4 if batch % 4 == 0 else (2 if batch % 2 == 0 else 1)
    meta_block = z.shape[0] // CHUNK - 1

    def row_spec(w, s):
        return pl.BlockSpec(
            (CHUNK, w),
            lambda bp, c: (jnp.where(c == 0, meta_block, (bp * nseq + s) * (nchunks - 1) + c - 1), 0))

    par_spec = lambda p: pl.BlockSpec(p.shape, lambda bp, c: (0,) * p.ndim)
    seq = (nchunks - 1) * CHUNK
    width = SSD_WIDTH + GLA_V_WIDTH
    out = pl.pallas_call(
        functools.partial(_mixer_kernel, nseq=nseq),
        grid=(batch // nseq, nchunks),
        in_specs=[row_spec(w, s) for s in range(nseq) for w in widths] + [par_spec(p) for p in params],
        out_specs=pl.BlockSpec((nseq, CHUNK, width), lambda bp, c: (bp, jnp.maximum(c - 1, 0), 0)),
        out_shape=jax.ShapeDtypeStruct((batch, seq, width), BF16),
        scratch_shapes=[
            pltpu.VMEM((nseq, CHUNK + 8, XBC_WIDTH), BF16),
            pltpu.VMEM((nseq, SSD_GROUPS, SSD_STATE, SSD_WIDTH // SSD_GROUPS), F32),
            pltpu.VMEM((nseq, GLA_HEADS, GLA_VAL_DIM, GLA_KEY_DIM), F32),
        ],
        compiler_params=pltpu.CompilerParams(
            dimension_semantics=("arbitrary", "arbitrary"), vmem_limit_bytes=VMEM_LIMIT),
        name="mixer",
    )(*([z, xbc, q, k, v, g, misc] * nseq), *params)
    return out.reshape(batch * seq, width)


def _post_kernel(x_ref, mixed_ref, wout_ref, fnw_ref, wrt_ref, br_ref,
                 h2_ref, t_ref, idx_ref, gate_ref, lrank_ref, cnt_ref, zero_ref, tprev_ref):
    @pl.when(pl.program_id(0) == 0)
    def _():
        tprev_ref[...] = jnp.zeros_like(tprev_ref)

    t_prev = tprev_ref[...]
    h2 = x_ref[...] + _dot(mixed_ref[...], wout_ref[...])
    h2_ref[...] = h2
    t = h2 * lax.rsqrt(jnp.mean(h2 * h2, axis=-1, keepdims=True) + EPS) * fnw_ref[...]
    t_ref[...] = t.astype(BF16)
    tprev_ref[...] = t

    t0, t1 = _split2(t_prev)
    w0 = wrt_ref[0]
    w1 = wrt_ref[1]
    logits = _dot_nt(w0, t0) + _dot_nt(w0, t1) + _dot_nt(w1, t0) + br_ref[...]
    e_iota = lax.broadcasted_iota(jnp.int32, logits.shape, 0)
    work = logits
    sel_any = jnp.zeros(logits.shape, F32)
    tops = []
    idxs = []
    onehots = []
    for _ in range(TOP_K):
        m = jnp.max(work, axis=0, keepdims=True)
        idx = jnp.min(jnp.where(work == m, e_iota, N_EXPERTS), axis=0, keepdims=True)
        hit = e_iota == idx
        tops.append(m)
        idxs.append(idx)
        onehots.append(hit)
        sel_any = sel_any + jnp.where(hit, 1.0, 0.0)
        work = jnp.where(hit, -jnp.inf, work)
    exps = [jnp.exp(tv - tops[0]) for tv in tops]
    denom = exps[0] + exps[1] + exps[2] + exps[3]
    tile = logits.shape[1]
    pad_rows = 8 - TOP_K
    gate_ref[...] = jnp.concatenate([e / denom for e in exps] + [jnp.zeros((pad_rows, tile), F32)], axis=0)
    idx_ref[...] = jnp.concatenate(idxs + [jnp.zeros((pad_rows, tile), jnp.int32)], axis=0)

    r_i = lax.broadcasted_iota(jnp.int32, (tile, tile), 0)
    c_i = lax.broadcasted_iota(jnp.int32, (tile, tile), 1)
    upper = jnp.where(r_i < c_i, 1.0, 0.0).astype(BF16)
    before = _dot(sel_any.astype(BF16), upper)
    ranks = [jnp.sum(jnp.where(h, before, 0.0), axis=0, keepdims=True) for h in onehots]
    ranks.append(jnp.full((pad_rows, tile), float(NO_ROW), F32))
    lrank_ref[...] = jnp.concatenate(ranks, axis=0).astype(jnp.int32)
    cnt = jnp.sum(sel_any, axis=1, keepdims=True)
    cnt_ref[...] = jnp.broadcast_to(cnt, cnt_ref.shape).astype(jnp.int32)
    zero_ref[...] = jnp.zeros_like(zero_ref)


def _post(x_flat, mixed, w_out_b, ffn_norm_w, w_router_t, b_router_col, zero_rows):
    n = x_flat.shape[0]
    tt = TOKEN_TILE
    n_tiles = n // tt
    const = lambda shape: pl.BlockSpec(shape, lambda i: (0,) * len(shape))
    cur = lambda i: jnp.minimum(i, n_tiles - 1)
    prev = lambda i: jnp.maximum(i - 1, 0)
    return pl.pallas_call(
        _post_kernel,
        grid=(n_tiles + 1,),
        in_specs=[
            pl.BlockSpec((tt, D_MODEL), lambda i: (cur(i), 0)),
            pl.BlockSpec((tt, SSD_WIDTH + GLA_V_WIDTH), lambda i: (cur(i), 0)),
            const(w_out_b.shape),
            const((1, D_MODEL)),
            const(w_router_t.shape),
            const((N_EXPERTS, 1)),
        ],
        out_specs=[
            pl.BlockSpec((tt, D_MODEL), lambda i: (cur(i), 0)),
            pl.BlockSpec((tt, D_MODEL), lambda i: (cur(i), 0)),
            pl.BlockSpec((8, tt), lambda i: (0, prev(i))),
            pl.BlockSpec((8, tt), lambda i: (0, prev(i))),
            pl.BlockSpec((8, tt), lambda i: (0, prev(i))),
            pl.BlockSpec((N_EXPERTS, 128), lambda i: (prev(i), 0)),
            pl.BlockSpec((zero_rows, HALF), lambda i: (i, 0)),
        ],
        out_shape=[
            jax.ShapeDtypeStruct((n, D_MODEL), F32),
            jax.ShapeDtypeStruct((n, D_MODEL), BF16),
            jax.ShapeDtypeStruct((8, n), jnp.int32),
            jax.ShapeDtypeStruct((8, n), F32),
            jax.ShapeDtypeStruct((8, n), jnp.int32),
            jax.ShapeDtypeStruct((n_tiles * N_EXPERTS, 128), jnp.int32),
            jax.ShapeDtypeStruct(((n_tiles + 1) * zero_rows, HALF), jnp.uint32),
        ],
        scratch_shapes=[pltpu.VMEM((tt, D_MODEL), F32)],
        compiler_params=pltpu.CompilerParams(
            dimension_semantics=("arbitrary",), vmem_limit_bytes=VMEM_LIMIT),
        name="post",
    )(x_flat, mixed, w_out_b, ffn_norm_w, w_router_t, b_router_col)


def _pack_pairs(x):
    lo = pltpu.bitcast(x[:, :HALF], jnp.uint32) >> 16
    hi = pltpu.bitcast(x[:, HALF:], jnp.uint32) & jnp.uint32(0xFFFF0000)
    return lo | hi


def _unpack_pairs(w):
    lo = pltpu.bitcast(w << 16, F32).astype(BF16)
    hi = pltpu.bitcast(w & jnp.uint32(0xFFFF0000), F32).astype(BF16)
    return lo, hi


def _sorted_positions(idx, lrank, offs_ref, base):
    pos = lrank
    for e in range(N_EXPERTS):
        pos = pos + jnp.where(idx == e, offs_ref[base + e], 0)
    return pos


def _group_copies(dst_ref, make_copy):
    return [make_copy(g * SEG_ALIGN, pl.multiple_of(dst_ref[0, 0, g], SEG_ALIGN))
            for g in range(SORT_GROUPS)]


def _dispatch_kernel(offs_ref, idx_ref, lrank_ref, t_ref, dst_ref, dst_prev_ref, buf_in_ref, buf_ref,
                     sorted_ref, sem):
    del buf_in_ref
    i = pl.program_id(0)
    slot = i % 2
    base = i * N_EXPERTS
    tt = t_ref.shape[0]
    pos = _sorted_positions(idx_ref[...], lrank_ref[...], offs_ref, base)
    t = t_ref[...]
    for r0 in range(0, SORT_ROWS, tt):
        prow = lax.broadcasted_iota(jnp.int32, (tt, tt), 0) + r0
        perm = jnp.zeros((tt, tt), F32)
        for kk in range(TOP_K):
            perm = jnp.where(prow == pos[kk:kk + 1, :], 1.0, perm)
        sorted_ref[slot, r0:r0 + tt, :] = _pack_pairs(_dot(perm.astype(BF16), t))

    def copies(table_ref, which):
        def make_copy(local, glob):
            return pltpu.make_async_copy(
                sorted_ref.at[which, pl.ds(local, SEG_ALIGN), :],
                buf_ref.at[pl.ds(glob, SEG_ALIGN), :], sem.at[which])
        return _group_copies(table_ref, make_copy)

    for cp in copies(dst_ref, slot):
        cp.start()

    @pl.when(i > 0)
    def _():
        for cp in copies(dst_prev_ref, 1 - slot):
            cp.wait()

    @pl.when(i == pl.num_programs(0) - 1)
    def _():
        for cp in copies(dst_ref, slot):
            cp.wait()


def _dispatch(offs, group_dst, top_idx, lrank, t_b, buf0):
    n = t_b.shape[0]
    tt = TOKEN_TILE
    table = lambda index: pl.BlockSpec((1, 1, SORT_GROUPS), index, memory_space=pltpu.SMEM)
    grid_spec = pltpu.PrefetchScalarGridSpec(
        num_scalar_prefetch=1,
        grid=(n // tt,),
        in_specs=[
            pl.BlockSpec((8, tt), lambda i, *_: (0, i)),
            pl.BlockSpec((8, tt), lambda i, *_: (0, i)),
            pl.BlockSpec((tt, D_MODEL), lambda i, *_: (i, 0)),
            table(lambda i, *_: (i, 0, 0)),
            table(lambda i, *_: (jnp.maximum(i - 1, 0), 0, 0)),
            pl.BlockSpec(memory_space=pl.ANY),
        ],
        out_specs=pl.BlockSpec(memory_space=pl.ANY),
        scratch_shapes=[pltpu.VMEM((2, SORT_ROWS, HALF), jnp.uint32), pltpu.SemaphoreType.DMA((2,))],
    )
    return pl.pallas_call(
        _dispatch_kernel,
        grid_spec=grid_spec,
        out_shape=jax.ShapeDtypeStruct(buf0.shape, jnp.uint32),
        input_output_aliases={6: 0},
        compiler_params=pltpu.CompilerParams(
            dimension_semantics=("arbitrary",), vmem_limit_bytes=VMEM_LIMIT),
        name="dispatch",
    )(offs, top_idx, lrank, t_b, group_dst, group_dst, buf0)


def _expert_kernel(be_ref, nused_ref, xp_ref, wgu_ref, bgu_ref, wd_ref, bd_ref, y_ref,
                   wgu_b_ref, wd_b_ref):
    i = pl.program_id(0)
    used = i < nused_ref[0]

    @pl.when(jnp.logical_not(used))
    def _():
        y_ref[...] = jnp.zeros_like(y_ref)

    new_expert = jnp.logical_or(i == 0, be_ref[i] != be_ref[jnp.maximum(i - 1, 0)])

    @pl.when(jnp.logical_and(used, new_expert))
    def _():
        rows = 128
        for r0 in range(0, D_MODEL, rows):
            wgu_b_ref[r0:r0 + rows, :] = wgu_ref[r0:r0 + rows, :].astype(BF16)
        for r0 in range(0, D_FF, rows):
            wd_b_ref[r0:r0 + rows, :] = wd_ref[r0:r0 + rows, :].astype(BF16)

    @pl.when(used)
    def _():
        sub = MOE_BLOCK // EXPERT_SUBBLOCKS
        for r0 in range(0, MOE_BLOCK, sub):
            x_lo, x_hi = _unpack_pairs(xp_ref[r0:r0 + sub, :])
            hgu = _dot(x_lo, wgu_b_ref[:HALF, :]) + _dot(x_hi, wgu_b_ref[HALF:, :]) + bgu_ref[...]
            gate = jnp.minimum(hgu[:, :D_FF], SWIGLU_LIMIT)
            up = jnp.clip(hgu[:, D_FF:], -SWIGLU_LIMIT, SWIGLU_LIMIT)
            act = gate * jax.nn.sigmoid(SWIGLU_ALPHA * gate)
            y = _dot(((up + 1.0) * act).astype(BF16), wd_b_ref[...]) + bd_ref[...]
            y_ref[r0:r0 + sub, :] = _pack_pairs(y.astype(BF16).astype(F32))


def _experts(block_expert, n_used, buf, w_gu, b_gu, w_d, b_d):
    n_blocks = block_expert.shape[0]
    rows = n_blocks * MOE_BLOCK
    row_map = lambda i, be, nu: (jnp.minimum(i, nu[0] - 1), 0)
    w_map = lambda i, be, nu: (be[jnp.minimum(i, nu[0] - 1)], 0, 0)
    grid_spec = pltpu.PrefetchScalarGridSpec(
        num_scalar_prefetch=2,
        grid=(n_blocks,),
        in_specs=[
            pl.BlockSpec((MOE_BLOCK, HALF), row_map),
            pl.BlockSpec((None, D_MODEL, 2 * D_FF), w_map),
            pl.BlockSpec((None, 1, 2 * D_FF), w_map),
            pl.BlockSpec((None, D_FF, D_MODEL), w_map),
            pl.BlockSpec((None, 1, D_MODEL), w_map),
        ],
        out_specs=pl.BlockSpec((MOE_BLOCK, HALF), lambda i, be, nu: (i, 0)),
        scratch_shapes=[pltpu.VMEM((D_MODEL, 2 * D_FF), BF16), pltpu.VMEM((D_FF, D_MODEL), BF16)],
    )
    return pl.pallas_call(
        _expert_kernel,
        grid_spec=grid_spec,
        out_shape=jax.ShapeDtypeStruct((rows, HALF), jnp.uint32),
        compiler_params=pltpu.CompilerParams(
            dimension_semantics=("arbitrary",), vmem_limit_bytes=VMEM_LIMIT),
        name="experts",
    )(block_expert, n_used, buf, w_gu, b_gu, w_d, b_d)


def _combine_kernel(offs_ref, idx_ref, lrank_ref, gate_ref, h2_ref, fw_ref, dst_ref, dst_next_ref,
                    y_hbm_ref, out_ref, ys_ref, sem):
    i = pl.program_id(0)
    slot = i % 2
    base = i * N_EXPERTS
    tt = h2_ref.shape[0]

    def copies(table_ref, which):
        def make_copy(local, glob):
            return pltpu.make_async_copy(
                y_hbm_ref.at[pl.ds(glob, SEG_ALIGN), :],
                ys_ref.at[which, pl.ds(local, SEG_ALIGN), :], sem.at[which])
        return _group_copies(table_ref, make_copy)

    @pl.when(i == 0)
    def _():
        for cp in copies(dst_ref, slot):
            cp.start()

    @pl.when(i < pl.num_programs(0) - 1)
    def _():
        for cp in copies(dst_next_ref, 1 - slot):
            cp.start()

    pos = _sorted_positions(idx_ref[...], lrank_ref[...], offs_ref, base).astype(F32)
    zpad = jnp.zeros((128 - 8, tt), F32)
    pos_c = jnp.concatenate([pos, zpad], axis=0).T
    gate_c = jnp.concatenate([gate_ref[...], zpad], axis=0).T
    lane = lax.broadcasted_iota(jnp.int32, (tt, SORT_ROWS), 1).astype(F32)
    pg = jnp.zeros((tt, SORT_ROWS), F32)
    for kk in range(TOP_K):
        pg = jnp.where(lane == pos_c[:, kk:kk + 1], gate_c[:, kk:kk + 1], pg)
    p_hi, p_lo = _split2(pg)

    for cp in copies(dst_ref, slot):
        cp.wait()
    y_lo, y_hi = _unpack_pairs(ys_ref[slot])
    ffn = jnp.concatenate(
        [_dot(p_hi, y_lo) + _dot(p_lo, y_lo), _dot(p_hi, y_hi) + _dot(p_lo, y_hi)], axis=1)
    acc = h2_ref[...] + ffn
    out_ref[...] = acc * lax.rsqrt(jnp.mean(acc * acc, axis=-1, keepdims=True) + EPS) * fw_ref[...]


def _combine(offs, group_dst, top_idx, lrank, gates, h2, final_norm_w, y_buf):
    n = h2.shape[0]
    tt = TOKEN_TILE
    grid_spec = pltpu.PrefetchScalarGridSpec(
        num_scalar_prefetch=1,
        grid=(n // tt,),
        in_specs=[
            pl.BlockSpec((8, tt), lambda i, *_: (0, i)),
            pl.BlockSpec((8, tt), lambda i, *_: (0, i)),
            pl.BlockSpec((8, tt), lambda i, *_: (0, i)),
            pl.BlockSpec((tt, D_MODEL), lambda i, *_: (i, 0)),
            pl.BlockSpec((1, D_MODEL), lambda i, *_: (0, 0)),
            pl.BlockSpec((1, 1, SORT_GROUPS), lambda i, *_: (i, 0, 0), memory_space=pltpu.SMEM),
            pl.BlockSpec((1, 1, SORT_GROUPS), lambda i, *_: (jnp.minimum(i + 1, n // tt - 1), 0, 0),
                         memory_space=pltpu.SMEM),
            pl.BlockSpec(memory_space=pl.ANY),
        ],
        out_specs=pl.BlockSpec((tt, D_MODEL), lambda i, *_: (i, 0)),
        scratch_shapes=[pltpu.VMEM((2, SORT_ROWS, HALF), jnp.uint32), pltpu.SemaphoreType.DMA((2,))],
    )
    return pl.pallas_call(
        _combine_kernel,
        grid_spec=grid_spec,
        out_shape=jax.ShapeDtypeStruct((n, D_MODEL), F32),
        compiler_params=pltpu.CompilerParams(
            dimension_semantics=("arbitrary",), vmem_limit_bytes=VMEM_LIMIT),
        name="combine",
    )(offs, top_idx, lrank, gates, h2, final_norm_w, group_dst, group_dst, y_buf)


def _pad_lanes(v, width):
    return jnp.pad(v, ((0, 0), (0, width - v.shape[1])))


def kernel(x, meta_tokens, mix_norm_w, w_in, conv_w, conv_b, dt_bias, a_log, d_skip, ssd_norm_w,
           w_decay_up, b_decay, gla_norm_w, w_out, ffn_norm_w, w_router, b_router, w_gate_up,
           b_gate_up, w_down, b_down, final_norm_w):
    batch, seq, d = x.shape
    assert d == D_MODEL and seq % TOKEN_TILE == 0
    assert mix_norm_w.shape[0] == 1, "single-layer block"
    nchunks = (FRONT_PAD + N_META + seq) // CHUNK

    n = batch * seq
    x_flat = x.reshape(n, D_MODEL)
    meta_tile = jnp.concatenate(
        [jnp.zeros((TOKEN_TILE - N_META, D_MODEL), x.dtype), meta_tokens.astype(x.dtype)], axis=0)

    wi = w_in[0]
    o_z, o_xbc = 0, SSD_WIDTH
    o_dt = o_xbc + XBC_WIDTH
    o_q = o_dt + SSD_HEADS
    o_k = o_q + GLA_K_WIDTH
    o_v = o_k + GLA_K_WIDTH
    o_g = o_v + GLA_V_WIDTH
    o_a = o_g + GLA_V_WIDTH
    w_misc = jnp.concatenate(
        [wi[:, o_dt:o_dt + SSD_HEADS], wi[:, o_a:o_a + GLA_RANK],
         jnp.zeros((D_MODEL, MISC_WIDTH - SSD_HEADS - GLA_RANK), wi.dtype)], axis=1)
    w_in_r = jnp.concatenate(
        [wi[:, o_z:o_dt], wi[:, o_q:o_a], w_misc], axis=1).astype(BF16)
    z, xbc, q, k, v, g, misc = _in_proj(x_flat, meta_tile, mix_norm_w[0][None, :], w_in_r)

    dtb = _pad_lanes(dt_bias[0][None, :].astype(F32), MISC_WIDTH)
    aneg = _pad_lanes(-jnp.exp(a_log[0].astype(F32))[None, :], MISC_WIDTH)
    dskip = jnp.repeat(d_skip[0].astype(F32), SSD_HEAD_DIM)[None, :]
    wdec = jnp.zeros((MISC_WIDTH, GLA_K_WIDTH), F32).at[SSD_HEADS:SSD_HEADS + GLA_RANK].set(w_decay_up[0])
    params = (conv_w[0], conv_b[0][None, :], dtb, aneg, dskip, ssd_norm_w[0][None, :],
              wdec.astype(BF16), b_decay[0][None, :], gla_norm_w[0][None, :])
    mixed = _mixer(z, xbc, q, k, v, g, misc, params, batch, nchunks)

    wr_hi, wr_lo = _split2(w_router[0].T.astype(F32))
    n_tiles = n // TOKEN_TILE
    n_blocks = -(-(n * TOP_K + n_tiles * N_EXPERTS * (SEG_ALIGN - 1)) // MOE_BLOCK) + N_EXPERTS
    spare = n_blocks * MOE_BLOCK
    total_rows = spare + 2 * SORT_ROWS
    zero_rows = -(-total_rows // ((n_tiles + 1) * SEG_ALIGN)) * SEG_ALIGN
    h2, t_b, top_idx, gates, lrank, tile_cnt, buf0 = _post(
        x_flat, mixed, w_out[0].astype(BF16), ffn_norm_w[0][None, :],
        jnp.stack([wr_hi, wr_lo]), b_router[0][:, None], zero_rows)

    tile_cnt = tile_cnt.reshape(n_tiles, N_EXPERTS, 128)[:, :, 0]
    seg_rows = (tile_cnt + SEG_ALIGN - 1) // SEG_ALIGN * SEG_ALIGN
    counts = jnp.sum(seg_rows, axis=0)
    padded = (counts + MOE_BLOCK - 1) // MOE_BLOCK * MOE_BLOCK
    pend = jnp.cumsum(padded)
    pstart = pend - padded
    dstart = pstart[None, :] + jnp.cumsum(seg_rows, axis=0) - seg_rows
    seg_end = jnp.cumsum(seg_rows, axis=1)
    offs = seg_end - seg_rows
    block_pos = jnp.arange(total_rows // MOE_BLOCK, dtype=jnp.int32) * MOE_BLOCK
    block_expert = jnp.minimum(
        jnp.sum((pend[None, :] <= block_pos[:, None]).astype(jnp.int32), axis=1), N_EXPERTS - 1)
    n_used = (pend[-1:] // MOE_BLOCK).astype(jnp.int32)
    grow = jnp.arange(SORT_GROUPS, dtype=jnp.int32)[None, :, None] * SEG_ALIGN
    inside = (offs[:, None, :] <= grow) & (grow < seg_end[:, None, :])
    group_dst = jnp.sum(jnp.where(inside, dstart[:, None, :] + grow - offs[:, None, :], 0), axis=2)
    parity = (jnp.arange(n_tiles, dtype=jnp.int32) % 2)[:, None]
    group_dst = jnp.where(jnp.any(inside, axis=2), group_dst, spare + parity * SORT_ROWS + grow[:, :, 0])
    group_dst = group_dst.astype(jnp.int32)[:, None, :]
    offs = offs.reshape(-1).astype(jnp.int32)

    buf = _dispatch(offs, group_dst, top_idx, lrank, t_b, buf0)
    y_buf = _experts(block_expert, n_used, buf, w_gate_up[0], b_gate_up[0][:, None, :],
                     w_down[0], b_down[0][:, None, :])
    out = _combine(offs, group_dst, top_idx, lrank, gates, h2, final_norm_w[None, :], y_buf)
    return out.reshape(batch, seq, D_MODEL)
```

```python
import functools

import jax
import jax.numpy as jnp
from jax import lax
from jax.experimental import pallas as pl
from jax.experimental.pallas import tpu as pltpu

F32 = jnp.float32
BF16 = jnp.bfloat16

D_MODEL = 1024
N_META = 16
EPS = 1e-5
SSD_HEAD_DIM = 64
SSD_HEADS = 16
SSD_GROUPS = 2
SSD_STATE = 128
SSD_CONV = 4
SSD_WIDTH = SSD_HEADS * SSD_HEAD_DIM
XBC_WIDTH = SSD_WIDTH + 2 * SSD_GROUPS * SSD_STATE
GLA_HEADS = 4
GLA_KEY_DIM = 128
GLA_VAL_DIM = 256
GLA_K_WIDTH = GLA_HEADS * GLA_KEY_DIM
GLA_V_WIDTH = GLA_HEADS * GLA_VAL_DIM
GLA_RANK = 16
GLA_TAU = 16.0
GLA_SUB = 16
N_EXPERTS = 32
TOP_K = 4
D_FF = D_MODEL
SWIGLU_LIMIT = 7.0
SWIGLU_ALPHA = 1.702
MOE_BLOCK = 512

CHUNK = 128
FRONT_PAD = CHUNK - N_META
MISC_WIDTH = 128
IN_PAD_WIDTH = SSD_WIDTH + XBC_WIDTH + 2 * GLA_K_WIDTH + 2 * GLA_V_WIDTH + MISC_WIDTH
TOKEN_TILE = 256
HALF = D_MODEL // 2
NO_ROW = 1 << 12
SEG_ALIGN = 8
SORT_ROWS = -(-(TOP_K * TOKEN_TILE + N_EXPERTS * (SEG_ALIGN - 1)) // TOKEN_TILE) * TOKEN_TILE
SORT_GROUPS = SORT_ROWS // SEG_ALIGN
VMEM_LIMIT = 56 * 1024 * 1024


def _split2(x):
    hi = x.astype(BF16)
    lo = (x - hi.astype(F32)).astype(BF16)
    return hi, lo


def _split3(x):
    hi = x.astype(BF16)
    r = x - hi.astype(F32)
    mid = r.astype(BF16)
    lo = (r - mid.astype(F32)).astype(BF16)
    return hi, mid, lo


def _dot(a, b):
    return jnp.dot(a, b, preferred_element_type=F32)


def _dot_nt(a, b):
    return lax.dot_general(a, b, (((1,), (1,)), ((), ())), preferred_element_type=F32)


def _dot_tn(a, b):
    return lax.dot_general(a, b, (((0,), (0,)), ((), ())), preferred_element_type=F32)


def _silu(x):
    return x * jax.nn.sigmoid(x)


_IN_SECTIONS = (SSD_WIDTH, XBC_WIDTH, GLA_K_WIDTH, GLA_K_WIDTH, GLA_V_WIDTH, GLA_V_WIDTH, MISC_WIDTH)
_GATE_SECTIONS = (0, 5)


def _in_proj_kernel(x_ref, meta_ref, nw_ref, w_ref, z_ref, xbc_ref, q_ref, k_ref, v_ref, g_ref, misc_ref):
    h = jnp.where(pl.program_id(0) < pl.num_programs(0) - 1, x_ref[...], meta_ref[...])
    u = h * lax.rsqrt(jnp.mean(h * h, axis=-1, keepdims=True) + EPS) * nw_ref[...]
    ub = u.astype(BF16)
    outs = (z_ref, xbc_ref, q_ref, k_ref, v_ref, g_ref, misc_ref)
    off = 0
    for section, (o_ref, width) in enumerate(zip(outs, _IN_SECTIONS)):
        p = _dot(ub, w_ref[:, off:off + width])
        if section in _GATE_SECTIONS:
            p = _silu(p)
        o_ref[...] = p.astype(o_ref.dtype)
        off += width


def _in_proj(x_flat, meta_tile, norm_w, w_in_r):
    tm = TOKEN_TILE
    nx = x_flat.shape[0] // tm
    rows = (nx + 1) * tm
    row_spec = lambda w: pl.BlockSpec((tm, w), lambda i: (i, 0))
    out_dtypes = (BF16,) * 6 + (F32,)
    return pl.pallas_call(
        _in_proj_kernel,
        grid=(nx + 1,),
        in_specs=[
            pl.BlockSpec((tm, D_MODEL), lambda i: (jnp.minimum(i, nx - 1), 0)),
            pl.BlockSpec((tm, D_MODEL), lambda i: (0, 0)),
            pl.BlockSpec((1, D_MODEL), lambda i: (0, 0)),
            pl.BlockSpec((D_MODEL, IN_PAD_WIDTH), lambda i: (0, 0)),
        ],
        out_specs=[row_spec(w) for w in _IN_SECTIONS],
        out_shape=[jax.ShapeDtypeStruct((rows, w), dt) for w, dt in zip(_IN_SECTIONS, out_dtypes)],
        compiler_params=pltpu.CompilerParams(
            dimension_semantics=("arbitrary",), vmem_limit_bytes=VMEM_LIMIT),
        name="in_proj",
    )(x_flat, meta_tile, norm_w, w_in_r)


def _block_rows(x, rows, width):
    parts = []
    for r in rows:
        if r is None:
            parts.append(jnp.zeros((width, x.shape[1]), x.dtype))
        else:
            parts.append(jnp.broadcast_to(x[r:r + 1, :], (width, x.shape[1])))
    return jnp.concatenate(parts, axis=0)


N_SEQ_INPUTS = 7
N_MIXER_PARAMS = 9


def _mixer_kernel(*refs, nseq):
    seq_refs = [refs[s * N_SEQ_INPUTS:(s + 1) * N_SEQ_INPUTS] for s in range(nseq)]
    rest = refs[nseq * N_SEQ_INPUTS:]
    param_refs = rest[:N_MIXER_PARAMS]
    out_ref, xext_ref, sstate_ref, gstate_ref = rest[N_MIXER_PARAMS:]
    c = pl.program_id(1)

    @pl.when(c == 0)
    def _():
        xext_ref[:, 0:8, :] = jnp.zeros((nseq, 8, XBC_WIDTH), BF16)
        sstate_ref[...] = jnp.zeros_like(sstate_ref)
        gstate_ref[...] = jnp.zeros_like(gstate_ref)

    for s in range(nseq):
        _mixer_chunk(c, *seq_refs[s], *param_refs, out_ref.at[s], xext_ref.at[s], sstate_ref.at[s],
                     gstate_ref.at[s])


def _mixer_chunk(c, z_ref, xbc_ref, q_ref, k_ref, v_ref, g_ref, misc_ref,
                 convw_ref, convb_ref, dtb_ref, aneg_ref, dskip_ref, ssdw_ref,
                 wdec_ref, bdec_ref, glaw_ref,
                 out_ref,
                 xext_ref, sstate_ref, gstate_ref):

    row = lax.broadcasted_iota(jnp.int32, (CHUNK, CHUNK), 0)
    col = lax.broadcasted_iota(jnp.int32, (CHUNK, CHUNK), 1)
    causal = row >= col
    tri = jnp.where(causal, 1.0, 0.0).astype(BF16)
    valid = jnp.logical_or(c > 0, row >= FRONT_PAD)

    xcur = xbc_ref[...]
    xext_ref[8:8 + CHUNK, :] = xcur
    xall = xext_ref[...]
    srow = lax.broadcasted_iota(jnp.int32, (CHUNK, CHUNK + 8), 0)
    scol = lax.broadcasted_iota(jnp.int32, (CHUNK, CHUNK + 8), 1)
    conv = convb_ref[...] + convw_ref[SSD_CONV - 1:SSD_CONV, :] * xcur.astype(F32)
    for kk in range(SSD_CONV - 1):
        shift = jnp.where(scol == srow + 5 + kk, 1.0, 0.0).astype(BF16)
        conv = conv + convw_ref[kk:kk + 1, :] * _dot(shift, xall)
    xext_ref[0:8, :] = xext_ref[CHUNK:CHUNK + 8, :]
    valid_w = jnp.logical_or(c > 0, lax.broadcasted_iota(jnp.int32, (CHUNK, XBC_WIDTH), 0) >= FRONT_PAD)
    act = jnp.where(valid_w, _silu(conv), 0.0)
    xs = act[:, :SSD_WIDTH]
    bmat = act[:, SSD_WIDTH:SSD_WIDTH + SSD_GROUPS * SSD_STATE].astype(BF16)
    cmat = act[:, SSD_WIDTH + SSD_GROUPS * SSD_STATE:].astype(BF16)

    misc = misc_ref[...]
    dt = jnp.where(valid, jax.nn.softplus(misc + dtb_ref[...]), 0.0)
    d_a = dt * aneg_ref[...]
    p0, p1, p2 = _split3(d_a)
    a_cs = _dot(tri, p0) + _dot(tri, p1) + _dot(tri, p2)
    a_cs_t = a_cs.T

    lo_half = col < SSD_HEAD_DIM
    hg = SSD_HEADS // SSD_GROUPS
    pairs_per_group = hg // 2
    y_parts = []
    for gi in range(SSD_GROUPS):
        b_g = bmat[:, gi * SSD_STATE:(gi + 1) * SSD_STATE]
        c_g = cmat[:, gi * SSD_STATE:(gi + 1) * SSD_STATE]
        c_g32 = c_g.astype(F32)
        cb = _dot_nt(c_g, b_g)
        w_parts = []
        cd_parts = []
        for pj in range(pairs_per_group):
            h0 = gi * hg + 2 * pj
            h1 = h0 + 1
            lanes = slice(h0 * SSD_HEAD_DIM, (h1 + 1) * SSD_HEAD_DIM)
            ab0 = jnp.broadcast_to(a_cs[:, h0:h0 + 1], (CHUNK, CHUNK))
            ab1 = jnp.broadcast_to(a_cs[:, h1:h1 + 1], (CHUNK, CHUNK))
            db0 = jnp.broadcast_to(dt[:, h0:h0 + 1], (CHUNK, CHUNK))
            db1 = jnp.broadcast_to(dt[:, h1:h1 + 1], (CHUNK, CHUNK))
            l0 = jnp.exp(jnp.where(causal, ab0 - a_cs_t[h0:h0 + 1, :], -jnp.inf))
            l1 = jnp.exp(jnp.where(causal, ab1 - a_cs_t[h1:h1 + 1, :], -jnp.inf))
            lhs = jnp.concatenate(
                [(cb * l0).astype(BF16), (c_g32 * jnp.exp(ab0)).astype(BF16),
                 (cb * l1).astype(BF16), (c_g32 * jnp.exp(ab1)).astype(BF16)], axis=1)
            xs_p = xs[:, lanes]
            dtx = xs_p * jnp.where(lo_half, db0, db1)
            st = sstate_ref[gi, :, pj * CHUNK:(pj + 1) * CHUNK]
            rhs = jnp.concatenate(
                [jnp.where(lo_half, dtx, 0.0).astype(BF16), jnp.where(lo_half, st, 0.0).astype(BF16),
                 jnp.where(lo_half, 0.0, dtx).astype(BF16), jnp.where(lo_half, 0.0, st).astype(BF16)],
                axis=0)
            y_p = _dot(lhs, rhs) + dskip_ref[:, lanes] * xs_p
            y_parts.append(y_p)
            ae = jnp.where(lo_half, ab0, ab1)
            a_last = ae[CHUNK - 1:CHUNK, :]
            w_parts.append((dtx * jnp.exp(a_last - ae)).astype(BF16))
            cd_parts.append(jnp.exp(a_last))
        upd = _dot_tn(b_g, jnp.concatenate(w_parts, axis=1))
        sstate_ref[gi] = jnp.concatenate(cd_parts, axis=1) * sstate_ref[gi] + upd
    y = jnp.concatenate(y_parts, axis=1)
    y = y * z_ref[...].astype(F32)
    gsz = SSD_WIDTH // SSD_GROUPS
    for gi in range(SSD_GROUPS):
        yg = y[:, gi * gsz:(gi + 1) * gsz]
        yn = yg * lax.rsqrt(jnp.mean(yg * yg, axis=-1, keepdims=True) + EPS)
        out_ref[:, gi * gsz:(gi + 1) * gsz] = (yn * ssdw_ref[:, gi * gsz:(gi + 1) * gsz]).astype(out_ref.dtype)

    m0, m1 = _split2(misc)
    pre = _dot(m0, wdec_ref[...]) + _dot(m1, wdec_ref[...]) + bdec_ref[...]
    log_a = jax.nn.log_sigmoid(pre) * (1.0 / GLA_TAU)
    g0, g1 = _split2(log_a)
    gcum = _dot(tri, g0) + _dot(tri, g1)
    roww = lax.broadcasted_iota(jnp.int32, (CHUNK, GLA_K_WIDTH), 0)
    qf = q_ref[...].astype(F32) * (GLA_KEY_DIM ** -0.5)
    kf = k_ref[...].astype(F32)
    nsub = CHUNK // GLA_SUB
    g_start = _block_rows(gcum, [None] + [GLA_SUB * i - 1 for i in range(1, nsub)], GLA_SUB)
    e0 = gcum - g_start
    q_lv = [(qf * jnp.exp(e0)).astype(BF16)]
    k_lv = [(kf * jnp.exp(-e0)).astype(BF16)]
    shifts = (4, 5, 6)
    for sh in shifts:
        b = 1 << sh
        bound = _block_rows(gcum, [2 * b * i + b - 1 for i in range(CHUNK // (2 * b))], 2 * b)
        second = ((roww >> sh) & 1) == 1
        x_l = jnp.exp(jnp.where(second, gcum - bound, bound - gcum))
        q_lv.append((qf * x_l).astype(BF16))
        k_lv.append((kf * x_l).astype(BF16))
    masks = [jnp.logical_and((row >> 4) == (col >> 4), causal)]
    for sh in shifts:
        same = (row >> (sh + 1)) == (col >> (sh + 1))
        m = jnp.logical_and(same, jnp.logical_and(((row >> sh) & 1) == 1, ((col >> sh) & 1) == 0))
        masks.append(m)
    g_last = gcum[CHUNK - 1:CHUNK, :]
    q_in = (qf * jnp.exp(gcum)).astype(BF16)
    k_end = (kf * jnp.exp(g_last - gcum)).astype(BF16)
    dec = jnp.exp(g_last)
    vb = v_ref[...]
    for hh in range(GLA_HEADS):
        kl = slice(hh * GLA_KEY_DIM, (hh + 1) * GLA_KEY_DIM)
        vl = slice(hh * GLA_VAL_DIM, (hh + 1) * GLA_VAL_DIM)
        scores = jnp.zeros((CHUNK, CHUNK), F32)
        for lv in range(len(masks)):
            scores = scores + jnp.where(masks[lv], _dot_nt(q_lv[lv][:, kl], k_lv[lv][:, kl]), 0.0)
        v_h = vb[:, vl]
        s_t = gstate_ref[hh]
        o = _dot(scores.astype(BF16), v_h) + _dot_nt(q_in[:, kl], s_t.astype(BF16))
        gstate_ref[hh] = dec[:, kl] * s_t + _dot_tn(v_h, k_end[:, kl])
        o = o * lax.rsqrt(jnp.mean(o * o, axis=-1, keepdims=True) + EPS) * glaw_ref[...]
        o = o * g_ref[:, vl].astype(F32)
        out_ref[:, SSD_WIDTH + hh * GLA_VAL_DIM:SSD_WIDTH + (hh + 1) * GLA_VAL_DIM] = o.astype(out_ref.dtype)


def _mixer(z, xbc, q, k, v, g, misc, params, batch, nchunks):
    widths = (SSD_WIDTH, XBC_WIDTH, GLA_K_WIDTH, GLA_K_WIDTH, GLA_V_WIDTH, GLA_V_WIDTH, MISC_WIDTH)
    assert len(params) == N_MIXER_PARAMS
    nseq = 4 if batch % 4 == 0 else (2 if batch % 2 == 0 else 1)
    meta_block = z.shape[0] // CHUNK - 1

    def row_spec(w, s):
        return pl.BlockSpec(
            (CHUNK, w),
            lambda bp, c: (jnp.where(c == 0, meta_block, (bp * nseq + s) * (nchunks - 1) + c - 1), 0))

    par_spec = lambda p: pl.BlockSpec(p.shape, lambda bp, c: (0,) * p.ndim)
    seq = (nchunks - 1) * CHUNK
    width = SSD_WIDTH + GLA_V_WIDTH
    out = pl.pallas_call(
        functools.partial(_mixer_kernel, nseq=nseq),
        grid=(batch // nseq, nchunks),
        in_specs=[row_spec(w, s) for s in range(nseq) for w in widths] + [par_spec(p) for p in params],
        out_specs=pl.BlockSpec((nseq, CHUNK, width), lambda bp, c: (bp, jnp.maximum(c - 1, 0), 0)),
        out_shape=jax.ShapeDtypeStruct((batch, seq, width), BF16),
        scratch_shapes=[
            pltpu.VMEM((nseq, CHUNK + 8, XBC_WIDTH), BF16),
            pltpu.VMEM((nseq, SSD_GROUPS, SSD_STATE, SSD_WIDTH // SSD_GROUPS), F32),
            pltpu.VMEM((nseq, GLA_HEADS, GLA_VAL_DIM, GLA_KEY_DIM), F32),
        ],
        compiler_params=pltpu.CompilerParams(
            dimension_semantics=("arbitrary", "arbitrary"), vmem_limit_bytes=VMEM_LIMIT),
        name="mixer",
    )(*([z, xbc, q, k, v, g, misc] * nseq), *params)
    return out.reshape(batch * seq, width)


def _post_kernel(x_ref, mixed_ref, wout_ref, fnw_ref, wrt_ref, br_ref,
                 h2_ref, t_ref, idx_ref, gate_ref, lrank_ref, cnt_ref, zero_ref, tprev_ref):
    @pl.when(pl.program_id(0) == 0)
    def _():
        tprev_ref[...] = jnp.zeros_like(tprev_ref)

    t_prev = tprev_ref[...]
    h2 = x_ref[...] + _dot(mixed_ref[...], wout_ref[...])
    h2_ref[...] = h2
    t = h2 * lax.rsqrt(jnp.mean(h2 * h2, axis=-1, keepdims=True) + EPS) * fnw_ref[...]
    t_ref[...] = t.astype(BF16)
    tprev_ref[...] = t

    t0, t1 = _split2(t_prev)
    w0 = wrt_ref[0]
    w1 = wrt_ref[1]
    logits = _dot_nt(w0, t0) + _dot_nt(w0, t1) + _dot_nt(w1, t0) + br_ref[...]
    e_iota = lax.broadcasted_iota(jnp.int32, logits.shape, 0)
    work = logits
    sel_any = jnp.zeros(logits.shape, F32)
    tops = []
    idxs = []
    onehots = []
    for _ in range(TOP_K):
        m = jnp.max(work, axis=0, keepdims=True)
        idx = jnp.min(jnp.where(work == m, e_iota, N_EXPERTS), axis=0, keepdims=True)
        hit = e_iota == idx
        tops.append(m)
        idxs.append(idx)
        onehots.append(hit)
        sel_any = sel_any + jnp.where(hit, 1.0, 0.0)
        work = jnp.where(hit, -jnp.inf, work)
    exps = [jnp.exp(tv - tops[0]) for tv in tops]
    denom = exps[0] + exps[1] + exps[2] + exps[3]
    tile = logits.shape[1]
    pad_rows = 8 - TOP_K
    gate_ref[...] = jnp.concatenate([e / denom for e in exps] + [jnp.zeros((pad_rows, tile), F32)], axis=0)
    idx_ref[...] = jnp.concatenate(idxs + [jnp.zeros((pad_rows, tile), jnp.int32)], axis=0)

    r_i = lax.broadcasted_iota(jnp.int32, (tile, tile), 0)
    c_i = lax.broadcasted_iota(jnp.int32, (tile, tile), 1)
    upper = jnp.where(r_i < c_i, 1.0, 0.0).astype(BF16)
    before = _dot(sel_any.astype(BF16), upper)
    ranks = [jnp.sum(jnp.where(h, before, 0.0), axis=0, keepdims=True) for h in onehots]
    ranks.append(jnp.full((pad_rows, tile), float(NO_ROW), F32))
    lrank_ref[...] = jnp.concatenate(ranks, axis=0).astype(jnp.int32)
    cnt = jnp.sum(sel_any, axis=1, keepdims=True)
    cnt_ref[...] = jnp.broadcast_to(cnt, cnt_ref.shape).astype(jnp.int32)
    zero_ref[...] = jnp.zeros_like(zero_ref)


def _post(x_flat, mixed, w_out_b, ffn_norm_w, w_router_t, b_router_col, zero_rows):
    n = x_flat.shape[0]
    tt = TOKEN_TILE
    n_tiles = n // tt
    const = lambda shape: pl.BlockSpec(shape, lambda i: (0,) * len(shape))
    cur = lambda i: jnp.minimum(i, n_tiles - 1)
    prev = lambda i: jnp.maximum(i - 1, 0)
    return pl.pallas_call(
        _post_kernel,
        grid=(n_tiles + 1,),
        in_specs=[
            pl.BlockSpec((tt, D_MODEL), lambda i: (cur(i), 0)),
            pl.BlockSpec((tt, SSD_WIDTH + GLA_V_WIDTH), lambda i: (cur(i), 0)),
            const(w_out_b.shape),
            const((1, D_MODEL)),
            const(w_router_t.shape),
            const((N_EXPERTS, 1)),
        ],
        out_specs=[
            pl.BlockSpec((tt, D_MODEL), lambda i: (cur(i), 0)),
            pl.BlockSpec((tt, D_MODEL), lambda i: (cur(i), 0)),
            pl.BlockSpec((8, tt), lambda i: (0, prev(i))),
            pl.BlockSpec((8, tt), lambda i: (0, prev(i))),
            pl.BlockSpec((8, tt), lambda i: (0, prev(i))),
            pl.BlockSpec((N_EXPERTS, 128), lambda i: (prev(i), 0)),
            pl.BlockSpec((zero_rows, HALF), lambda i: (i, 0)),
        ],
        out_shape=[
            jax.ShapeDtypeStruct((n, D_MODEL), F32),
            jax.ShapeDtypeStruct((n, D_MODEL), BF16),
            jax.ShapeDtypeStruct((8, n), jnp.int32),
            jax.ShapeDtypeStruct((8, n), F32),
            jax.ShapeDtypeStruct((8, n), jnp.int32),
            jax.ShapeDtypeStruct((n_tiles * N_EXPERTS, 128), jnp.int32),
            jax.ShapeDtypeStruct(((n_tiles + 1) * zero_rows, HALF), jnp.uint32),
        ],
        scratch_shapes=[pltpu.VMEM((tt, D_MODEL), F32)],
        compiler_params=pltpu.CompilerParams(
            dimension_semantics=("arbitrary",), vmem_limit_bytes=VMEM_LIMIT),
        name="post",
    )(x_flat, mixed, w_out_b, ffn_norm_w, w_router_t, b_router_col)


def _pack_pairs(x):
    lo = pltpu.bitcast(x[:, :HALF], jnp.uint32) >> 16
    hi = pltpu.bitcast(x[:, HALF:], jnp.uint32) & jnp.uint32(0xFFFF0000)
    return lo | hi


def _unpack_pairs(w):
    lo = pltpu.bitcast(w << 16, F32).astype(BF16)
    hi = pltpu.bitcast(w & jnp.uint32(0xFFFF0000), F32).astype(BF16)
    return lo, hi


def _sorted_positions(idx, lrank, offs_ref, base):
    pos = lrank
    for e in range(N_EXPERTS):
        pos = pos + jnp.where(idx == e, offs_ref[base + e], 0)
    return pos


def _group_copies(dst_ref, make_copy):
    return [make_copy(g * SEG_ALIGN, pl.multiple_of(dst_ref[0, 0, g], SEG_ALIGN))
            for g in range(SORT_GROUPS)]


def _dispatch_kernel(offs_ref, idx_ref, lrank_ref, t_ref, dst_ref, dst_prev_ref, buf_in_ref, buf_ref,
                     sorted_ref, sem):
    del buf_in_ref
    i = pl.program_id(0)
    slot = i % 2
    base = i * N_EXPERTS
    tt = t_ref.shape[0]
    pos = _sorted_positions(idx_ref[...], lrank_ref[...], offs_ref, base)
    t = t_ref[...]
    for r0 in range(0, SORT_ROWS, tt):
        prow = lax.broadcasted_iota(jnp.int32, (tt, tt), 0) + r0
        perm = jnp.zeros((tt, tt), F32)
        for kk in range(TOP_K):
            perm = jnp.where(prow == pos[kk:kk + 1, :], 1.0, perm)
        sorted_ref[slot, r0:r0 + tt, :] = _pack_pairs(_dot(perm.astype(BF16), t))

    def copies(table_ref, which):
        def make_copy(local, glob):
            return pltpu.make_async_copy(
                sorted_ref.at[which, pl.ds(local, SEG_ALIGN), :],
                buf_ref.at[pl.ds(glob, SEG_ALIGN), :], sem.at[which])
        return _group_copies(table_ref, make_copy)

    for cp in copies(dst_ref, slot):
        cp.start()

    @pl.when(i > 0)
    def _():
        for cp in copies(dst_prev_ref, 1 - slot):
            cp.wait()

    @pl.when(i == pl.num_programs(0) - 1)
    def _():
        for cp in copies(dst_ref, slot):
            cp.wait()


def _dispatch(offs, group_dst, top_idx, lrank, t_b, buf0):
    n = t_b.shape[0]
    tt = TOKEN_TILE
    table = lambda index: pl.BlockSpec((1, 1, SORT_GROUPS), index, memory_space=pltpu.SMEM)
    grid_spec = pltpu.PrefetchScalarGridSpec(
        num_scalar_prefetch=1,
        grid=(n // tt,),
        in_specs=[
            pl.BlockSpec((8, tt), lambda i, *_: (0, i)),
            pl.BlockSpec((8, tt), lambda i, *_: (0, i)),
            pl.BlockSpec((tt, D_MODEL), lambda i, *_: (i, 0)),
            table(lambda i, *_: (i, 0, 0)),
            table(lambda i, *_: (jnp.maximum(i - 1, 0), 0, 0)),
            pl.BlockSpec(memory_space=pl.ANY),
        ],
        out_specs=pl.BlockSpec(memory_space=pl.ANY),
        scratch_shapes=[pltpu.VMEM((2, SORT_ROWS, HALF), jnp.uint32), pltpu.SemaphoreType.DMA((2,))],
    )
    return pl.pallas_call(
        _dispatch_kernel,
        grid_spec=grid_spec,
        out_shape=jax.ShapeDtypeStruct(buf0.shape, jnp.uint32),
        input_output_aliases={6: 0},
        compiler_params=pltpu.CompilerParams(
            dimension_semantics=("arbitrary",), vmem_limit_bytes=VMEM_LIMIT),
        name="dispatch",
    )(offs, top_idx, lrank, t_b, group_dst, group_dst, buf0)


def _expert_kernel(be_ref, nused_ref, xp_ref, wgu_ref, bgu_ref, wd_ref, bd_ref, y_ref,
                   wgu_b_ref, wd_b_ref):
    i = pl.program_id(0)
    used = i < nused_ref[0]

    @pl.when(jnp.logical_not(used))
    def _():
        y_ref[...] = jnp.zeros_like(y_ref)

    new_expert = jnp.logical_or(i == 0, be_ref[i] != be_ref[jnp.maximum(i - 1, 0)])

    @pl.when(jnp.logical_and(used, new_expert))
    def _():
        rows = 128
        for r0 in range(0, D_MODEL, rows):
            wgu_b_ref[r0:r0 + rows, :] = wgu_ref[r0:r0 + rows, :].astype(BF16)
        for r0 in range(0, D_FF, rows):
            wd_b_ref[r0:r0 + rows, :] = wd_ref[r0:r0 + rows, :].astype(BF16)

    @pl.when(used)
    def _():
        x_lo, x_hi = _unpack_pairs(xp_ref[...])
        hgu = _dot(x_lo, wgu_b_ref[:HALF, :]) + _dot(x_hi, wgu_b_ref[HALF:, :]) + bgu_ref[...]
        gate = jnp.minimum(hgu[:, :D_FF], SWIGLU_LIMIT)
        up = jnp.clip(hgu[:, D_FF:], -SWIGLU_LIMIT, SWIGLU_LIMIT)
        act = gate * jax.nn.sigmoid(SWIGLU_ALPHA * gate)
        y = _dot(((up + 1.0) * act).astype(BF16), wd_b_ref[...]) + bd_ref[...]
        y_ref[...] = _pack_pairs(y.astype(BF16).astype(F32))


def _experts(block_expert, n_used, buf, w_gu, b_gu, w_d, b_d):
    n_blocks = block_expert.shape[0]
    rows = n_blocks * MOE_BLOCK
    row_map = lambda i, be, nu: (jnp.minimum(i, nu[0] - 1), 0)
    w_map = lambda i, be, nu: (be[jnp.minimum(i, nu[0] - 1)], 0, 0)
    grid_spec = pltpu.PrefetchScalarGridSpec(
        num_scalar_prefetch=2,
        grid=(n_blocks,),
        in_specs=[
            pl.BlockSpec((MOE_BLOCK, HALF), row_map),
            pl.BlockSpec((None, D_MODEL, 2 * D_FF), w_map),
            pl.BlockSpec((None, 1, 2 * D_FF), w_map),
            pl.BlockSpec((None, D_FF, D_MODEL), w_map),
            pl.BlockSpec((None, 1, D_MODEL), w_map),
        ],
        out_specs=pl.BlockSpec((MOE_BLOCK, HALF), lambda i, be, nu: (i, 0)),
        scratch_shapes=[pltpu.VMEM((D_MODEL, 2 * D_FF), BF16), pltpu.VMEM((D_FF, D_MODEL), BF16)],
    )
    return pl.pallas_call(
        _expert_kernel,
        grid_spec=grid_spec,
        out_shape=jax.ShapeDtypeStruct((rows, HALF), jnp.uint32),
        compiler_params=pltpu.CompilerParams(
            dimension_semantics=("arbitrary",), vmem_limit_bytes=VMEM_LIMIT),
        name="experts",
    )(block_expert, n_used, buf, w_gu, b_gu, w_d, b_d)


def _combine_kernel(offs_ref, idx_ref, lrank_ref, gate_ref, h2_ref, fw_ref, dst_ref, dst_next_ref,
                    y_hbm_ref, out_ref, ys_ref, sem):
    i = pl.program_id(0)
    slot = i % 2
    base = i * N_EXPERTS
    tt = h2_ref.shape[0]

    def copies(table_ref, which):
        def make_copy(local, glob):
            return pltpu.make_async_copy(
                y_hbm_ref.at[pl.ds(glob, SEG_ALIGN), :],
                ys_ref.at[which, pl.ds(local, SEG_ALIGN), :], sem.at[which])
        return _group_copies(table_ref, make_copy)

    @pl.when(i == 0)
    def _():
        for cp in copies(dst_ref, slot):
            cp.start()

    @pl.when(i < pl.num_programs(0) - 1)
    def _():
        for cp in copies(dst_next_ref, 1 - slot):
            cp.start()

    pos = _sorted_positions(idx_ref[...], lrank_ref[...], offs_ref, base).astype(F32)
    zpad = jnp.zeros((128 - 8, tt), F32)
    pos_c = jnp.concatenate([pos, zpad], axis=0).T
    gate_c = jnp.concatenate([gate_ref[...], zpad], axis=0).T
    lane = lax.broadcasted_iota(jnp.int32, (tt, SORT_ROWS), 1).astype(F32)
    pg = jnp.zeros((tt, SORT_ROWS), F32)
    for kk in range(TOP_K):
        pg = jnp.where(lane == pos_c[:, kk:kk + 1], gate_c[:, kk:kk + 1], pg)
    p_hi, p_lo = _split2(pg)

    for cp in copies(dst_ref, slot):
        cp.wait()
    y_lo, y_hi = _unpack_pairs(ys_ref[slot])
    ffn = jnp.concatenate(
        [_dot(p_hi, y_lo) + _dot(p_lo, y_lo), _dot(p_hi, y_hi) + _dot(p_lo, y_hi)], axis=1)
    acc = h2_ref[...] + ffn
    out_ref[...] = acc * lax.rsqrt(jnp.mean(acc * acc, axis=-1, keepdims=True) + EPS) * fw_ref[...]


def _combine(offs, group_dst, top_idx, lrank, gates, h2, final_norm_w, y_buf):
    n = h2.shape[0]
    tt = TOKEN_TILE
    grid_spec = pltpu.PrefetchScalarGridSpec(
        num_scalar_prefetch=1,
        grid=(n // tt,),
        in_specs=[
            pl.BlockSpec((8, tt), lambda i, *_: (0, i)),
            pl.BlockSpec((8, tt), lambda i, *_: (0, i)),
            pl.BlockSpec((8, tt), lambda i, *_: (0, i)),
            pl.BlockSpec((tt, D_MODEL), lambda i, *_: (i, 0)),
            pl.BlockSpec((1, D_MODEL), lambda i, *_: (0, 0)),
            pl.BlockSpec((1, 1, SORT_GROUPS), lambda i, *_: (i, 0, 0), memory_space=pltpu.SMEM),
            pl.BlockSpec((1, 1, SORT_GROUPS), lambda i, *_: (jnp.minimum(i + 1, n // tt - 1), 0, 0),
                         memory_space=pltpu.SMEM),
            pl.BlockSpec(memory_space=pl.ANY),
        ],
        out_specs=pl.BlockSpec((tt, D_MODEL), lambda i, *_: (i, 0)),
        scratch_shapes=[pltpu.VMEM((2, SORT_ROWS, HALF), jnp.uint32), pltpu.SemaphoreType.DMA((2,))],
    )
    return pl.pallas_call(
        _combine_kernel,
        grid_spec=grid_spec,
        out_shape=jax.ShapeDtypeStruct((n, D_MODEL), F32),
        compiler_params=pltpu.CompilerParams(
            dimension_semantics=("arbitrary",), vmem_limit_bytes=VMEM_LIMIT),
        name="combine",
    )(offs, top_idx, lrank, gates, h2, final_norm_w, group_dst, group_dst, y_buf)


def _pad_lanes(v, width):
    return jnp.pad(v, ((0, 0), (0, width - v.shape[1])))


def kernel(x, meta_tokens, mix_norm_w, w_in, conv_w, conv_b, dt_bias, a_log, d_skip, ssd_norm_w,
           w_decay_up, b_decay, gla_norm_w, w_out, ffn_norm_w, w_router, b_router, w_gate_up,
           b_gate_up, w_down, b_down, final_norm_w):
    batch, seq, d = x.shape
    assert d == D_MODEL and seq % TOKEN_TILE == 0
    assert mix_norm_w.shape[0] == 1, "single-layer block"
    nchunks = (FRONT_PAD + N_META + seq) // CHUNK

    n = batch * seq
    x_flat = x.reshape(n, D_MODEL)
    meta_tile = jnp.concatenate(
        [jnp.zeros((TOKEN_TILE - N_META, D_MODEL), x.dtype), meta_tokens.astype(x.dtype)], axis=0)

    wi = w_in[0]
    o_z, o_xbc = 0, SSD_WIDTH
    o_dt = o_xbc + XBC_WIDTH
    o_q = o_dt + SSD_HEADS
    o_k = o_q + GLA_K_WIDTH
    o_v = o_k + GLA_K_WIDTH
    o_g = o_v + GLA_V_WIDTH
    o_a = o_g + GLA_V_WIDTH
    w_misc = jnp.concatenate(
        [wi[:, o_dt:o_dt + SSD_HEADS], wi[:, o_a:o_a + GLA_RANK],
         jnp.zeros((D_MODEL, MISC_WIDTH - SSD_HEADS - GLA_RANK), wi.dtype)], axis=1)
    w_in_r = jnp.concatenate(
        [wi[:, o_z:o_dt], wi[:, o_q:o_a], w_misc], axis=1).astype(BF16)
    z, xbc, q, k, v, g, misc = _in_proj(x_flat, meta_tile, mix_norm_w[0][None, :], w_in_r)

    dtb = _pad_lanes(dt_bias[0][None, :].astype(F32), MISC_WIDTH)
    aneg = _pad_lanes(-jnp.exp(a_log[0].astype(F32))[None, :], MISC_WIDTH)
    dskip = jnp.repeat(d_skip[0].astype(F32), SSD_HEAD_DIM)[None, :]
    wdec = jnp.zeros((MISC_WIDTH, GLA_K_WIDTH), F32).at[SSD_HEADS:SSD_HEADS + GLA_RANK].set(w_decay_up[0])
    params = (conv_w[0], conv_b[0][None, :], dtb, aneg, dskip, ssd_norm_w[0][None, :],
              wdec.astype(BF16), b_decay[0][None, :], gla_norm_w[0][None, :])
    mixed = _mixer(z, xbc, q, k, v, g, misc, params, batch, nchunks)

    wr_hi, wr_lo = _split2(w_router[0].T.astype(F32))
    n_tiles = n // TOKEN_TILE
    n_blocks = -(-(n * TOP_K + n_tiles * N_EXPERTS * (SEG_ALIGN - 1)) // MOE_BLOCK) + N_EXPERTS
    spare = n_blocks * MOE_BLOCK
    total_rows = spare + 2 * SORT_ROWS
    zero_rows = -(-total_rows // ((n_tiles + 1) * SEG_ALIGN)) * SEG_ALIGN
    h2, t_b, top_idx, gates, lrank, tile_cnt, buf0 = _post(
        x_flat, mixed, w_out[0].astype(BF16), ffn_norm_w[0][None, :],
        jnp.stack([wr_hi, wr_lo]), b_router[0][:, None], zero_rows)

    tile_cnt = tile_cnt.reshape(n_tiles, N_EXPERTS, 128)[:, :, 0]
    seg_rows = (tile_cnt + SEG_ALIGN - 1) // SEG_ALIGN * SEG_ALIGN
    counts = jnp.sum(seg_rows, axis=0)
    padded = (counts + MOE_BLOCK - 1) // MOE_BLOCK * MOE_BLOCK
    pend = jnp.cumsum(padded)
    pstart = pend - padded
    dstart = pstart[None, :] + jnp.cumsum(seg_rows, axis=0) - seg_rows
    seg_end = jnp.cumsum(seg_rows, axis=1)
    offs = seg_end - seg_rows
    block_pos = jnp.arange(total_rows // MOE_BLOCK, dtype=jnp.int32) * MOE_BLOCK
    block_expert = jnp.minimum(
        jnp.sum((pend[None, :] <= block_pos[:, None]).astype(jnp.int32), axis=1), N_EXPERTS - 1)
    n_used = (pend[-1:] // MOE_BLOCK).astype(jnp.int32)
    grow = jnp.arange(SORT_GROUPS, dtype=jnp.int32)[None, :, None] * SEG_ALIGN
    inside = (offs[:, None, :] <= grow) & (grow < seg_end[:, None, :])
    group_dst = jnp.sum(jnp.where(inside, dstart[:, None, :] + grow - offs[:, None, :], 0), axis=2)
    parity = (jnp.arange(n_tiles, dtype=jnp.int32) % 2)[:, None]
    group_dst = jnp.where(jnp.any(inside, axis=2), group_dst, spare + parity * SORT_ROWS + grow[:, :, 0])
    group_dst = group_dst.astype(jnp.int32)[:, None, :]
    offs = offs.reshape(-1).astype(jnp.int32)

    buf = _dispatch(offs, group_dst, top_idx, lrank, t_b, buf0)
    y_buf = _experts(block_expert, n_used, buf, w_gate_up[0], b_gate_up[0][:, None, :],
                     w_down[0], b_down[0][:, None, :])
    out = _combine(offs, group_dst, top_idx, lrank, gates, h2, final_norm_w[None, :], y_buf)
    return out.reshape(batch, seq, D_MODEL)
```

```python
import functools

import jax
import jax.numpy as jnp
from jax import lax
from jax.experimental import pallas as pl
from jax.experimental.pallas import tpu as pltpu

F32 = jnp.float32
BF16 = jnp.bfloat16

LANES = 128
SUBLANES = 8

D_MODEL = 1024
N_META = 16
EPS = 1e-5
SSD_HEAD_DIM = 64
SSD_HEADS = 16
SSD_GROUPS = 2
SSD_STATE = 128
SSD_CONV = 4
SSD_WIDTH = SSD_HEADS * SSD_HEAD_DIM
XBC_WIDTH = SSD_WIDTH + 2 * SSD_GROUPS * SSD_STATE
GLA_HEADS = 4
GLA_KEY_DIM = 128
GLA_VAL_DIM = 256
GLA_K_WIDTH = GLA_HEADS * GLA_KEY_DIM
GLA_V_WIDTH = GLA_HEADS * GLA_VAL_DIM
GLA_RANK = 16
GLA_TAU = 16.0
GLA_SUB = 16
N_EXPERTS = 32
TOP_K = 4
D_FF = D_MODEL
SWIGLU_LIMIT = 7.0
SWIGLU_ALPHA = 1.702
MOE_BLOCK = 512

CHUNK = 128
FRONT_PAD = CHUNK - N_META
MISC_WIDTH = LANES
IN_PAD_WIDTH = SSD_WIDTH + XBC_WIDTH + 2 * GLA_K_WIDTH + 2 * GLA_V_WIDTH + MISC_WIDTH
TOKEN_TILE = 256
HALF = D_MODEL // 2
NO_ROW = 1 << 12
SEG_ALIGN = SUBLANES
PICK_ROWS = SUBLANES
CONV_CARRY = SUBLANES
SORT_ROWS = -(-(TOP_K * TOKEN_TILE + N_EXPERTS * (SEG_ALIGN - 1)) // TOKEN_TILE) * TOKEN_TILE
SORT_GROUPS = SORT_ROWS // SEG_ALIGN
VMEM_LIMIT = 56 * 1024 * 1024


def _split2(x):
    hi = x.astype(BF16)
    lo = (x - hi.astype(F32)).astype(BF16)
    return hi, lo


def _split3(x):
    hi = x.astype(BF16)
    r = x - hi.astype(F32)
    mid = r.astype(BF16)
    lo = (r - mid.astype(F32)).astype(BF16)
    return hi, mid, lo


def _dot(a, b):
    return jnp.dot(a, b, preferred_element_type=F32)


def _dot_nt(a, b):
    return lax.dot_general(a, b, (((1,), (1,)), ((), ())), preferred_element_type=F32)


def _dot_tn(a, b):
    return lax.dot_general(a, b, (((0,), (0,)), ((), ())), preferred_element_type=F32)


def _silu(x):
    return x * jax.nn.sigmoid(x)


_IN_SECTIONS = (SSD_WIDTH, XBC_WIDTH, GLA_K_WIDTH, GLA_K_WIDTH, GLA_V_WIDTH, GLA_V_WIDTH, MISC_WIDTH)
_GATE_SECTIONS = (0, 5)


def _in_proj_kernel(x_ref, meta_ref, nw_ref, w_ref, z_ref, xbc_ref, q_ref, k_ref, v_ref, g_ref, misc_ref):
    h = jnp.where(pl.program_id(0) < pl.num_programs(0) - 1, x_ref[...], meta_ref[...])
    u = h * lax.rsqrt(jnp.mean(h * h, axis=-1, keepdims=True) + EPS) * nw_ref[...]
    ub = u.astype(BF16)
    outs = (z_ref, xbc_ref, q_ref, k_ref, v_ref, g_ref, misc_ref)
    off = 0
    for section, (o_ref, width) in enumerate(zip(outs, _IN_SECTIONS)):
        p = _dot(ub, w_ref[:, off:off + width])
        if section in _GATE_SECTIONS:
            p = _silu(p)
        o_ref[...] = p.astype(o_ref.dtype)
        off += width


def _in_proj(x_flat, meta_tile, norm_w, w_in_r):
    tm = TOKEN_TILE
    nx = x_flat.shape[0] // tm
    rows = (nx + 1) * tm
    row_spec = lambda w: pl.BlockSpec((tm, w), lambda i: (i, 0))
    out_dtypes = (BF16,) * 6 + (F32,)
    return pl.pallas_call(
        _in_proj_kernel,
        grid=(nx + 1,),
        in_specs=[
            pl.BlockSpec((tm, D_MODEL), lambda i: (jnp.minimum(i, nx - 1), 0)),
            pl.BlockSpec((tm, D_MODEL), lambda i: (0, 0)),
            pl.BlockSpec((1, D_MODEL), lambda i: (0, 0)),
            pl.BlockSpec((D_MODEL, IN_PAD_WIDTH), lambda i: (0, 0)),
        ],
        out_specs=[row_spec(w) for w in _IN_SECTIONS],
        out_shape=[jax.ShapeDtypeStruct((rows, w), dt) for w, dt in zip(_IN_SECTIONS, out_dtypes)],
        compiler_params=pltpu.CompilerParams(
            dimension_semantics=("arbitrary",), vmem_limit_bytes=VMEM_LIMIT),
        name="in_proj",
    )(x_flat, meta_tile, norm_w, w_in_r)


def _block_rows(x, rows, width):
    parts = []
    for r in rows:
        if r is None:
            parts.append(jnp.zeros((width, x.shape[1]), x.dtype))
        else:
            parts.append(jnp.broadcast_to(x[r:r + 1, :], (width, x.shape[1])))
    return jnp.concatenate(parts, axis=0)


N_SEQ_INPUTS = 7
N_MIXER_PARAMS = 9


def _mixer_kernel(*refs, nseq):
    seq_refs = [refs[s * N_SEQ_INPUTS:(s + 1) * N_SEQ_INPUTS] for s in range(nseq)]
    rest = refs[nseq * N_SEQ_INPUTS:]
    param_refs = rest[:N_MIXER_PARAMS]
    out_ref, xext_ref, sstate_ref, gstate_ref = rest[N_MIXER_PARAMS:]
    c = pl.program_id(1)

    @pl.when(c == 0)
    def _():
        xext_ref[:, 0:CONV_CARRY, :] = jnp.zeros((nseq, CONV_CARRY, XBC_WIDTH), BF16)
        sstate_ref[...] = jnp.zeros_like(sstate_ref)
        gstate_ref[...] = jnp.zeros_like(gstate_ref)

    for s in range(nseq):
        _mixer_chunk(c, *seq_refs[s], *param_refs, out_ref.at[s], xext_ref.at[s], sstate_ref.at[s],
                     gstate_ref.at[s])


def _mixer_chunk(c, z_ref, xbc_ref, q_ref, k_ref, v_ref, g_ref, misc_ref,
                 convw_ref, convb_ref, dtb_ref, aneg_ref, dskip_ref, ssdw_ref,
                 wdec_ref, bdec_ref, glaw_ref,
                 out_ref,
                 xext_ref, sstate_ref, gstate_ref):

    row = lax.broadcasted_iota(jnp.int32, (CHUNK, CHUNK), 0)
    col = lax.broadcasted_iota(jnp.int32, (CHUNK, CHUNK), 1)
    causal = row >= col
    tri = jnp.where(causal, 1.0, 0.0).astype(BF16)
    valid = jnp.logical_or(c > 0, row >= FRONT_PAD)

    xcur = xbc_ref[...]
    xext_ref[CONV_CARRY:CONV_CARRY + CHUNK, :] = xcur
    xall = xext_ref[...]
    srow = lax.broadcasted_iota(jnp.int32, (CHUNK, CHUNK + CONV_CARRY), 0)
    scol = lax.broadcasted_iota(jnp.int32, (CHUNK, CHUNK + CONV_CARRY), 1)
    conv = convb_ref[...] + convw_ref[SSD_CONV - 1:SSD_CONV, :] * xcur.astype(F32)
    for kk in range(SSD_CONV - 1):
        shift = jnp.where(scol == srow + CONV_CARRY - (SSD_CONV - 1) + kk, 1.0, 0.0).astype(BF16)
        conv = conv + convw_ref[kk:kk + 1, :] * _dot(shift, xall)
    xext_ref[0:CONV_CARRY, :] = xext_ref[CHUNK:CHUNK + CONV_CARRY, :]
    valid_w = jnp.logical_or(c > 0, lax.broadcasted_iota(jnp.int32, (CHUNK, XBC_WIDTH), 0) >= FRONT_PAD)
    act = jnp.where(valid_w, _silu(conv), 0.0)
    xs = act[:, :SSD_WIDTH]
    bmat = act[:, SSD_WIDTH:SSD_WIDTH + SSD_GROUPS * SSD_STATE].astype(BF16)
    cmat = act[:, SSD_WIDTH + SSD_GROUPS * SSD_STATE:].astype(BF16)

    misc = misc_ref[...]
    dt = jnp.where(valid, jax.nn.softplus(misc + dtb_ref[...]), 0.0)
    d_a = dt * aneg_ref[...]
    p0, p1, p2 = _split3(d_a)
    a_cs = _dot(tri, p0) + _dot(tri, p1) + _dot(tri, p2)
    a_cs_t = a_cs.T

    lo_half = col < SSD_HEAD_DIM
    hg = SSD_HEADS // SSD_GROUPS
    pairs_per_group = hg // 2
    y_parts = []
    for gi in range(SSD_GROUPS):
        b_g = bmat[:, gi * SSD_STATE:(gi + 1) * SSD_STATE]
        c_g = cmat[:, gi * SSD_STATE:(gi + 1) * SSD_STATE]
        c_g32 = c_g.astype(F32)
        cb = _dot_nt(c_g, b_g)
        w_parts = []
        cd_parts = []
        for pj in range(pairs_per_group):
            h0 = gi * hg + 2 * pj
            h1 = h0 + 1
            lanes = slice(h0 * SSD_HEAD_DIM, (h1 + 1) * SSD_HEAD_DIM)
            ab0 = jnp.broadcast_to(a_cs[:, h0:h0 + 1], (CHUNK, CHUNK))
            ab1 = jnp.broadcast_to(a_cs[:, h1:h1 + 1], (CHUNK, CHUNK))
            db0 = jnp.broadcast_to(dt[:, h0:h0 + 1], (CHUNK, CHUNK))
            db1 = jnp.broadcast_to(dt[:, h1:h1 + 1], (CHUNK, CHUNK))
            l0 = jnp.exp(jnp.where(causal, ab0 - a_cs_t[h0:h0 + 1, :], -jnp.inf))
            l1 = jnp.exp(jnp.where(causal, ab1 - a_cs_t[h1:h1 + 1, :], -jnp.inf))
            lhs = jnp.concatenate(
                [(cb * l0).astype(BF16), (c_g32 * jnp.exp(ab0)).astype(BF16),
                 (cb * l1).astype(BF16), (c_g32 * jnp.exp(ab1)).astype(BF16)], axis=1)
            xs_p = xs[:, lanes]
            dtx = xs_p * jnp.where(lo_half, db0, db1)
            st = sstate_ref[gi, :, pj * CHUNK:(pj + 1) * CHUNK]
            rhs = jnp.concatenate(
                [jnp.where(lo_half, dtx, 0.0).astype(BF16), jnp.where(lo_half, st, 0.0).astype(BF16),
                 jnp.where(lo_half, 0.0, dtx).astype(BF16), jnp.where(lo_half, 0.0, st).astype(BF16)],
                axis=0)
            y_p = _dot(lhs, rhs) + dskip_ref[:, lanes] * xs_p
            y_parts.append(y_p)
            ae = jnp.where(lo_half, ab0, ab1)
            a_last = ae[CHUNK - 1:CHUNK, :]
            w_parts.append((dtx * jnp.exp(a_last - ae)).astype(BF16))
            cd_parts.append(jnp.exp(a_last))
        upd = _dot_tn(b_g, jnp.concatenate(w_parts, axis=1))
        sstate_ref[gi] = jnp.concatenate(cd_parts, axis=1) * sstate_ref[gi] + upd
    y = jnp.concatenate(y_parts, axis=1)
    y = y * z_ref[...].astype(F32)
    gsz = SSD_WIDTH // SSD_GROUPS
    for gi in range(SSD_GROUPS):
        yg = y[:, gi * gsz:(gi + 1) * gsz]
        yn = yg * lax.rsqrt(jnp.mean(yg * yg, axis=-1, keepdims=True) + EPS)
        out_ref[:, gi * gsz:(gi + 1) * gsz] = (yn * ssdw_ref[:, gi * gsz:(gi + 1) * gsz]).astype(out_ref.dtype)

    m0, m1 = _split2(misc)
    pre = _dot(m0, wdec_ref[...]) + _dot(m1, wdec_ref[...]) + bdec_ref[...]
    log_a = jax.nn.log_sigmoid(pre) * (1.0 / GLA_TAU)
    g0, g1 = _split2(log_a)
    gcum = _dot(tri, g0) + _dot(tri, g1)
    roww = lax.broadcasted_iota(jnp.int32, (CHUNK, GLA_K_WIDTH), 0)
    qf = q_ref[...].astype(F32) * (GLA_KEY_DIM ** -0.5)
    kf = k_ref[...].astype(F32)
    nsub = CHUNK // GLA_SUB
    g_start = _block_rows(gcum, [None] + [GLA_SUB * i - 1 for i in range(1, nsub)], GLA_SUB)
    e0 = gcum - g_start
    q_lv = [(qf * jnp.exp(e0)).astype(BF16)]
    k_lv = [(kf * jnp.exp(-e0)).astype(BF16)]
    shifts = (4, 5, 6)
    for sh in shifts:
        b = 1 << sh
        bound = _block_rows(gcum, [2 * b * i + b - 1 for i in range(CHUNK // (2 * b))], 2 * b)
        second = ((roww >> sh) & 1) == 1
        x_l = jnp.exp(jnp.where(second, gcum - bound, bound - gcum))
        q_lv.append((qf * x_l).astype(BF16))
        k_lv.append((kf * x_l).astype(BF16))
    masks = [jnp.logical_and((row >> 4) == (col >> 4), causal)]
    for sh in shifts:
        same = (row >> (sh + 1)) == (col >> (sh + 1))
        m = jnp.logical_and(same, jnp.logical_and(((row >> sh) & 1) == 1, ((col >> sh) & 1) == 0))
        masks.append(m)
    g_last = gcum[CHUNK - 1:CHUNK, :]
    q_in = (qf * jnp.exp(gcum)).astype(BF16)
    k_end = (kf * jnp.exp(g_last - gcum)).astype(BF16)
    dec = jnp.exp(g_last)
    vb = v_ref[...]
    for hh in range(GLA_HEADS):
        kl = slice(hh * GLA_KEY_DIM, (hh + 1) * GLA_KEY_DIM)
        vl = slice(hh * GLA_VAL_DIM, (hh + 1) * GLA_VAL_DIM)
        scores = jnp.zeros((CHUNK, CHUNK), F32)
        for lv in range(len(masks)):
            scores = scores + jnp.where(masks[lv], _dot_nt(q_lv[lv][:, kl], k_lv[lv][:, kl]), 0.0)
        v_h = vb[:, vl]
        s_t = gstate_ref[hh]
        o = _dot(scores.astype(BF16), v_h) + _dot_nt(q_in[:, kl], s_t.astype(BF16))
        gstate_ref[hh] = dec[:, kl] * s_t + _dot_tn(v_h, k_end[:, kl])
        o = o * lax.rsqrt(jnp.mean(o * o, axis=-1, keepdims=True) + EPS) * glaw_ref[...]
        o = o * g_ref[:, vl].astype(F32)
        out_ref[:, SSD_WIDTH + hh * GLA_VAL_DIM:SSD_WIDTH + (hh + 1) * GLA_VAL_DIM] = o.astype(out_ref.dtype)


def _mixer(z, xbc, q, k, v, g, misc, params, batch, nchunks):
    widths = (SSD_WIDTH, XBC_WIDTH, GLA_K_WIDTH, GLA_K_WIDTH, GLA_V_WIDTH, GLA_V_WIDTH, MISC_WIDTH)
    assert len(params) == N_MIXER_PARAMS
    nseq = 4 if batch % 4 == 0 else (2 if batch % 2 == 0 else 1)
    meta_block = z.shape[0] // CHUNK - 1

    def row_spec(w, s):
        return pl.BlockSpec(
            (CHUNK, w),
            lambda bp, c: (jnp.where(c == 0, meta_block, (bp * nseq + s) * (nchunks - 1) + c - 1), 0))

    par_spec = lambda p: pl.BlockSpec(p.shape, lambda bp, c: (0,) * p.ndim)
    seq = (nchunks - 1) * CHUNK
    width = SSD_WIDTH + GLA_V_WIDTH
    out = pl.pallas_call(
        functools.partial(_mixer_kernel, nseq=nseq),
        grid=(batch // nseq, nchunks),
        in_specs=[row_spec(w, s) for s in range(nseq) for w in widths] + [par_spec(p) for p in params],
        out_specs=pl.BlockSpec((nseq, CHUNK, width), lambda bp, c: (bp, jnp.maximum(c - 1, 0), 0)),
        out_shape=jax.ShapeDtypeStruct((batch, seq, width), BF16),
        scratch_shapes=[
            pltpu.VMEM((nseq, CHUNK + CONV_CARRY, XBC_WIDTH), BF16),
            pltpu.VMEM((nseq, SSD_GROUPS, SSD_STATE, SSD_WIDTH // SSD_GROUPS), F32),
            pltpu.VMEM((nseq, GLA_HEADS, GLA_VAL_DIM, GLA_KEY_DIM), F32),
        ],
        compiler_params=pltpu.CompilerParams(
            dimension_semantics=("arbitrary", "arbitrary"), vmem_limit_bytes=VMEM_LIMIT),
        name="mixer",
    )(*([z, xbc, q, k, v, g, misc] * nseq), *params)
    return out.reshape(batch * seq, width)


def _post_kernel(x_ref, mixed_ref, wout_ref, fnw_ref, wrt_ref, br_ref,
                 h2_ref, t_ref, idx_ref, gate_ref, lrank_ref, cnt_ref, zero_ref, tprev_ref):
    @pl.when(pl.program_id(0) == 0)
    def _():
        tprev_ref[...] = jnp.zeros_like(tprev_ref)

    t_prev = tprev_ref[...]
    h2 = x_ref[...] + _dot(mixed_ref[...], wout_ref[...])
    h2_ref[...] = h2
    t = h2 * lax.rsqrt(jnp.mean(h2 * h2, axis=-1, keepdims=True) + EPS) * fnw_ref[...]
    t_ref[...] = t.astype(BF16)
    tprev_ref[...] = t

    t0, t1 = _split2(t_prev)
    w0 = wrt_ref[0]
    w1 = wrt_ref[1]
    logits = _dot_nt(w0, t0) + _dot_nt(w0, t1) + _dot_nt(w1, t0) + br_ref[...]
    e_iota = lax.broadcasted_iota(jnp.int32, logits.shape, 0)
    work = logits
    sel_any = jnp.zeros(logits.shape, F32)
    tops = []
    idxs = []
    onehots = []
    for _ in range(TOP_K):
        m = jnp.max(work, axis=0, keepdims=True)
        idx = jnp.min(jnp.where(work == m, e_iota, N_EXPERTS), axis=0, keepdims=True)
        hit = e_iota == idx
        tops.append(m)
        idxs.append(idx)
        onehots.append(hit)
        sel_any = sel_any + jnp.where(hit, 1.0, 0.0)
        work = jnp.where(hit, -jnp.inf, work)
    exps = [jnp.exp(tv - tops[0]) for tv in tops]
    denom = exps[0] + exps[1] + exps[2] + exps[3]
    tile = logits.shape[1]
    pad_rows = PICK_ROWS - TOP_K
    gate_ref[...] = jnp.concatenate([e / denom for e in exps] + [jnp.zeros((pad_rows, tile), F32)], axis=0)
    idx_ref[...] = jnp.concatenate(idxs + [jnp.zeros((pad_rows, tile), jnp.int32)], axis=0)

    r_i = lax.broadcasted_iota(jnp.int32, (tile, tile), 0)
    c_i = lax.broadcasted_iota(jnp.int32, (tile, tile), 1)
    upper = jnp.where(r_i < c_i, 1.0, 0.0).astype(BF16)
    before = _dot(sel_any.astype(BF16), upper)
    ranks = [jnp.sum(jnp.where(h, before, 0.0), axis=0, keepdims=True) for h in onehots]
    ranks.append(jnp.full((pad_rows, tile), float(NO_ROW), F32))
    lrank_ref[...] = jnp.concatenate(ranks, axis=0).astype(jnp.int32)
    cnt = jnp.sum(sel_any, axis=1, keepdims=True)
    cnt_ref[...] = jnp.broadcast_to(cnt, cnt_ref.shape).astype(jnp.int32)
    zero_ref[...] = jnp.zeros_like(zero_ref)


def _post(x_flat, mixed, w_out_b, ffn_norm_w, w_router_t, b_router_col, zero_rows):
    n = x_flat.shape[0]
    tt = TOKEN_TILE
    n_tiles = n // tt
    const = lambda shape: pl.BlockSpec(shape, lambda i: (0,) * len(shape))
    cur = lambda i: jnp.minimum(i, n_tiles - 1)
    prev = lambda i: jnp.maximum(i - 1, 0)
    return pl.pallas_call(
        _post_kernel,
        grid=(n_tiles + 1,),
        in_specs=[
            pl.BlockSpec((tt, D_MODEL), lambda i: (cur(i), 0)),
            pl.BlockSpec((tt, SSD_WIDTH + GLA_V_WIDTH), lambda i: (cur(i), 0)),
            const(w_out_b.shape),
            const((1, D_MODEL)),
            const(w_router_t.shape),
            const((N_EXPERTS, 1)),
        ],
        out_specs=[
            pl.BlockSpec((tt, D_MODEL), lambda i: (cur(i), 0)),
            pl.BlockSpec((tt, D_MODEL), lambda i: (cur(i), 0)),
            pl.BlockSpec((PICK_ROWS, tt), lambda i: (0, prev(i))),
            pl.BlockSpec((PICK_ROWS, tt), lambda i: (0, prev(i))),
            pl.BlockSpec((PICK_ROWS, tt), lambda i: (0, prev(i))),
            pl.BlockSpec((N_EXPERTS, LANES), lambda i: (prev(i), 0)),
            pl.BlockSpec((zero_rows, HALF), lambda i: (i, 0)),
        ],
        out_shape=[
            jax.ShapeDtypeStruct((n, D_MODEL), F32),
            jax.ShapeDtypeStruct((n, D_MODEL), BF16),
            jax.ShapeDtypeStruct((PICK_ROWS, n), jnp.int32),
            jax.ShapeDtypeStruct((PICK_ROWS, n), F32),
            jax.ShapeDtypeStruct((PICK_ROWS, n), jnp.int32),
            jax.ShapeDtypeStruct((n_tiles * N_EXPERTS, LANES), jnp.int32),
            jax.ShapeDtypeStruct(((n_tiles + 1) * zero_rows, HALF), jnp.uint32),
        ],
        scratch_shapes=[pltpu.VMEM((tt, D_MODEL), F32)],
        compiler_params=pltpu.CompilerParams(
            dimension_semantics=("arbitrary",), vmem_limit_bytes=VMEM_LIMIT),
        name="post",
    )(x_flat, mixed, w_out_b, ffn_norm_w, w_router_t, b_router_col)


def _pack_pairs(x):
    lo = pltpu.bitcast(x[:, :HALF], jnp.uint32) >> 16
    hi = pltpu.bitcast(x[:, HALF:], jnp.uint32) & jnp.uint32(0xFFFF0000)
    return lo | hi


def _unpack_pairs(w):
    lo = pltpu.bitcast(w << 16, F32).astype(BF16)
    hi = pltpu.bitcast(w & jnp.uint32(0xFFFF0000), F32).astype(BF16)
    return lo, hi


def _sorted_positions(idx, lrank, offs_ref, base):
    pos = lrank
    for e in range(N_EXPERTS):
        pos = pos + jnp.where(idx == e, offs_ref[base + e], 0)
    return pos


def _group_copies(dst_ref, make_copy):
    return [make_copy(g * SEG_ALIGN, pl.multiple_of(dst_ref[0, 0, g], SEG_ALIGN))
            for g in range(SORT_GROUPS)]


def _dispatch_kernel(offs_ref, idx_ref, lrank_ref, t_ref, dst_ref, dst_prev_ref, buf_in_ref, buf_ref,
                     sorted_ref, sem):
    del buf_in_ref
    i = pl.program_id(0)
    slot = i % 2
    base = i * N_EXPERTS
    tt = t_ref.shape[0]
    pos = _sorted_positions(idx_ref[...], lrank_ref[...], offs_ref, base)
    t = t_ref[...]
    for r0 in range(0, SORT_ROWS, tt):
        prow = lax.broadcasted_iota(jnp.int32, (tt, tt), 0) + r0
        perm = jnp.zeros((tt, tt), F32)
        for kk in range(TOP_K):
            perm = jnp.where(prow == pos[kk:kk + 1, :], 1.0, perm)
        sorted_ref[slot, r0:r0 + tt, :] = _pack_pairs(_dot(perm.astype(BF16), t))

    def copies(table_ref, which):
        def make_copy(local, glob):
            return pltpu.make_async_copy(
                sorted_ref.at[which, pl.ds(local, SEG_ALIGN), :],
                buf_ref.at[pl.ds(glob, SEG_ALIGN), :], sem.at[which])
        return _group_copies(table_ref, make_copy)

    for cp in copies(dst_ref, slot):
        cp.start()

    @pl.when(i > 0)
    def _():
        for cp in copies(dst_prev_ref, 1 - slot):
            cp.wait()

    @pl.when(i == pl.num_programs(0) - 1)
    def _():
        for cp in copies(dst_ref, slot):
            cp.wait()


def _dispatch(offs, group_dst, top_idx, lrank, t_b, buf0):
    n = t_b.shape[0]
    tt = TOKEN_TILE
    table = lambda index: pl.BlockSpec((1, 1, SORT_GROUPS), index, memory_space=pltpu.SMEM)
    grid_spec = pltpu.PrefetchScalarGridSpec(
        num_scalar_prefetch=1,
        grid=(n // tt,),
        in_specs=[
            pl.BlockSpec((PICK_ROWS, tt), lambda i, *_: (0, i)),
            pl.BlockSpec((PICK_ROWS, tt), lambda i, *_: (0, i)),
            pl.BlockSpec((tt, D_MODEL), lambda i, *_: (i, 0)),
            table(lambda i, *_: (i, 0, 0)),
            table(lambda i, *_: (jnp.maximum(i - 1, 0), 0, 0)),
            pl.BlockSpec(memory_space=pl.ANY),
        ],
        out_specs=pl.BlockSpec(memory_space=pl.ANY),
        scratch_shapes=[pltpu.VMEM((2, SORT_ROWS, HALF), jnp.uint32), pltpu.SemaphoreType.DMA((2,))],
    )
    return pl.pallas_call(
        _dispatch_kernel,
        grid_spec=grid_spec,
        out_shape=jax.ShapeDtypeStruct(buf0.shape, jnp.uint32),
        input_output_aliases={6: 0},
        compiler_params=pltpu.CompilerParams(
            dimension_semantics=("arbitrary",), vmem_limit_bytes=VMEM_LIMIT),
        name="dispatch",
    )(offs, top_idx, lrank, t_b, group_dst, group_dst, buf0)


def _expert_kernel(be_ref, nused_ref, xp_ref, wgu_ref, bgu_ref, wd_ref, bd_ref, y_ref,
                   wgu_b_ref, wd_b_ref):
    i = pl.program_id(0)
    used = i < nused_ref[0]

    @pl.when(jnp.logical_not(used))
    def _():
        y_ref[...] = jnp.zeros_like(y_ref)

    new_expert = jnp.logical_or(i == 0, be_ref[i] != be_ref[jnp.maximum(i - 1, 0)])

    @pl.when(jnp.logical_and(used, new_expert))
    def _():
        rows = CHUNK
        for r0 in range(0, D_MODEL, rows):
            wgu_b_ref[r0:r0 + rows, :] = wgu_ref[r0:r0 + rows, :].astype(BF16)
        for r0 in range(0, D_FF, rows):
            wd_b_ref[r0:r0 + rows, :] = wd_ref[r0:r0 + rows, :].astype(BF16)

    @pl.when(used)
    def _():
        x_lo, x_hi = _unpack_pairs(xp_ref[...])
        hgu = _dot(x_lo, wgu_b_ref[:HALF, :]) + _dot(x_hi, wgu_b_ref[HALF:, :]) + bgu_ref[...]
        gate = jnp.minimum(hgu[:, :D_FF], SWIGLU_LIMIT)
        up = jnp.clip(hgu[:, D_FF:], -SWIGLU_LIMIT, SWIGLU_LIMIT)
        act = gate * jax.nn.sigmoid(SWIGLU_ALPHA * gate)
        y = _dot(((up + 1.0) * act).astype(BF16), wd_b_ref[...]) + bd_ref[...]
        y_ref[...] = _pack_pairs(y.astype(BF16).astype(F32))


def _experts(block_expert, n_used, buf, w_gu, b_gu, w_d, b_d):
    n_blocks = block_expert.shape[0]
    rows = n_blocks * MOE_BLOCK
    row_map = lambda i, be, nu: (jnp.minimum(i, nu[0] - 1), 0)
    w_map = lambda i, be, nu: (be[jnp.minimum(i, nu[0] - 1)], 0, 0)
    grid_spec = pltpu.PrefetchScalarGridSpec(
        num_scalar_prefetch=2,
        grid=(n_blocks,),
        in_specs=[
            pl.BlockSpec((MOE_BLOCK, HALF), row_map),
            pl.BlockSpec((None, D_MODEL, 2 * D_FF), w_map),
            pl.BlockSpec((None, 1, 2 * D_FF), w_map),
            pl.BlockSpec((None, D_FF, D_MODEL), w_map),
            pl.BlockSpec((None, 1, D_MODEL), w_map),
        ],
        out_specs=pl.BlockSpec((MOE_BLOCK, HALF), lambda i, be, nu: (i, 0)),
        scratch_shapes=[pltpu.VMEM((D_MODEL, 2 * D_FF), BF16), pltpu.VMEM((D_FF, D_MODEL), BF16)],
    )
    return pl.pallas_call(
        _expert_kernel,
        grid_spec=grid_spec,
        out_shape=jax.ShapeDtypeStruct((rows, HALF), jnp.uint32),
        compiler_params=pltpu.CompilerParams(
            dimension_semantics=("arbitrary",), vmem_limit_bytes=VMEM_LIMIT),
        name="experts",
    )(block_expert, n_used, buf, w_gu, b_gu, w_d, b_d)


def _combine_kernel(offs_ref, idx_ref, lrank_ref, gate_ref, h2_ref, fw_ref, dst_ref, dst_next_ref,
                    y_hbm_ref, out_ref, ys_ref, sem):
    i = pl.program_id(0)
    slot = i % 2
    base = i * N_EXPERTS
    tt = h2_ref.shape[0]

    def copies(table_ref, which):
        def make_copy(local, glob):
            return pltpu.make_async_copy(
                y_hbm_ref.at[pl.ds(glob, SEG_ALIGN), :],
                ys_ref.at[which, pl.ds(local, SEG_ALIGN), :], sem.at[which])
        return _group_copies(table_ref, make_copy)

    @pl.when(i == 0)
    def _():
        for cp in copies(dst_ref, slot):
            cp.start()

    @pl.when(i < pl.num_programs(0) - 1)
    def _():
        for cp in copies(dst_next_ref, 1 - slot):
            cp.start()

    pos = _sorted_positions(idx_ref[...], lrank_ref[...], offs_ref, base).astype(F32)
    zpad = jnp.zeros((LANES - PICK_ROWS, tt), F32)
    pos_c = jnp.concatenate([pos, zpad], axis=0).T
    gate_c = jnp.concatenate([gate_ref[...], zpad], axis=0).T
    lane = lax.broadcasted_iota(jnp.int32, (tt, SORT_ROWS), 1).astype(F32)
    pg = jnp.zeros((tt, SORT_ROWS), F32)
    for kk in range(TOP_K):
        pg = jnp.where(lane == pos_c[:, kk:kk + 1], gate_c[:, kk:kk + 1], pg)
    p_hi, p_lo = _split2(pg)

    for cp in copies(dst_ref, slot):
        cp.wait()
    y_lo, y_hi = _unpack_pairs(ys_ref[slot])
    ffn = jnp.concatenate(
        [_dot(p_hi, y_lo) + _dot(p_lo, y_lo), _dot(p_hi, y_hi) + _dot(p_lo, y_hi)], axis=1)
    acc = h2_ref[...] + ffn
    out_ref[...] = acc * lax.rsqrt(jnp.mean(acc * acc, axis=-1, keepdims=True) + EPS) * fw_ref[...]


def _combine(offs, group_dst, top_idx, lrank, gates, h2, final_norm_w, y_buf):
    n = h2.shape[0]
    tt = TOKEN_TILE
    grid_spec = pltpu.PrefetchScalarGridSpec(
        num_scalar_prefetch=1,
        grid=(n // tt,),
        in_specs=[
            pl.BlockSpec((PICK_ROWS, tt), lambda i, *_: (0, i)),
            pl.BlockSpec((PICK_ROWS, tt), lambda i, *_: (0, i)),
            pl.BlockSpec((PICK_ROWS, tt), lambda i, *_: (0, i)),
            pl.BlockSpec((tt, D_MODEL), lambda i, *_: (i, 0)),
            pl.BlockSpec((1, D_MODEL), lambda i, *_: (0, 0)),
            pl.BlockSpec((1, 1, SORT_GROUPS), lambda i, *_: (i, 0, 0), memory_space=pltpu.SMEM),
            pl.BlockSpec((1, 1, SORT_GROUPS), lambda i, *_: (jnp.minimum(i + 1, n // tt - 1), 0, 0),
                         memory_space=pltpu.SMEM),
            pl.BlockSpec(memory_space=pl.ANY),
        ],
        out_specs=pl.BlockSpec((tt, D_MODEL), lambda i, *_: (i, 0)),
        scratch_shapes=[pltpu.VMEM((2, SORT_ROWS, HALF), jnp.uint32), pltpu.SemaphoreType.DMA((2,))],
    )
    return pl.pallas_call(
        _combine_kernel,
        grid_spec=grid_spec,
        out_shape=jax.ShapeDtypeStruct((n, D_MODEL), F32),
        compiler_params=pltpu.CompilerParams(
            dimension_semantics=("arbitrary",), vmem_limit_bytes=VMEM_LIMIT),
        name="combine",
    )(offs, top_idx, lrank, gates, h2, final_norm_w, group_dst, group_dst, y_buf)


def _pad_lanes(v, width):
    return jnp.pad(v, ((0, 0), (0, width - v.shape[1])))


def kernel(x, meta_tokens, mix_norm_w, w_in, conv_w, conv_b, dt_bias, a_log, d_skip, ssd_norm_w,
           w_decay_up, b_decay, gla_norm_w, w_out, ffn_norm_w, w_router, b_router, w_gate_up,
           b_gate_up, w_down, b_down, final_norm_w):
    batch, seq, d = x.shape
    assert d == D_MODEL and seq % TOKEN_TILE == 0
    assert mix_norm_w.shape[0] == 1, "single-layer block"
    nchunks = (FRONT_PAD + N_META + seq) // CHUNK

    n = batch * seq
    x_flat = x.reshape(n, D_MODEL)
    meta_tile = jnp.concatenate(
        [jnp.zeros((TOKEN_TILE - N_META, D_MODEL), x.dtype), meta_tokens.astype(x.dtype)], axis=0)

    wi = w_in[0]
    o_z, o_xbc = 0, SSD_WIDTH
    o_dt = o_xbc + XBC_WIDTH
    o_q = o_dt + SSD_HEADS
    o_k = o_q + GLA_K_WIDTH
    o_v = o_k + GLA_K_WIDTH
    o_g = o_v + GLA_V_WIDTH
    o_a = o_g + GLA_V_WIDTH
    w_misc = jnp.concatenate(
        [wi[:, o_dt:o_dt + SSD_HEADS], wi[:, o_a:o_a + GLA_RANK],
         jnp.zeros((D_MODEL, MISC_WIDTH - SSD_HEADS - GLA_RANK), wi.dtype)], axis=1)
    w_in_r = jnp.concatenate(
        [wi[:, o_z:o_dt], wi[:, o_q:o_a], w_misc], axis=1).astype(BF16)
    z, xbc, q, k, v, g, misc = _in_proj(x_flat, meta_tile, mix_norm_w[0][None, :], w_in_r)

    dtb = _pad_lanes(dt_bias[0][None, :].astype(F32), MISC_WIDTH)
    aneg = _pad_lanes(-jnp.exp(a_log[0].astype(F32))[None, :], MISC_WIDTH)
    dskip = jnp.repeat(d_skip[0].astype(F32), SSD_HEAD_DIM)[None, :]
    wdec = jnp.zeros((MISC_WIDTH, GLA_K_WIDTH), F32).at[SSD_HEADS:SSD_HEADS + GLA_RANK].set(w_decay_up[0])
    params = (conv_w[0], conv_b[0][None, :], dtb, aneg, dskip, ssd_norm_w[0][None, :],
              wdec.astype(BF16), b_decay[0][None, :], gla_norm_w[0][None, :])
    mixed = _mixer(z, xbc, q, k, v, g, misc, params, batch, nchunks)

    wr_hi, wr_lo = _split2(w_router[0].T.astype(F32))
    n_tiles = n // TOKEN_TILE
    n_blocks = -(-(n * TOP_K + n_tiles * N_EXPERTS * (SEG_ALIGN - 1)) // MOE_BLOCK) + N_EXPERTS
    spare = n_blocks * MOE_BLOCK
    total_rows = spare + 2 * SORT_ROWS
    zero_rows = -(-total_rows // ((n_tiles + 1) * SEG_ALIGN)) * SEG_ALIGN
    h2, t_b, top_idx, gates, lrank, tile_cnt, buf0 = _post(
        x_flat, mixed, w_out[0].astype(BF16), ffn_norm_w[0][None, :],
        jnp.stack([wr_hi, wr_lo]), b_router[0][:, None], zero_rows)

    tile_cnt = tile_cnt.reshape(n_tiles, N_EXPERTS, LANES)[:, :, 0]
    seg_rows = (tile_cnt + SEG_ALIGN - 1) // SEG_ALIGN * SEG_ALIGN
    counts = jnp.sum(seg_rows, axis=0)
    padded = (counts + MOE_BLOCK - 1) // MOE_BLOCK * MOE_BLOCK
    pend = jnp.cumsum(padded)
    pstart = pend - padded
    dstart = pstart[None, :] + jnp.cumsum(seg_rows, axis=0) - seg_rows
    seg_end = jnp.cumsum(seg_rows, axis=1)
    offs = seg_end - seg_rows
    block_pos = jnp.arange(total_rows // MOE_BLOCK, dtype=jnp.int32) * MOE_BLOCK
    block_expert = jnp.minimum(
        jnp.sum((pend[None, :] <= block_pos[:, None]).astype(jnp.int32), axis=1), N_EXPERTS - 1)
    n_used = (pend[-1:] // MOE_BLOCK).astype(jnp.int32)
    grow = jnp.arange(SORT_GROUPS, dtype=jnp.int32)[None, :, None] * SEG_ALIGN
    inside = (offs[:, None, :] <= grow) & (grow < seg_end[:, None, :])
    group_dst = jnp.sum(jnp.where(inside, dstart[:, None, :] + grow - offs[:, None, :], 0), axis=2)
    parity = (jnp.arange(n_tiles, dtype=jnp.int32) % 2)[:, None]
    group_dst = jnp.where(jnp.any(inside, axis=2), group_dst, spare + parity * SORT_ROWS + grow[:, :, 0])
    group_dst = group_dst.astype(jnp.int32)[:, None, :]
    offs = offs.reshape(-1).astype(jnp.int32)

    buf = _dispatch(offs, group_dst, top_idx, lrank, t_b, buf0)
    y_buf = _experts(block_expert, n_used, buf, w_gate_up[0], b_gate_up[0][:, None, :],
                     w_down[0], b_down[0][:, None, :])
    out = _combine(offs, group_dst, top_idx, lrank, gates, h2, final_norm_w[None, :], y_buf)
    return out.reshape(batch, seq, D_MODEL)
```

```python
import functools

import jax
import jax.numpy as jnp
from jax import lax
from jax.experimental import pallas as pl
from jax.experimental.pallas import tpu as pltpu

F32 = jnp.float32
BF16 = jnp.bfloat16

LANES = 128
SUBLANES = 8

D_MODEL = 1024
N_META = 16
EPS = 1e-5
SSD_HEAD_DIM = 64
SSD_HEADS = 16
SSD_GROUPS = 2
SSD_STATE = 128
SSD_CONV = 4
SSD_WIDTH = SSD_HEADS * SSD_HEAD_DIM
XBC_WIDTH = SSD_WIDTH + 2 * SSD_GROUPS * SSD_STATE
GLA_HEADS = 4
GLA_KEY_DIM = 128
GLA_VAL_DIM = 256
GLA_K_WIDTH = GLA_HEADS * GLA_KEY_DIM
GLA_V_WIDTH = GLA_HEADS * GLA_VAL_DIM
GLA_RANK = 16
GLA_TAU = 16.0
GLA_SUB = 16
N_EXPERTS = 32
TOP_K = 4
D_FF = D_MODEL
SWIGLU_LIMIT = 7.0
SWIGLU_ALPHA = 1.702
MOE_BLOCK = 512

CHUNK = 128
FRONT_PAD = CHUNK - N_META
MISC_WIDTH = LANES
IN_PAD_WIDTH = SSD_WIDTH + XBC_WIDTH + 2 * GLA_K_WIDTH + 2 * GLA_V_WIDTH + MISC_WIDTH
TOKEN_TILE = 256
HALF = D_MODEL // 2
NO_ROW = 1 << 12
SEG_ALIGN = SUBLANES
PICK_ROWS = SUBLANES
CONV_CARRY = SUBLANES
SORT_ROWS = -(-(TOP_K * TOKEN_TILE + N_EXPERTS * (SEG_ALIGN - 1)) // TOKEN_TILE) * TOKEN_TILE
SORT_GROUPS = SORT_ROWS // SEG_ALIGN
VMEM_LIMIT = 56 * 1024 * 1024


def _split2(x):
    hi = x.astype(BF16)
    lo = (x - hi.astype(F32)).astype(BF16)
    return hi, lo


def _split3(x):
    hi = x.astype(BF16)
    r = x - hi.astype(F32)
    mid = r.astype(BF16)
    lo = (r - mid.astype(F32)).astype(BF16)
    return hi, mid, lo


def _dot(a, b):
    return jnp.dot(a, b, preferred_element_type=F32)


def _dot_nt(a, b):
    return lax.dot_general(a, b, (((1,), (1,)), ((), ())), preferred_element_type=F32)


def _dot_tn(a, b):
    return lax.dot_general(a, b, (((0,), (0,)), ((), ())), preferred_element_type=F32)


def _silu(x):
    return x * jax.nn.sigmoid(x)


_IN_SECTIONS = (SSD_WIDTH, XBC_WIDTH, GLA_K_WIDTH, GLA_K_WIDTH, GLA_V_WIDTH, GLA_V_WIDTH, MISC_WIDTH)
_GATE_SECTIONS = (0, 5)


def _in_proj_kernel(x_ref, meta_ref, nw_ref, w_ref, z_ref, xbc_ref, q_ref, k_ref, v_ref, g_ref, misc_ref):
    h = jnp.where(pl.program_id(0) < pl.num_programs(0) - 1, x_ref[...], meta_ref[...])
    u = h * lax.rsqrt(jnp.mean(h * h, axis=-1, keepdims=True) + EPS) * nw_ref[...]
    ub = u.astype(BF16)
    outs = (z_ref, xbc_ref, q_ref, k_ref, v_ref, g_ref, misc_ref)
    off = 0
    for section, (o_ref, width) in enumerate(zip(outs, _IN_SECTIONS)):
        p = _dot(ub, w_ref[:, off:off + width])
        if section in _GATE_SECTIONS:
            p = _silu(p)
        o_ref[...] = p.astype(o_ref.dtype)
        off += width


def _in_proj(x_flat, meta_tile, norm_w, w_in_r):
    tm = TOKEN_TILE
    nx = x_flat.shape[0] // tm
    rows = (nx + 1) * tm
    row_spec = lambda w: pl.BlockSpec((tm, w), lambda i: (i, 0))
    out_dtypes = (BF16,) * 6 + (F32,)
    return pl.pallas_call(
        _in_proj_kernel,
        grid=(nx + 1,),
        in_specs=[
            pl.BlockSpec((tm, D_MODEL), lambda i: (jnp.minimum(i, nx - 1), 0)),
            pl.BlockSpec((tm, D_MODEL), lambda i: (0, 0)),
            pl.BlockSpec((1, D_MODEL), lambda i: (0, 0)),
            pl.BlockSpec((D_MODEL, IN_PAD_WIDTH), lambda i: (0, 0)),
        ],
        out_specs=[row_spec(w) for w in _IN_SECTIONS],
        out_shape=[jax.ShapeDtypeStruct((rows, w), dt) for w, dt in zip(_IN_SECTIONS, out_dtypes)],
        compiler_params=pltpu.CompilerParams(
            dimension_semantics=("arbitrary",), vmem_limit_bytes=VMEM_LIMIT),
        name="in_proj",
    )(x_flat, meta_tile, norm_w, w_in_r)


def _block_rows(x, rows, width):
    parts = []
    for r in rows:
        if r is None:
            parts.append(jnp.zeros((width, x.shape[1]), x.dtype))
        else:
            parts.append(jnp.broadcast_to(x[r:r + 1, :], (width, x.shape[1])))
    return jnp.concatenate(parts, axis=0)


N_SEQ_INPUTS = 7
N_MIXER_PARAMS = 9


def _mixer_kernel(*refs, nseq):
    seq_refs = [refs[s * N_SEQ_INPUTS:(s + 1) * N_SEQ_INPUTS] for s in range(nseq)]
    rest = refs[nseq * N_SEQ_INPUTS:]
    param_refs = rest[:N_MIXER_PARAMS]
    out_ref, xext_ref, sstate_ref, gstate_ref = rest[N_MIXER_PARAMS:]
    c = pl.program_id(1)

    @pl.when(c == 0)
    def _():
        xext_ref[:, 0:CONV_CARRY, :] = jnp.zeros((nseq, CONV_CARRY, XBC_WIDTH), BF16)
        sstate_ref[...] = jnp.zeros_like(sstate_ref)
        gstate_ref[...] = jnp.zeros_like(gstate_ref)

    for s in range(nseq):
        _mixer_chunk(c, *seq_refs[s], *param_refs, out_ref.at[s], xext_ref.at[s], sstate_ref.at[s],
                     gstate_ref.at[s])


def _mixer_chunk(c, z_ref, xbc_ref, q_ref, k_ref, v_ref, g_ref, misc_ref,
                 convw_ref, convb_ref, dtb_ref, aneg_ref, dskip_ref, ssdw_ref,
                 wdec_ref, bdec_ref, glaw_ref,
                 out_ref,
                 xext_ref, sstate_ref, gstate_ref):

    row = lax.broadcasted_iota(jnp.int32, (CHUNK, CHUNK), 0)
    col = lax.broadcasted_iota(jnp.int32, (CHUNK, CHUNK), 1)
    causal = row >= col
    tri = jnp.where(causal, 1.0, 0.0).astype(BF16)
    valid = jnp.logical_or(c > 0, row >= FRONT_PAD)

    xcur = xbc_ref[...]
    xext_ref[CONV_CARRY:CONV_CARRY + CHUNK, :] = xcur
    xall = xext_ref[...]
    srow = lax.broadcasted_iota(jnp.int32, (CHUNK, CHUNK + CONV_CARRY), 0)
    scol = lax.broadcasted_iota(jnp.int32, (CHUNK, CHUNK + CONV_CARRY), 1)
    conv = convb_ref[...] + convw_ref[SSD_CONV - 1:SSD_CONV, :] * xcur.astype(F32)
    for kk in range(SSD_CONV - 1):
        shift = jnp.where(scol == srow + CONV_CARRY - (SSD_CONV - 1) + kk, 1.0, 0.0).astype(BF16)
        conv = conv + convw_ref[kk:kk + 1, :] * _dot(shift, xall)
    xext_ref[0:CONV_CARRY, :] = xext_ref[CHUNK:CHUNK + CONV_CARRY, :]
    valid_w = jnp.logical_or(c > 0, lax.broadcasted_iota(jnp.int32, (CHUNK, XBC_WIDTH), 0) >= FRONT_PAD)
    act = jnp.where(valid_w, _silu(conv), 0.0)
    xs = act[:, :SSD_WIDTH]
    bmat = act[:, SSD_WIDTH:SSD_WIDTH + SSD_GROUPS * SSD_STATE].astype(BF16)
    cmat = act[:, SSD_WIDTH + SSD_GROUPS * SSD_STATE:].astype(BF16)

    misc = misc_ref[...]
    dt = jnp.where(valid, jax.nn.softplus(misc + dtb_ref[...]), 0.0)
    d_a = dt * aneg_ref[...]
    p0, p1, p2 = _split3(d_a)
    a_cs = _dot(tri, p0) + _dot(tri, p1) + _dot(tri, p2)
    a_cs_t = a_cs.T

    lo_half = col < SSD_HEAD_DIM
    hg = SSD_HEADS // SSD_GROUPS
    pairs_per_group = hg // 2
    y_parts = []
    for gi in range(SSD_GROUPS):
        b_g = bmat[:, gi * SSD_STATE:(gi + 1) * SSD_STATE]
        c_g = cmat[:, gi * SSD_STATE:(gi + 1) * SSD_STATE]
        c_g32 = c_g.astype(F32)
        cb = _dot_nt(c_g, b_g)
        w_parts = []
        cd_parts = []
        for pj in range(pairs_per_group):
            h0 = gi * hg + 2 * pj
            h1 = h0 + 1
            lanes = slice(h0 * SSD_HEAD_DIM, (h1 + 1) * SSD_HEAD_DIM)
            ab0 = jnp.broadcast_to(a_cs[:, h0:h0 + 1], (CHUNK, CHUNK))
            ab1 = jnp.broadcast_to(a_cs[:, h1:h1 + 1], (CHUNK, CHUNK))
            db0 = jnp.broadcast_to(dt[:, h0:h0 + 1], (CHUNK, CHUNK))
            db1 = jnp.broadcast_to(dt[:, h1:h1 + 1], (CHUNK, CHUNK))
            l0 = jnp.exp(jnp.where(causal, ab0 - a_cs_t[h0:h0 + 1, :], -jnp.inf))
            l1 = jnp.exp(jnp.where(causal, ab1 - a_cs_t[h1:h1 + 1, :], -jnp.inf))
            lhs = jnp.concatenate(
                [(cb * l0).astype(BF16), (c_g32 * jnp.exp(ab0)).astype(BF16),
                 (cb * l1).astype(BF16), (c_g32 * jnp.exp(ab1)).astype(BF16)], axis=1)
            xs_p = xs[:, lanes]
            dtx = xs_p * jnp.where(lo_half, db0, db1)
            st = sstate_ref[gi, :, pj * CHUNK:(pj + 1) * CHUNK]
            rhs = jnp.concatenate(
                [jnp.where(lo_half, dtx, 0.0).astype(BF16), jnp.where(lo_half, st, 0.0).astype(BF16),
                 jnp.where(lo_half, 0.0, dtx).astype(BF16), jnp.where(lo_half, 0.0, st).astype(BF16)],
                axis=0)
            y_p = _dot(lhs, rhs) + dskip_ref[:, lanes] * xs_p
            y_parts.append(y_p)
            ae = jnp.where(lo_half, ab0, ab1)
            a_last = ae[CHUNK - 1:CHUNK, :]
            w_parts.append((dtx * jnp.exp(a_last - ae)).astype(BF16))
            cd_parts.append(jnp.exp(a_last))
        upd = _dot_tn(b_g, jnp.concatenate(w_parts, axis=1))
        sstate_ref[gi] = jnp.concatenate(cd_parts, axis=1) * sstate_ref[gi] + upd
    y = jnp.concatenate(y_parts, axis=1)
    y = y * z_ref[...].astype(F32)
    gsz = SSD_WIDTH // SSD_GROUPS
    for gi in range(SSD_GROUPS):
        yg = y[:, gi * gsz:(gi + 1) * gsz]
        yn = yg * lax.rsqrt(jnp.mean(yg * yg, axis=-1, keepdims=True) + EPS)
        out_ref[:, gi * gsz:(gi + 1) * gsz] = (yn * ssdw_ref[:, gi * gsz:(gi + 1) * gsz]).astype(out_ref.dtype)

    m0, m1 = _split2(misc)
    pre = _dot(m0, wdec_ref[...]) + _dot(m1, wdec_ref[...]) + bdec_ref[...]
    log_a = jax.nn.log_sigmoid(pre) * (1.0 / GLA_TAU)
    g0, g1 = _split2(log_a)
    gcum = _dot(tri, g0) + _dot(tri, g1)
    roww = lax.broadcasted_iota(jnp.int32, (CHUNK, GLA_K_WIDTH), 0)
    qf = q_ref[...].astype(F32) * (GLA_KEY_DIM ** -0.5)
    kf = k_ref[...].astype(F32)
    nsub = CHUNK // GLA_SUB
    g_start = _block_rows(gcum, [None] + [GLA_SUB * i - 1 for i in range(1, nsub)], GLA_SUB)
    e0 = gcum - g_start
    q_lv = [(qf * jnp.exp(e0)).astype(BF16)]
    k_lv = [(kf * jnp.exp(-e0)).astype(BF16)]
    shifts = (4, 5, 6)
    for sh in shifts:
        b = 1 << sh
        bound = _block_rows(gcum, [2 * b * i + b - 1 for i in range(CHUNK // (2 * b))], 2 * b)
        second = ((roww >> sh) & 1) == 1
        x_l = jnp.exp(jnp.where(second, gcum - bound, bound - gcum))
        q_lv.append((qf * x_l).astype(BF16))
        k_lv.append((kf * x_l).astype(BF16))
    masks = [jnp.logical_and((row >> 4) == (col >> 4), causal)]
    for sh in shifts:
        same = (row >> (sh + 1)) == (col >> (sh + 1))
        m = jnp.logical_and(same, jnp.logical_and(((row >> sh) & 1) == 1, ((col >> sh) & 1) == 0))
        masks.append(m)
    g_last = gcum[CHUNK - 1:CHUNK, :]
    q_in = (qf * jnp.exp(gcum)).astype(BF16)
    k_end = (kf * jnp.exp(g_last - gcum)).astype(BF16)
    dec = jnp.exp(g_last)
    vb = v_ref[...]
    for hh in range(GLA_HEADS):
        kl = slice(hh * GLA_KEY_DIM, (hh + 1) * GLA_KEY_DIM)
        vl = slice(hh * GLA_VAL_DIM, (hh + 1) * GLA_VAL_DIM)
        scores = jnp.zeros((CHUNK, CHUNK), F32)
        for lv in range(len(masks)):
            scores = scores + jnp.where(masks[lv], _dot_nt(q_lv[lv][:, kl], k_lv[lv][:, kl]), 0.0)
        v_h = vb[:, vl]
        s_t = gstate_ref[hh]
        o = _dot(scores.astype(BF16), v_h) + _dot_nt(q_in[:, kl], s_t.astype(BF16))
        gstate_ref[hh] = dec[:, kl] * s_t + _dot_tn(v_h, k_end[:, kl])
        o = o * lax.rsqrt(jnp.mean(o * o, axis=-1, keepdims=True) + EPS) * glaw_ref[...]
        o = o * g_ref[:, vl].astype(F32)
        out_ref[:, SSD_WIDTH + hh * GLA_VAL_DIM:SSD_WIDTH + (hh + 1) * GLA_VAL_DIM] = o.astype(out_ref.dtype)


def _mixer(z, xbc, q, k, v, g, misc, params, batch, nchunks):
    widths = (SSD_WIDTH, XBC_WIDTH, GLA_K_WIDTH, GLA_K_WIDTH, GLA_V_WIDTH, GLA_V_WIDTH, MISC_WIDTH)
    assert len(params) == N_MIXER_PARAMS
    nseq = 4 if batch % 4 == 0 else (2 if batch % 2 == 0 else 1)
    meta_block = z.shape[0] // CHUNK - 1

    def row_spec(w, s):
        return pl.BlockSpec(
            (CHUNK, w),
            lambda bp, c: (jnp.where(c == 0, meta_block, (bp * nseq + s) * (nchunks - 1) + c - 1), 0))

    par_spec = lambda p: pl.BlockSpec(p.shape, lambda bp, c: (0,) * p.ndim)
    seq = (nchunks - 1) * CHUNK
    width = SSD_WIDTH + GLA_V_WIDTH
    out = pl.pallas_call(
        functools.partial(_mixer_kernel, nseq=nseq),
        grid=(batch // nseq, nchunks),
        in_specs=[row_spec(w, s) for s in range(nseq) for w in widths] + [par_spec(p) for p in params],
        out_specs=pl.BlockSpec((nseq, CHUNK, width), lambda bp, c: (bp, jnp.maximum(c - 1, 0), 0)),
        out_shape=jax.ShapeDtypeStruct((batch, seq, width), BF16),
        scratch_shapes=[
            pltpu.VMEM((nseq, CHUNK + CONV_CARRY, XBC_WIDTH), BF16),
            pltpu.VMEM((nseq, SSD_GROUPS, SSD_STATE, SSD_WIDTH // SSD_GROUPS), F32),
            pltpu.VMEM((nseq, GLA_HEADS, GLA_VAL_DIM, GLA_KEY_DIM), F32),
        ],
        compiler_params=pltpu.CompilerParams(
            dimension_semantics=("arbitrary", "arbitrary"), vmem_limit_bytes=VMEM_LIMIT),
        name="mixer",
    )(*([z, xbc, q, k, v, g, misc] * nseq), *params)
    return out.reshape(batch * seq, width)


def _post_kernel(x_ref, mixed_ref, wout_ref, fnw_ref, wrt_ref, br_ref,
                 h2_ref, t_ref, idx_ref, gate_ref, lrank_ref, cnt_ref, zero_ref, tprev_ref):
    @pl.when(pl.program_id(0) == 0)
    def _():
        tprev_ref[...] = jnp.zeros_like(tprev_ref)

    t_prev = tprev_ref[...]
    h2 = x_ref[...] + _dot(mixed_ref[...], wout_ref[...])
    h2_ref[...] = h2
    t = h2 * lax.rsqrt(jnp.mean(h2 * h2, axis=-1, keepdims=True) + EPS) * fnw_ref[...]
    t_ref[...] = t.astype(BF16)
    tprev_ref[...] = t

    t0, t1 = _split2(t_prev)
    w0 = wrt_ref[0]
    w1 = wrt_ref[1]
    logits = _dot_nt(w0, t0) + _dot_nt(w0, t1) + _dot_nt(w1, t0) + br_ref[...]
    e_iota = lax.broadcasted_iota(jnp.int32, logits.shape, 0)
    work = logits
    sel_any = jnp.zeros(logits.shape, F32)
    tops = []
    idxs = []
    onehots = []
    for _ in range(TOP_K):
        m = jnp.max(work, axis=0, keepdims=True)
        idx = jnp.min(jnp.where(work == m, e_iota, N_EXPERTS), axis=0, keepdims=True)
        hit = e_iota == idx
        tops.append(m)
        idxs.append(idx)
        onehots.append(hit)
        sel_any = sel_any + jnp.where(hit, 1.0, 0.0)
        work = jnp.where(hit, -jnp.inf, work)
    exps = [jnp.exp(tv - tops[0]) for tv in tops]
    denom = exps[0] + exps[1] + exps[2] + exps[3]
    tile = logits.shape[1]
    pad_rows = PICK_ROWS - TOP_K
    gate_ref[...] = jnp.concatenate([e / denom for e in exps] + [jnp.zeros((pad_rows, tile), F32)], axis=0)
    idx_ref[...] = jnp.concatenate(idxs + [jnp.zeros((pad_rows, tile), jnp.int32)], axis=0)

    r_i = lax.broadcasted_iota(jnp.int32, (tile, tile), 0)
    c_i = lax.broadcasted_iota(jnp.int32, (tile, tile), 1)
    upper = jnp.where(r_i < c_i, 1.0, 0.0).astype(BF16)
    before = _dot(sel_any.astype(BF16), upper)
    ranks = [jnp.sum(jnp.where(h, before, 0.0), axis=0, keepdims=True) for h in onehots]
    ranks.append(jnp.full((pad_rows, tile), float(NO_ROW), F32))
    lrank_ref[...] = jnp.concatenate(ranks, axis=0).astype(jnp.int32)
    cnt = jnp.sum(sel_any, axis=1, keepdims=True)
    cnt_ref[...] = jnp.broadcast_to(cnt, cnt_ref.shape).astype(jnp.int32)
    zero_ref[...] = jnp.zeros_like(zero_ref)


def _post(x_flat, mixed, w_out_b, ffn_norm_w, w_router_t, b_router_col, zero_rows):
    n = x_flat.shape[0]
    tt = TOKEN_TILE
    n_tiles = n // tt
    const = lambda shape: pl.BlockSpec(shape, lambda i: (0,) * len(shape))
    cur = lambda i: jnp.minimum(i, n_tiles - 1)
    prev = lambda i: jnp.maximum(i - 1, 0)
    return pl.pallas_call(
        _post_kernel,
        grid=(n_tiles + 1,),
        in_specs=[
            pl.BlockSpec((tt, D_MODEL), lambda i: (cur(i), 0)),
            pl.BlockSpec((tt, SSD_WIDTH + GLA_V_WIDTH), lambda i: (cur(i), 0)),
            const(w_out_b.shape),
            const((1, D_MODEL)),
            const(w_router_t.shape),
            const((N_EXPERTS, 1)),
        ],
        out_specs=[
            pl.BlockSpec((tt, D_MODEL), lambda i: (cur(i), 0)),
            pl.BlockSpec((tt, D_MODEL), lambda i: (cur(i), 0)),
            pl.BlockSpec((PICK_ROWS, tt), lambda i: (0, prev(i))),
            pl.BlockSpec((PICK_ROWS, tt), lambda i: (0, prev(i))),
            pl.BlockSpec((PICK_ROWS, tt), lambda i: (0, prev(i))),
            pl.BlockSpec((N_EXPERTS, LANES), lambda i: (prev(i), 0)),
            pl.BlockSpec((zero_rows, HALF), lambda i: (i, 0)),
        ],
        out_shape=[
            jax.ShapeDtypeStruct((n, D_MODEL), F32),
            jax.ShapeDtypeStruct((n, D_MODEL), BF16),
            jax.ShapeDtypeStruct((PICK_ROWS, n), jnp.int32),
            jax.ShapeDtypeStruct((PICK_ROWS, n), F32),
            jax.ShapeDtypeStruct((PICK_ROWS, n), jnp.int32),
            jax.ShapeDtypeStruct((n_tiles * N_EXPERTS, LANES), jnp.int32),
            jax.ShapeDtypeStruct(((n_tiles + 1) * zero_rows, HALF), jnp.uint32),
        ],
        scratch_shapes=[pltpu.VMEM((tt, D_MODEL), F32)],
        compiler_params=pltpu.CompilerParams(
            dimension_semantics=("arbitrary",), vmem_limit_bytes=VMEM_LIMIT),
        name="post",
    )(x_flat, mixed, w_out_b, ffn_norm_w, w_router_t, b_router_col)


def _pack_pairs(x):
    lo = pltpu.bitcast(x[:, :HALF], jnp.uint32) >> 16
    hi = pltpu.bitcast(x[:, HALF:], jnp.uint32) & jnp.uint32(0xFFFF0000)
    return lo | hi


def _unpack_pairs(w):
    lo = pltpu.bitcast(w << 16, F32).astype(BF16)
    hi = pltpu.bitcast(w & jnp.uint32(0xFFFF0000), F32).astype(BF16)
    return lo, hi


def _sorted_positions(idx, lrank, offs_ref, base):
    pos = lrank
    for e in range(N_EXPERTS):
        pos = pos + jnp.where(idx == e, offs_ref[base + e], 0)
    return pos


def _group_copies(dst_ref, make_copy):
    return [make_copy(g * SEG_ALIGN, pl.multiple_of(dst_ref[0, 0, g], SEG_ALIGN))
            for g in range(SORT_GROUPS)]


def _start_all(copies):
    for g, cp in enumerate(copies):
        cp.start(priority=g % 2)


def _dispatch_kernel(offs_ref, idx_ref, lrank_ref, t_ref, dst_ref, dst_prev_ref, buf_in_ref, buf_ref,
                     sorted_ref, sem):
    del buf_in_ref
    i = pl.program_id(0)
    slot = i % 2
    base = i * N_EXPERTS
    tt = t_ref.shape[0]
    pos = _sorted_positions(idx_ref[...], lrank_ref[...], offs_ref, base)
    t = t_ref[...]
    for r0 in range(0, SORT_ROWS, tt):
        prow = lax.broadcasted_iota(jnp.int32, (tt, tt), 0) + r0
        perm = jnp.zeros((tt, tt), F32)
        for kk in range(TOP_K):
            perm = jnp.where(prow == pos[kk:kk + 1, :], 1.0, perm)
        sorted_ref[slot, r0:r0 + tt, :] = _pack_pairs(_dot(perm.astype(BF16), t))

    def copies(table_ref, which):
        def make_copy(local, glob):
            return pltpu.make_async_copy(
                sorted_ref.at[which, pl.ds(local, SEG_ALIGN), :],
                buf_ref.at[pl.ds(glob, SEG_ALIGN), :], sem.at[which])
        return _group_copies(table_ref, make_copy)

    _start_all(copies(dst_ref, slot))

    @pl.when(i > 0)
    def _():
        for cp in copies(dst_prev_ref, 1 - slot):
            cp.wait()

    @pl.when(i == pl.num_programs(0) - 1)
    def _():
        for cp in copies(dst_ref, slot):
            cp.wait()


def _dispatch(offs, group_dst, top_idx, lrank, t_b, buf0):
    n = t_b.shape[0]
    tt = TOKEN_TILE
    table = lambda index: pl.BlockSpec((1, 1, SORT_GROUPS), index, memory_space=pltpu.SMEM)
    grid_spec = pltpu.PrefetchScalarGridSpec(
        num_scalar_prefetch=1,
        grid=(n // tt,),
        in_specs=[
            pl.BlockSpec((PICK_ROWS, tt), lambda i, *_: (0, i)),
            pl.BlockSpec((PICK_ROWS, tt), lambda i, *_: (0, i)),
            pl.BlockSpec((tt, D_MODEL), lambda i, *_: (i, 0)),
            table(lambda i, *_: (i, 0, 0)),
            table(lambda i, *_: (jnp.maximum(i - 1, 0), 0, 0)),
            pl.BlockSpec(memory_space=pl.ANY),
        ],
        out_specs=pl.BlockSpec(memory_space=pl.ANY),
        scratch_shapes=[pltpu.VMEM((2, SORT_ROWS, HALF), jnp.uint32), pltpu.SemaphoreType.DMA((2,))],
    )
    return pl.pallas_call(
        _dispatch_kernel,
        grid_spec=grid_spec,
        out_shape=jax.ShapeDtypeStruct(buf0.shape, jnp.uint32),
        input_output_aliases={6: 0},
        compiler_params=pltpu.CompilerParams(
            dimension_semantics=("arbitrary",), vmem_limit_bytes=VMEM_LIMIT),
        name="dispatch",
    )(offs, top_idx, lrank, t_b, group_dst, group_dst, buf0)


def _expert_kernel(be_ref, nused_ref, xp_ref, wgu_ref, bgu_ref, wd_ref, bd_ref, y_ref,
                   wgu_b_ref, wd_b_ref):
    i = pl.program_id(0)
    used = i < nused_ref[0]

    @pl.when(jnp.logical_not(used))
    def _():
        y_ref[...] = jnp.zeros_like(y_ref)

    new_expert = jnp.logical_or(i == 0, be_ref[i] != be_ref[jnp.maximum(i - 1, 0)])

    @pl.when(jnp.logical_and(used, new_expert))
    def _():
        rows = CHUNK
        for r0 in range(0, D_MODEL, rows):
            wgu_b_ref[r0:r0 + rows, :] = wgu_ref[r0:r0 + rows, :].astype(BF16)
        for r0 in range(0, D_FF, rows):
            wd_b_ref[r0:r0 + rows, :] = wd_ref[r0:r0 + rows, :].astype(BF16)

    @pl.when(used)
    def _():
        x_lo, x_hi = _unpack_pairs(xp_ref[...])
        hgu = _dot(x_lo, wgu_b_ref[:HALF, :]) + _dot(x_hi, wgu_b_ref[HALF:, :]) + bgu_ref[...]
        gate = jnp.minimum(hgu[:, :D_FF], SWIGLU_LIMIT)
        up = jnp.clip(hgu[:, D_FF:], -SWIGLU_LIMIT, SWIGLU_LIMIT)
        act = gate * jax.nn.sigmoid(SWIGLU_ALPHA * gate)
        y = _dot(((up + 1.0) * act).astype(BF16), wd_b_ref[...]) + bd_ref[...]
        y_ref[...] = _pack_pairs(y.astype(BF16).astype(F32))


def _experts(block_expert, n_used, buf, w_gu, b_gu, w_d, b_d):
    n_blocks = block_expert.shape[0]
    rows = n_blocks * MOE_BLOCK
    row_map = lambda i, be, nu: (jnp.minimum(i, nu[0] - 1), 0)
    w_map = lambda i, be, nu: (be[jnp.minimum(i, nu[0] - 1)], 0, 0)
    grid_spec = pltpu.PrefetchScalarGridSpec(
        num_scalar_prefetch=2,
        grid=(n_blocks,),
        in_specs=[
            pl.BlockSpec((MOE_BLOCK, HALF), row_map),
            pl.BlockSpec((None, D_MODEL, 2 * D_FF), w_map),
            pl.BlockSpec((None, 1, 2 * D_FF), w_map),
            pl.BlockSpec((None, D_FF, D_MODEL), w_map),
            pl.BlockSpec((None, 1, D_MODEL), w_map),
        ],
        out_specs=pl.BlockSpec((MOE_BLOCK, HALF), lambda i, be, nu: (i, 0)),
        scratch_shapes=[pltpu.VMEM((D_MODEL, 2 * D_FF), BF16), pltpu.VMEM((D_FF, D_MODEL), BF16)],
    )
    return pl.pallas_call(
        _expert_kernel,
        grid_spec=grid_spec,
        out_shape=jax.ShapeDtypeStruct((rows, HALF), jnp.uint32),
        compiler_params=pltpu.CompilerParams(
            dimension_semantics=("arbitrary",), vmem_limit_bytes=VMEM_LIMIT),
        name="experts",
    )(block_expert, n_used, buf, w_gu, b_gu, w_d, b_d)


def _combine_kernel(offs_ref, idx_ref, lrank_ref, gate_ref, h2_ref, fw_ref, dst_ref, dst_next_ref,
                    y_hbm_ref, out_ref, ys_ref, sem):
    i = pl.program_id(0)
    slot = i % 2
    base = i * N_EXPERTS
    tt = h2_ref.shape[0]

    def copies(table_ref, which):
        def make_copy(local, glob):
            return pltpu.make_async_copy(
                y_hbm_ref.at[pl.ds(glob, SEG_ALIGN), :],
                ys_ref.at[which, pl.ds(local, SEG_ALIGN), :], sem.at[which])
        return _group_copies(table_ref, make_copy)

    @pl.when(i == 0)
    def _():
        _start_all(copies(dst_ref, slot))

    @pl.when(i < pl.num_programs(0) - 1)
    def _():
        _start_all(copies(dst_next_ref, 1 - slot))

    pos = _sorted_positions(idx_ref[...], lrank_ref[...], offs_ref, base).astype(F32)
    zpad = jnp.zeros((LANES - PICK_ROWS, tt), F32)
    pos_c = jnp.concatenate([pos, zpad], axis=0).T
    gate_c = jnp.concatenate([gate_ref[...], zpad], axis=0).T
    lane = lax.broadcasted_iota(jnp.int32, (tt, SORT_ROWS), 1).astype(F32)
    pg = jnp.zeros((tt, SORT_ROWS), F32)
    for kk in range(TOP_K):
        pg = jnp.where(lane == pos_c[:, kk:kk + 1], gate_c[:, kk:kk + 1], pg)
    p_hi, p_lo = _split2(pg)

    for cp in copies(dst_ref, slot):
        cp.wait()
    y_lo, y_hi = _unpack_pairs(ys_ref[slot])
    ffn = jnp.concatenate(
        [_dot(p_hi, y_lo) + _dot(p_lo, y_lo), _dot(p_hi, y_hi) + _dot(p_lo, y_hi)], axis=1)
    acc = h2_ref[...] + ffn
    out_ref[...] = acc * lax.rsqrt(jnp.mean(acc * acc, axis=-1, keepdims=True) + EPS) * fw_ref[...]


def _combine(offs, group_dst, top_idx, lrank, gates, h2, final_norm_w, y_buf):
    n = h2.shape[0]
    tt = TOKEN_TILE
    grid_spec = pltpu.PrefetchScalarGridSpec(
        num_scalar_prefetch=1,
        grid=(n // tt,),
        in_specs=[
            pl.BlockSpec((PICK_ROWS, tt), lambda i, *_: (0, i)),
            pl.BlockSpec((PICK_ROWS, tt), lambda i, *_: (0, i)),
            pl.BlockSpec((PICK_ROWS, tt), lambda i, *_: (0, i)),
            pl.BlockSpec((tt, D_MODEL), lambda i, *_: (i, 0)),
            pl.BlockSpec((1, D_MODEL), lambda i, *_: (0, 0)),
            pl.BlockSpec((1, 1, SORT_GROUPS), lambda i, *_: (i, 0, 0), memory_space=pltpu.SMEM),
            pl.BlockSpec((1, 1, SORT_GROUPS), lambda i, *_: (jnp.minimum(i + 1, n // tt - 1), 0, 0),
                         memory_space=pltpu.SMEM),
            pl.BlockSpec(memory_space=pl.ANY),
        ],
        out_specs=pl.BlockSpec((tt, D_MODEL), lambda i, *_: (i, 0)),
        scratch_shapes=[pltpu.VMEM((2, SORT_ROWS, HALF), jnp.uint32), pltpu.SemaphoreType.DMA((2,))],
    )
    return pl.pallas_call(
        _combine_kernel,
        grid_spec=grid_spec,
        out_shape=jax.ShapeDtypeStruct((n, D_MODEL), F32),
        compiler_params=pltpu.CompilerParams(
            dimension_semantics=("arbitrary",), vmem_limit_bytes=VMEM_LIMIT),
        name="combine",
    )(offs, top_idx, lrank, gates, h2, final_norm_w, group_dst, group_dst, y_buf)


def _pad_lanes(v, width):
    return jnp.pad(v, ((0, 0), (0, width - v.shape[1])))


def kernel(x, meta_tokens, mix_norm_w, w_in, conv_w, conv_b, dt_bias, a_log, d_skip, ssd_norm_w,
           w_decay_up, b_decay, gla_norm_w, w_out, ffn_norm_w, w_router, b_router, w_gate_up,
           b_gate_up, w_down, b_down, final_norm_w):
    batch, seq, d = x.shape
    assert d == D_MODEL and seq % TOKEN_TILE == 0
    assert mix_norm_w.shape[0] == 1, "single-layer block"
    nchunks = (FRONT_PAD + N_META + seq) // CHUNK

    n = batch * seq
    x_flat = x.reshape(n, D_MODEL)
    meta_tile = jnp.concatenate(
        [jnp.zeros((TOKEN_TILE - N_META, D_MODEL), x.dtype), meta_tokens.astype(x.dtype)], axis=0)

    wi = w_in[0]
    o_z, o_xbc = 0, SSD_WIDTH
    o_dt = o_xbc + XBC_WIDTH
    o_q = o_dt + SSD_HEADS
    o_k = o_q + GLA_K_WIDTH
    o_v = o_k + GLA_K_WIDTH
    o_g = o_v + GLA_V_WIDTH
    o_a = o_g + GLA_V_WIDTH
    w_misc = jnp.concatenate(
        [wi[:, o_dt:o_dt + SSD_HEADS], wi[:, o_a:o_a + GLA_RANK],
         jnp.zeros((D_MODEL, MISC_WIDTH - SSD_HEADS - GLA_RANK), wi.dtype)], axis=1)
    w_in_r = jnp.concatenate(
        [wi[:, o_z:o_dt], wi[:, o_q:o_a], w_misc], axis=1).astype(BF16)
    z, xbc, q, k, v, g, misc = _in_proj(x_flat, meta_tile, mix_norm_w[0][None, :], w_in_r)

    dtb = _pad_lanes(dt_bias[0][None, :].astype(F32), MISC_WIDTH)
    aneg = _pad_lanes(-jnp.exp(a_log[0].astype(F32))[None, :], MISC_WIDTH)
    dskip = jnp.repeat(d_skip[0].astype(F32), SSD_HEAD_DIM)[None, :]
    wdec = jnp.zeros((MISC_WIDTH, GLA_K_WIDTH), F32).at[SSD_HEADS:SSD_HEADS + GLA_RANK].set(w_decay_up[0])
    params = (conv_w[0], conv_b[0][None, :], dtb, aneg, dskip, ssd_norm_w[0][None, :],
              wdec.astype(BF16), b_decay[0][None, :], gla_norm_w[0][None, :])
    mixed = _mixer(z, xbc, q, k, v, g, misc, params, batch, nchunks)

    wr_hi, wr_lo = _split2(w_router[0].T.astype(F32))
    n_tiles = n // TOKEN_TILE
    n_blocks = -(-(n * TOP_K + n_tiles * N_EXPERTS * (SEG_ALIGN - 1)) // MOE_BLOCK) + N_EXPERTS
    spare = n_blocks * MOE_BLOCK
    total_rows = spare + 2 * SORT_ROWS
    zero_rows = -(-total_rows // ((n_tiles + 1) * SEG_ALIGN)) * SEG_ALIGN
    h2, t_b, top_idx, gates, lrank, tile_cnt, buf0 = _post(
        x_flat, mixed, w_out[0].astype(BF16), ffn_norm_w[0][None, :],
        jnp.stack([wr_hi, wr_lo]), b_router[0][:, None], zero_rows)

    tile_cnt = tile_cnt.reshape(n_tiles, N_EXPERTS, LANES)[:, :, 0]
    seg_rows = (tile_cnt + SEG_ALIGN - 1) // SEG_ALIGN * SEG_ALIGN
    counts = jnp.sum(seg_rows, axis=0)
    padded = (counts + MOE_BLOCK - 1) // MOE_BLOCK * MOE_BLOCK
    pend = jnp.cumsum(padded)
    pstart = pend - padded
    dstart = pstart[None, :] + jnp.cumsum(seg_rows, axis=0) - seg_rows
    seg_end = jnp.cumsum(seg_rows, axis=1)
    offs = seg_end - seg_rows
    block_pos = jnp.arange(total_rows // MOE_BLOCK, dtype=jnp.int32) * MOE_BLOCK
    block_expert = jnp.minimum(
        jnp.sum((pend[None, :] <= block_pos[:, None]).astype(jnp.int32), axis=1), N_EXPERTS - 1)
    n_used = (pend[-1:] // MOE_BLOCK).astype(jnp.int32)
    grow = jnp.arange(SORT_GROUPS, dtype=jnp.int32)[None, :, None] * SEG_ALIGN
    inside = (offs[:, None, :] <= grow) & (grow < seg_end[:, None, :])
    group_dst = jnp.sum(jnp.where(inside, dstart[:, None, :] + grow - offs[:, None, :], 0), axis=2)
    parity = (jnp.arange(n_tiles, dtype=jnp.int32) % 2)[:, None]
    group_dst = jnp.where(jnp.any(inside, axis=2), group_dst, spare + parity * SORT_ROWS + grow[:, :, 0])
    group_dst = group_dst.astype(jnp.int32)[:, None, :]
    offs = offs.reshape(-1).astype(jnp.int32)

    buf = _dispatch(offs, group_dst, top_idx, lrank, t_b, buf0)
    y_buf = _experts(block_expert, n_used, buf, w_gate_up[0], b_gate_up[0][:, None, :],
                     w_down[0], b_down[0][:, None, :])
    out = _combine(offs, group_dst, top_idx, lrank, gates, h2, final_norm_w[None, :], y_buf)
    return out.reshape(batch, seq, D_MODEL)
```

```python
import functools

import jax
import jax.numpy as jnp
from jax import lax
from jax.experimental import pallas as pl
from jax.experimental.pallas import tpu as pltpu

F32 = jnp.float32
BF16 = jnp.bfloat16

LANES = 128
SUBLANES = 8

D_MODEL = 1024
N_META = 16
EPS = 1e-5
SSD_HEAD_DIM = 64
SSD_HEADS = 16
SSD_GROUPS = 2
SSD_STATE = 128
SSD_CONV = 4
SSD_WIDTH = SSD_HEADS * SSD_HEAD_DIM
XBC_WIDTH = SSD_WIDTH + 2 * SSD_GROUPS * SSD_STATE
GLA_HEADS = 4
GLA_KEY_DIM = 128
GLA_VAL_DIM = 256
GLA_K_WIDTH = GLA_HEADS * GLA_KEY_DIM
GLA_V_WIDTH = GLA_HEADS * GLA_VAL_DIM
GLA_RANK = 16
GLA_TAU = 16.0
GLA_SUB = 16
N_EXPERTS = 32
TOP_K = 4
D_FF = D_MODEL
SWIGLU_LIMIT = 7.0
SWIGLU_ALPHA = 1.702
MOE_BLOCK = 512

CHUNK = 128
FRONT_PAD = CHUNK - N_META
MISC_WIDTH = LANES
TOKEN_TILE = 256
HALF = D_MODEL // 2
NO_ROW = 1 << 12
SEG_ALIGN = SUBLANES
PICK_ROWS = SUBLANES
CONV_CARRY = SUBLANES
SORT_ROWS = -(-(TOP_K * TOKEN_TILE + N_EXPERTS * (SEG_ALIGN - 1)) // TOKEN_TILE) * TOKEN_TILE
SORT_GROUPS = SORT_ROWS // SEG_ALIGN
VMEM_LIMIT = 56 * 1024 * 1024


def _split2(x):
    hi = x.astype(BF16)
    lo = (x - hi.astype(F32)).astype(BF16)
    return hi, lo


def _split3(x):
    hi = x.astype(BF16)
    r = x - hi.astype(F32)
    mid = r.astype(BF16)
    lo = (r - mid.astype(F32)).astype(BF16)
    return hi, mid, lo


def _dot(a, b):
    return jnp.dot(a, b, preferred_element_type=F32)


def _dot_nt(a, b):
    return lax.dot_general(a, b, (((1,), (1,)), ((), ())), preferred_element_type=F32)


def _dot_tn(a, b):
    return lax.dot_general(a, b, (((0,), (0,)), ((), ())), preferred_element_type=F32)


def _silu(x):
    return x * jax.nn.sigmoid(x)


_IN_SECTIONS = (SSD_WIDTH, XBC_WIDTH, GLA_K_WIDTH, GLA_K_WIDTH, GLA_V_WIDTH, GLA_V_WIDTH, MISC_WIDTH)
_GATE_SECTIONS = (0, 5)


_SECTION_SOURCE = ((0, 0), (0, SSD_WIDTH),
                   (1, 0), (1, GLA_K_WIDTH), (1, 2 * GLA_K_WIDTH), (1, 2 * GLA_K_WIDTH + GLA_V_WIDTH),
                   (2, 0))


def _in_proj_kernel(x_ref, meta_ref, nw_ref, wa_ref, wb_ref, wc_ref,
                    z_ref, xbc_ref, q_ref, k_ref, v_ref, g_ref, misc_ref):
    h = jnp.where(pl.program_id(0) < pl.num_programs(0) - 1, x_ref[...], meta_ref[...])
    u = h * lax.rsqrt(jnp.mean(h * h, axis=-1, keepdims=True) + EPS) * nw_ref[...]
    ub = u.astype(BF16)
    outs = (z_ref, xbc_ref, q_ref, k_ref, v_ref, g_ref, misc_ref)
    w_refs = (wa_ref, wb_ref, wc_ref)
    for section, (o_ref, width) in enumerate(zip(outs, _IN_SECTIONS)):
        src, off = _SECTION_SOURCE[section]
        p = _dot(ub, w_refs[src][:, off:off + width])
        if section in _GATE_SECTIONS:
            p = _silu(p)
        o_ref[...] = p.astype(o_ref.dtype)


def _in_proj(x_flat, meta_tile, norm_w, w_parts):
    tm = TOKEN_TILE
    nx = x_flat.shape[0] // tm
    rows = (nx + 1) * tm
    row_spec = lambda w: pl.BlockSpec((tm, w), lambda i: (i, 0))
    out_dtypes = (BF16,) * 6 + (F32,)
    return pl.pallas_call(
        _in_proj_kernel,
        grid=(nx + 1,),
        in_specs=[
            pl.BlockSpec((tm, D_MODEL), lambda i: (jnp.minimum(i, nx - 1), 0)),
            pl.BlockSpec((tm, D_MODEL), lambda i: (0, 0)),
            pl.BlockSpec((1, D_MODEL), lambda i: (0, 0)),
        ] + [pl.BlockSpec(w.shape, lambda i: (0, 0)) for w in w_parts],
        out_specs=[row_spec(w) for w in _IN_SECTIONS],
        out_shape=[jax.ShapeDtypeStruct((rows, w), dt) for w, dt in zip(_IN_SECTIONS, out_dtypes)],
        compiler_params=pltpu.CompilerParams(
            dimension_semantics=("arbitrary",), vmem_limit_bytes=VMEM_LIMIT),
        name="in_proj",
    )(x_flat, meta_tile, norm_w, *w_parts)


def _block_rows(x, rows, width):
    parts = []
    for r in rows:
        if r is None:
            parts.append(jnp.zeros((width, x.shape[1]), x.dtype))
        else:
            parts.append(jnp.broadcast_to(x[r:r + 1, :], (width, x.shape[1])))
    return jnp.concatenate(parts, axis=0)


N_SEQ_INPUTS = 7
N_MIXER_PARAMS = 9


def _mixer_kernel(*refs, nseq):
    seq_refs = [refs[s * N_SEQ_INPUTS:(s + 1) * N_SEQ_INPUTS] for s in range(nseq)]
    rest = refs[nseq * N_SEQ_INPUTS:]
    param_refs = rest[:N_MIXER_PARAMS]
    out_ref, xext_ref, sstate_ref, gstate_ref = rest[N_MIXER_PARAMS:]
    c = pl.program_id(1)

    @pl.when(c == 0)
    def _():
        xext_ref[:, 0:CONV_CARRY, :] = jnp.zeros((nseq, CONV_CARRY, XBC_WIDTH), BF16)
        sstate_ref[...] = jnp.zeros_like(sstate_ref)
        gstate_ref[...] = jnp.zeros_like(gstate_ref)

    for s in range(nseq):
        _mixer_chunk(c, *seq_refs[s], *param_refs, out_ref.at[s], xext_ref.at[s], sstate_ref.at[s],
                     gstate_ref.at[s])


def _mixer_chunk(c, z_ref, xbc_ref, q_ref, k_ref, v_ref, g_ref, misc_ref,
                 convw_ref, convb_ref, dtb_ref, aneg_ref, dskip_ref, ssdw_ref,
                 wdec_ref, bdec_ref, glaw_ref,
                 out_ref,
                 xext_ref, sstate_ref, gstate_ref):

    row = lax.broadcasted_iota(jnp.int32, (CHUNK, CHUNK), 0)
    col = lax.broadcasted_iota(jnp.int32, (CHUNK, CHUNK), 1)
    causal = row >= col
    tri = jnp.where(causal, 1.0, 0.0).astype(BF16)
    valid = jnp.logical_or(c > 0, row >= FRONT_PAD)

    xcur = xbc_ref[...]
    xext_ref[CONV_CARRY:CONV_CARRY + CHUNK, :] = xcur
    xall = xext_ref[...]
    srow = lax.broadcasted_iota(jnp.int32, (CHUNK, CHUNK + CONV_CARRY), 0)
    scol = lax.broadcasted_iota(jnp.int32, (CHUNK, CHUNK + CONV_CARRY), 1)
    conv = convb_ref[...] + convw_ref[SSD_CONV - 1:SSD_CONV, :] * xcur.astype(F32)
    for kk in range(SSD_CONV - 1):
        shift = jnp.where(scol == srow + CONV_CARRY - (SSD_CONV - 1) + kk, 1.0, 0.0).astype(BF16)
        conv = conv + convw_ref[kk:kk + 1, :] * _dot(shift, xall)
    xext_ref[0:CONV_CARRY, :] = xext_ref[CHUNK:CHUNK + CONV_CARRY, :]
    valid_w = jnp.logical_or(c > 0, lax.broadcasted_iota(jnp.int32, (CHUNK, XBC_WIDTH), 0) >= FRONT_PAD)
    act = jnp.where(valid_w, _silu(conv), 0.0)
    xs = act[:, :SSD_WIDTH]
    bmat = act[:, SSD_WIDTH:SSD_WIDTH + SSD_GROUPS * SSD_STATE].astype(BF16)
    cmat = act[:, SSD_WIDTH + SSD_GROUPS * SSD_STATE:].astype(BF16)

    misc = misc_ref[...]
    dt = jnp.where(valid, jax.nn.softplus(misc + dtb_ref[...]), 0.0)
    d_a = dt * aneg_ref[...]
    p0, p1, p2 = _split3(d_a)
    a_cs = _dot(tri, p0) + _dot(tri, p1) + _dot(tri, p2)
    a_cs_t = a_cs.T

    lo_half = col < SSD_HEAD_DIM
    hg = SSD_HEADS // SSD_GROUPS
    pairs_per_group = hg // 2
    y_parts = []
    for gi in range(SSD_GROUPS):
        b_g = bmat[:, gi * SSD_STATE:(gi + 1) * SSD_STATE]
        c_g = cmat[:, gi * SSD_STATE:(gi + 1) * SSD_STATE]
        c_g32 = c_g.astype(F32)
        cb = _dot_nt(c_g, b_g)
        w_parts = []
        cd_parts = []
        for pj in range(pairs_per_group):
            h0 = gi * hg + 2 * pj
            h1 = h0 + 1
            lanes = slice(h0 * SSD_HEAD_DIM, (h1 + 1) * SSD_HEAD_DIM)
            ab0 = jnp.broadcast_to(a_cs[:, h0:h0 + 1], (CHUNK, CHUNK))
            ab1 = jnp.broadcast_to(a_cs[:, h1:h1 + 1], (CHUNK, CHUNK))
            db0 = jnp.broadcast_to(dt[:, h0:h0 + 1], (CHUNK, CHUNK))
            db1 = jnp.broadcast_to(dt[:, h1:h1 + 1], (CHUNK, CHUNK))
            l0 = jnp.exp(jnp.where(causal, ab0 - a_cs_t[h0:h0 + 1, :], -jnp.inf))
            l1 = jnp.exp(jnp.where(causal, ab1 - a_cs_t[h1:h1 + 1, :], -jnp.inf))
            lhs = jnp.concatenate(
                [(cb * l0).astype(BF16), (c_g32 * jnp.exp(ab0)).astype(BF16),
                 (cb * l1).astype(BF16), (c_g32 * jnp.exp(ab1)).astype(BF16)], axis=1)
            xs_p = xs[:, lanes]
            dtx = xs_p * jnp.where(lo_half, db0, db1)
            st = sstate_ref[gi, :, pj * CHUNK:(pj + 1) * CHUNK]
            rhs = jnp.concatenate(
                [jnp.where(lo_half, dtx, 0.0).astype(BF16), jnp.where(lo_half, st, 0.0).astype(BF16),
                 jnp.where(lo_half, 0.0, dtx).astype(BF16), jnp.where(lo_half, 0.0, st).astype(BF16)],
                axis=0)
            y_p = _dot(lhs, rhs) + dskip_ref[:, lanes] * xs_p
            y_parts.append(y_p)
            ae = jnp.where(lo_half, ab0, ab1)
            a_last = ae[CHUNK - 1:CHUNK, :]
            w_parts.append((dtx * jnp.exp(a_last - ae)).astype(BF16))
            cd_parts.append(jnp.exp(a_last))
        upd = _dot_tn(b_g, jnp.concatenate(w_parts, axis=1))
        sstate_ref[gi] = jnp.concatenate(cd_parts, axis=1) * sstate_ref[gi] + upd
    y = jnp.concatenate(y_parts, axis=1)
    y = y * z_ref[...].astype(F32)
    gsz = SSD_WIDTH // SSD_GROUPS
    for gi in range(SSD_GROUPS):
        yg = y[:, gi * gsz:(gi + 1) * gsz]
        yn = yg * lax.rsqrt(jnp.mean(yg * yg, axis=-1, keepdims=True) + EPS)
        out_ref[:, gi * gsz:(gi + 1) * gsz] = (yn * ssdw_ref[:, gi * gsz:(gi + 1) * gsz]).astype(out_ref.dtype)

    m0, m1 = _split2(misc)
    pre = _dot(m0, wdec_ref[...]) + _dot(m1, wdec_ref[...]) + bdec_ref[...]
    log_a = jax.nn.log_sigmoid(pre) * (1.0 / GLA_TAU)
    g0, g1 = _split2(log_a)
    gcum = _dot(tri, g0) + _dot(tri, g1)
    roww = lax.broadcasted_iota(jnp.int32, (CHUNK, GLA_K_WIDTH), 0)
    qf = q_ref[...].astype(F32) * (GLA_KEY_DIM ** -0.5)
    kf = k_ref[...].astype(F32)
    nsub = CHUNK // GLA_SUB
    g_start = _block_rows(gcum, [None] + [GLA_SUB * i - 1 for i in range(1, nsub)], GLA_SUB)
    e0 = gcum - g_start
    q_lv = [(qf * jnp.exp(e0)).astype(BF16)]
    k_lv = [(kf * jnp.exp(-e0)).astype(BF16)]
    shifts = (4, 5, 6)
    for sh in shifts:
        b = 1 << sh
        bound = _block_rows(gcum, [2 * b * i + b - 1 for i in range(CHUNK // (2 * b))], 2 * b)
        second = ((roww >> sh) & 1) == 1
        x_l = jnp.exp(jnp.where(second, gcum - bound, bound - gcum))
        q_lv.append((qf * x_l).astype(BF16))
        k_lv.append((kf * x_l).astype(BF16))
    masks = [jnp.logical_and((row >> 4) == (col >> 4), causal)]
    for sh in shifts:
        same = (row >> (sh + 1)) == (col >> (sh + 1))
        m = jnp.logical_and(same, jnp.logical_and(((row >> sh) & 1) == 1, ((col >> sh) & 1) == 0))
        masks.append(m)
    g_last = gcum[CHUNK - 1:CHUNK, :]
    q_in = (qf * jnp.exp(gcum)).astype(BF16)
    k_end = (kf * jnp.exp(g_last - gcum)).astype(BF16)
    dec = jnp.exp(g_last)
    vb = v_ref[...]
    for hh in range(GLA_HEADS):
        kl = slice(hh * GLA_KEY_DIM, (hh + 1) * GLA_KEY_DIM)
        vl = slice(hh * GLA_VAL_DIM, (hh + 1) * GLA_VAL_DIM)
        scores = jnp.zeros((CHUNK, CHUNK), F32)
        for lv in range(len(masks)):
            scores = scores + jnp.where(masks[lv], _dot_nt(q_lv[lv][:, kl], k_lv[lv][:, kl]), 0.0)
        v_h = vb[:, vl]
        s_t = gstate_ref[hh]
        o = _dot(scores.astype(BF16), v_h) + _dot_nt(q_in[:, kl], s_t.astype(BF16))
        gstate_ref[hh] = dec[:, kl] * s_t + _dot_tn(v_h, k_end[:, kl])
        o = o * lax.rsqrt(jnp.mean(o * o, axis=-1, keepdims=True) + EPS) * glaw_ref[...]
        o = o * g_ref[:, vl].astype(F32)
        out_ref[:, SSD_WIDTH + hh * GLA_VAL_DIM:SSD_WIDTH + (hh + 1) * GLA_VAL_DIM] = o.astype(out_ref.dtype)


def _mixer(z, xbc, q, k, v, g, misc, params, batch, nchunks):
    widths = (SSD_WIDTH, XBC_WIDTH, GLA_K_WIDTH, GLA_K_WIDTH, GLA_V_WIDTH, GLA_V_WIDTH, MISC_WIDTH)
    assert len(params) == N_MIXER_PARAMS
    nseq = 4 if batch % 4 == 0 else (2 if batch % 2 == 0 else 1)
    meta_block = z.shape[0] // CHUNK - 1

    def row_spec(w, s):
        return pl.BlockSpec(
            (CHUNK, w),
            lambda bp, c: (jnp.where(c == 0, meta_block, (bp * nseq + s) * (nchunks - 1) + c - 1), 0))

    par_spec = lambda p: pl.BlockSpec(p.shape, lambda bp, c: (0,) * p.ndim)
    seq = (nchunks - 1) * CHUNK
    width = SSD_WIDTH + GLA_V_WIDTH
    out = pl.pallas_call(
        functools.partial(_mixer_kernel, nseq=nseq),
        grid=(batch // nseq, nchunks),
        in_specs=[row_spec(w, s) for s in range(nseq) for w in widths] + [par_spec(p) for p in params],
        out_specs=pl.BlockSpec((nseq, CHUNK, width), lambda bp, c: (bp, jnp.maximum(c - 1, 0), 0)),
        out_shape=jax.ShapeDtypeStruct((batch, seq, width), BF16),
        scratch_shapes=[
            pltpu.VMEM((nseq, CHUNK + CONV_CARRY, XBC_WIDTH), BF16),
            pltpu.VMEM((nseq, SSD_GROUPS, SSD_STATE, SSD_WIDTH // SSD_GROUPS), F32),
            pltpu.VMEM((nseq, GLA_HEADS, GLA_VAL_DIM, GLA_KEY_DIM), F32),
        ],
        compiler_params=pltpu.CompilerParams(
            dimension_semantics=("arbitrary", "arbitrary"), vmem_limit_bytes=VMEM_LIMIT),
        name="mixer",
    )(*([z, xbc, q, k, v, g, misc] * nseq), *params)
    return out.reshape(batch * seq, width)


def _post_kernel(x_ref, mixed_ref, wout_ref, fnw_ref, wrt_ref, br_ref,
                 h2_ref, t_ref, idx_ref, gate_ref, lrank_ref, cnt_ref, zero_ref, tprev_ref):
    @pl.when(pl.program_id(0) == 0)
    def _():
        tprev_ref[...] = jnp.zeros_like(tprev_ref)

    t_prev = tprev_ref[...]
    h2 = x_ref[...] + _dot(mixed_ref[...], wout_ref[...])
    h2_ref[...] = h2
    t = h2 * lax.rsqrt(jnp.mean(h2 * h2, axis=-1, keepdims=True) + EPS) * fnw_ref[...]
    t_ref[...] = t.astype(BF16)
    tprev_ref[...] = t

    t0, t1 = _split2(t_prev)
    w0 = wrt_ref[0]
    w1 = wrt_ref[1]
    logits = _dot_nt(w0, t0) + _dot_nt(w0, t1) + _dot_nt(w1, t0) + br_ref[...]
    e_iota = lax.broadcasted_iota(jnp.int32, logits.shape, 0)
    work = logits
    sel_any = jnp.zeros(logits.shape, F32)
    tops = []
    idxs = []
    onehots = []
    for _ in range(TOP_K):
        m = jnp.max(work, axis=0, keepdims=True)
        idx = jnp.min(jnp.where(work == m, e_iota, N_EXPERTS), axis=0, keepdims=True)
        hit = e_iota == idx
        tops.append(m)
        idxs.append(idx)
        onehots.append(hit)
        sel_any = sel_any + jnp.where(hit, 1.0, 0.0)
        work = jnp.where(hit, -jnp.inf, work)
    exps = [jnp.exp(tv - tops[0]) for tv in tops]
    denom = exps[0] + exps[1] + exps[2] + exps[3]
    tile = logits.shape[1]
    pad_rows = PICK_ROWS - TOP_K
    gate_ref[...] = jnp.concatenate([e / denom for e in exps] + [jnp.zeros((pad_rows, tile), F32)], axis=0)
    idx_ref[...] = jnp.concatenate(idxs + [jnp.zeros((pad_rows, tile), jnp.int32)], axis=0)

    r_i = lax.broadcasted_iota(jnp.int32, (tile, tile), 0)
    c_i = lax.broadcasted_iota(jnp.int32, (tile, tile), 1)
    upper = jnp.where(r_i < c_i, 1.0, 0.0).astype(BF16)
    before = _dot(sel_any.astype(BF16), upper)
    ranks = [jnp.sum(jnp.where(h, before, 0.0), axis=0, keepdims=True) for h in onehots]
    ranks.append(jnp.full((pad_rows, tile), float(NO_ROW), F32))
    lrank_ref[...] = jnp.concatenate(ranks, axis=0).astype(jnp.int32)
    cnt = jnp.sum(sel_any, axis=1, keepdims=True)
    cnt_ref[...] = jnp.broadcast_to(cnt, cnt_ref.shape).astype(jnp.int32)
    zero_ref[...] = jnp.zeros_like(zero_ref)


def _post(x_flat, mixed, w_out_b, ffn_norm_w, w_router_t, b_router_col, zero_rows):
    n = x_flat.shape[0]
    tt = TOKEN_TILE
    n_tiles = n // tt
    const = lambda shape: pl.BlockSpec(shape, lambda i: (0,) * len(shape))
    cur = lambda i: jnp.minimum(i, n_tiles - 1)
    prev = lambda i: jnp.maximum(i - 1, 0)
    return pl.pallas_call(
        _post_kernel,
        grid=(n_tiles + 1,),
        in_specs=[
            pl.BlockSpec((tt, D_MODEL), lambda i: (cur(i), 0)),
            pl.BlockSpec((tt, SSD_WIDTH + GLA_V_WIDTH), lambda i: (cur(i), 0)),
            const(w_out_b.shape),
            const((1, D_MODEL)),
            const(w_router_t.shape),
            const((N_EXPERTS, 1)),
        ],
        out_specs=[
            pl.BlockSpec((tt, D_MODEL), lambda i: (cur(i), 0)),
            pl.BlockSpec((tt, D_MODEL), lambda i: (cur(i), 0)),
            pl.BlockSpec((PICK_ROWS, tt), lambda i: (0, prev(i))),
            pl.BlockSpec((PICK_ROWS, tt), lambda i: (0, prev(i))),
            pl.BlockSpec((PICK_ROWS, tt), lambda i: (0, prev(i))),
            pl.BlockSpec((N_EXPERTS, LANES), lambda i: (prev(i), 0)),
            pl.BlockSpec((zero_rows, HALF), lambda i: (i, 0)),
        ],
        out_shape=[
            jax.ShapeDtypeStruct((n, D_MODEL), F32),
            jax.ShapeDtypeStruct((n, D_MODEL), BF16),
            jax.ShapeDtypeStruct((PICK_ROWS, n), jnp.int32),
            jax.ShapeDtypeStruct((PICK_ROWS, n), F32),
            jax.ShapeDtypeStruct((PICK_ROWS, n), jnp.int32),
            jax.ShapeDtypeStruct((n_tiles * N_EXPERTS, LANES), jnp.int32),
            jax.ShapeDtypeStruct(((n_tiles + 1) * zero_rows, HALF), jnp.uint32),
        ],
        scratch_shapes=[pltpu.VMEM((tt, D_MODEL), F32)],
        compiler_params=pltpu.CompilerParams(
            dimension_semantics=("arbitrary",), vmem_limit_bytes=VMEM_LIMIT),
        name="post",
    )(x_flat, mixed, w_out_b, ffn_norm_w, w_router_t, b_router_col)


def _pack_pairs(x):
    lo = pltpu.bitcast(x[:, :HALF], jnp.uint32) >> 16
    hi = pltpu.bitcast(x[:, HALF:], jnp.uint32) & jnp.uint32(0xFFFF0000)
    return lo | hi


def _unpack_pairs(w):
    lo = pltpu.bitcast(w << 16, F32).astype(BF16)
    hi = pltpu.bitcast(w & jnp.uint32(0xFFFF0000), F32).astype(BF16)
    return lo, hi


def _sorted_positions(idx, lrank, offs_ref, base):
    pos = lrank
    for e in range(N_EXPERTS):
        pos = pos + jnp.where(idx == e, offs_ref[base + e], 0)
    return pos


def _group_copies(dst_ref, make_copy):
    return [make_copy(g * SEG_ALIGN, pl.multiple_of(dst_ref[0, 0, g], SEG_ALIGN))
            for g in range(SORT_GROUPS)]


def _dispatch_kernel(offs_ref, idx_ref, lrank_ref, t_ref, dst_ref, dst_prev_ref, buf_in_ref, buf_ref,
                     sorted_ref, sem):
    del buf_in_ref
    i = pl.program_id(0)
    slot = i % 2
    base = i * N_EXPERTS
    tt = t_ref.shape[0]
    pos = _sorted_positions(idx_ref[...], lrank_ref[...], offs_ref, base)
    t = t_ref[...]
    for r0 in range(0, SORT_ROWS, tt):
        prow = lax.broadcasted_iota(jnp.int32, (tt, tt), 0) + r0
        perm = jnp.zeros((tt, tt), F32)
        for kk in range(TOP_K):
            perm = jnp.where(prow == pos[kk:kk + 1, :], 1.0, perm)
        sorted_ref[slot, r0:r0 + tt, :] = _pack_pairs(_dot(perm.astype(BF16), t))

    def copies(table_ref, which):
        def make_copy(local, glob):
            return pltpu.make_async_copy(
                sorted_ref.at[which, pl.ds(local, SEG_ALIGN), :],
                buf_ref.at[pl.ds(glob, SEG_ALIGN), :], sem.at[which])
        return _group_copies(table_ref, make_copy)

    for cp in copies(dst_ref, slot):
        cp.start()

    @pl.when(i > 0)
    def _():
        for cp in copies(dst_prev_ref, 1 - slot):
            cp.wait()

    @pl.when(i == pl.num_programs(0) - 1)
    def _():
        for cp in copies(dst_ref, slot):
            cp.wait()


def _dispatch(offs, group_dst, top_idx, lrank, t_b, buf0):
    n = t_b.shape[0]
    tt = TOKEN_TILE
    table = lambda index: pl.BlockSpec((1, 1, SORT_GROUPS), index, memory_space=pltpu.SMEM)
    grid_spec = pltpu.PrefetchScalarGridSpec(
        num_scalar_prefetch=1,
        grid=(n // tt,),
        in_specs=[
            pl.BlockSpec((PICK_ROWS, tt), lambda i, *_: (0, i)),
            pl.BlockSpec((PICK_ROWS, tt), lambda i, *_: (0, i)),
            pl.BlockSpec((tt, D_MODEL), lambda i, *_: (i, 0)),
            table(lambda i, *_: (i, 0, 0)),
            table(lambda i, *_: (jnp.maximum(i - 1, 0), 0, 0)),
            pl.BlockSpec(memory_space=pl.ANY),
        ],
        out_specs=pl.BlockSpec(memory_space=pl.ANY),
        scratch_shapes=[pltpu.VMEM((2, SORT_ROWS, HALF), jnp.uint32), pltpu.SemaphoreType.DMA((2,))],
    )
    return pl.pallas_call(
        _dispatch_kernel,
        grid_spec=grid_spec,
        out_shape=jax.ShapeDtypeStruct(buf0.shape, jnp.uint32),
        input_output_aliases={6: 0},
        compiler_params=pltpu.CompilerParams(
            dimension_semantics=("arbitrary",), vmem_limit_bytes=VMEM_LIMIT),
        name="dispatch",
    )(offs, top_idx, lrank, t_b, group_dst, group_dst, buf0)


def _expert_kernel(be_ref, nused_ref, xp_ref, wgu_ref, bgu_ref, wd_ref, bd_ref, y_ref,
                   wgu_b_ref, wd_b_ref):
    i = pl.program_id(0)
    used = i < nused_ref[0]

    @pl.when(jnp.logical_not(used))
    def _():
        y_ref[...] = jnp.zeros_like(y_ref)

    new_expert = jnp.logical_or(i == 0, be_ref[i] != be_ref[jnp.maximum(i - 1, 0)])

    @pl.when(jnp.logical_and(used, new_expert))
    def _():
        rows = CHUNK
        for r0 in range(0, D_MODEL, rows):
            wgu_b_ref[r0:r0 + rows, :] = wgu_ref[r0:r0 + rows, :].astype(BF16)
        for r0 in range(0, D_FF, rows):
            wd_b_ref[r0:r0 + rows, :] = wd_ref[r0:r0 + rows, :].astype(BF16)

    @pl.when(used)
    def _():
        x_lo, x_hi = _unpack_pairs(xp_ref[...])
        hgu = _dot(x_lo, wgu_b_ref[:HALF, :]) + _dot(x_hi, wgu_b_ref[HALF:, :]) + bgu_ref[...]
        gate = jnp.minimum(hgu[:, :D_FF], SWIGLU_LIMIT)
        up = jnp.clip(hgu[:, D_FF:], -SWIGLU_LIMIT, SWIGLU_LIMIT)
        act = gate * jax.nn.sigmoid(SWIGLU_ALPHA * gate)
        y = _dot(((up + 1.0) * act).astype(BF16), wd_b_ref[...]) + bd_ref[...]
        y_ref[...] = _pack_pairs(y.astype(BF16).astype(F32))


def _experts(block_expert, n_used, buf, w_gu, b_gu, w_d, b_d):
    n_blocks = block_expert.shape[0]
    rows = n_blocks * MOE_BLOCK
    row_map = lambda i, be, nu: (jnp.minimum(i, nu[0] - 1), 0)
    w_map = lambda i, be, nu: (be[jnp.minimum(i, nu[0] - 1)], 0, 0)
    grid_spec = pltpu.PrefetchScalarGridSpec(
        num_scalar_prefetch=2,
        grid=(n_blocks,),
        in_specs=[
            pl.BlockSpec((MOE_BLOCK, HALF), row_map),
            pl.BlockSpec((None, D_MODEL, 2 * D_FF), w_map),
            pl.BlockSpec((None, 1, 2 * D_FF), w_map),
            pl.BlockSpec((None, D_FF, D_MODEL), w_map),
            pl.BlockSpec((None, 1, D_MODEL), w_map),
        ],
        out_specs=pl.BlockSpec((MOE_BLOCK, HALF), lambda i, be, nu: (i, 0)),
        scratch_shapes=[pltpu.VMEM((D_MODEL, 2 * D_FF), BF16), pltpu.VMEM((D_FF, D_MODEL), BF16)],
    )
    return pl.pallas_call(
        _expert_kernel,
        grid_spec=grid_spec,
        out_shape=jax.ShapeDtypeStruct((rows, HALF), jnp.uint32),
        compiler_params=pltpu.CompilerParams(
            dimension_semantics=("arbitrary",), vmem_limit_bytes=VMEM_LIMIT),
        name="experts",
    )(block_expert, n_used, buf, w_gu, b_gu, w_d, b_d)


def _combine_kernel(offs_ref, idx_ref, lrank_ref, gate_ref, h2_ref, fw_ref, dst_ref, dst_next_ref,
                    y_hbm_ref, out_ref, ys_ref, sem):
    i = pl.program_id(0)
    slot = i % 2
    base = i * N_EXPERTS
    tt = h2_ref.shape[0]

    def copies(table_ref, which):
        def make_copy(local, glob):
            return pltpu.make_async_copy(
                y_hbm_ref.at[pl.ds(glob, SEG_ALIGN), :],
                ys_ref.at[which, pl.ds(local, SEG_ALIGN), :], sem.at[which])
        return _group_copies(table_ref, make_copy)

    @pl.when(i == 0)
    def _():
        for cp in copies(dst_ref, slot):
            cp.start()

    @pl.when(i < pl.num_programs(0) - 1)
    def _():
        for cp in copies(dst_next_ref, 1 - slot):
            cp.start()

    pos = _sorted_positions(idx_ref[...], lrank_ref[...], offs_ref, base).astype(F32)
    zpad = jnp.zeros((LANES - PICK_ROWS, tt), F32)
    pos_c = jnp.concatenate([pos, zpad], axis=0).T
    gate_c = jnp.concatenate([gate_ref[...], zpad], axis=0).T
    lane = lax.broadcasted_iota(jnp.int32, (tt, SORT_ROWS), 1).astype(F32)
    pg = jnp.zeros((tt, SORT_ROWS), F32)
    for kk in range(TOP_K):
        pg = jnp.where(lane == pos_c[:, kk:kk + 1], gate_c[:, kk:kk + 1], pg)
    p_hi, p_lo = _split2(pg)

    for cp in copies(dst_ref, slot):
        cp.wait()
    y_lo, y_hi = _unpack_pairs(ys_ref[slot])
    ffn = jnp.concatenate(
        [_dot(p_hi, y_lo) + _dot(p_lo, y_lo), _dot(p_hi, y_hi) + _dot(p_lo, y_hi)], axis=1)
    acc = h2_ref[...] + ffn
    out_ref[...] = acc * lax.rsqrt(jnp.mean(acc * acc, axis=-1, keepdims=True) + EPS) * fw_ref[...]


def _combine(offs, group_dst, top_idx, lrank, gates, h2, final_norm_w, y_buf):
    n = h2.shape[0]
    tt = TOKEN_TILE
    grid_spec = pltpu.PrefetchScalarGridSpec(
        num_scalar_prefetch=1,
        grid=(n // tt,),
        in_specs=[
            pl.BlockSpec((PICK_ROWS, tt), lambda i, *_: (0, i)),
            pl.BlockSpec((PICK_ROWS, tt), lambda i, *_: (0, i)),
            pl.BlockSpec((PICK_ROWS, tt), lambda i, *_: (0, i)),
            pl.BlockSpec((tt, D_MODEL), lambda i, *_: (i, 0)),
            pl.BlockSpec((1, D_MODEL), lambda i, *_: (0, 0)),
            pl.BlockSpec((1, 1, SORT_GROUPS), lambda i, *_: (i, 0, 0), memory_space=pltpu.SMEM),
            pl.BlockSpec((1, 1, SORT_GROUPS), lambda i, *_: (jnp.minimum(i + 1, n // tt - 1), 0, 0),
                         memory_space=pltpu.SMEM),
            pl.BlockSpec(memory_space=pl.ANY),
        ],
        out_specs=pl.BlockSpec((tt, D_MODEL), lambda i, *_: (i, 0)),
        scratch_shapes=[pltpu.VMEM((2, SORT_ROWS, HALF), jnp.uint32), pltpu.SemaphoreType.DMA((2,))],
    )
    return pl.pallas_call(
        _combine_kernel,
        grid_spec=grid_spec,
        out_shape=jax.ShapeDtypeStruct((n, D_MODEL), F32),
        compiler_params=pltpu.CompilerParams(
            dimension_semantics=("arbitrary",), vmem_limit_bytes=VMEM_LIMIT),
        name="combine",
    )(offs, top_idx, lrank, gates, h2, final_norm_w, group_dst, group_dst, y_buf)


def _pad_lanes(v, width):
    return jnp.pad(v, ((0, 0), (0, width - v.shape[1])))


def kernel(x, meta_tokens, mix_norm_w, w_in, conv_w, conv_b, dt_bias, a_log, d_skip, ssd_norm_w,
           w_decay_up, b_decay, gla_norm_w, w_out, ffn_norm_w, w_router, b_router, w_gate_up,
           b_gate_up, w_down, b_down, final_norm_w):
    batch, seq, d = x.shape
    assert d == D_MODEL and seq % TOKEN_TILE == 0
    assert mix_norm_w.shape[0] == 1, "single-layer block"
    nchunks = (FRONT_PAD + N_META + seq) // CHUNK

    n = batch * seq
    x_flat = x.reshape(n, D_MODEL)
    meta_tile = jnp.concatenate(
        [jnp.zeros((TOKEN_TILE - N_META, D_MODEL), x.dtype), meta_tokens.astype(x.dtype)], axis=0)

    wi = w_in[0]
    o_z, o_xbc = 0, SSD_WIDTH
    o_dt = o_xbc + XBC_WIDTH
    o_q = o_dt + SSD_HEADS
    o_k = o_q + GLA_K_WIDTH
    o_v = o_k + GLA_K_WIDTH
    o_g = o_v + GLA_V_WIDTH
    o_a = o_g + GLA_V_WIDTH
    w_misc = jnp.concatenate(
        [wi[:, o_dt:o_dt + SSD_HEADS], wi[:, o_a:o_a + GLA_RANK],
         jnp.zeros((D_MODEL, MISC_WIDTH - SSD_HEADS - GLA_RANK), wi.dtype)], axis=1)
    w_parts = (wi[:, o_z:o_dt].astype(BF16), wi[:, o_q:o_a].astype(BF16), w_misc.astype(BF16))
    z, xbc, q, k, v, g, misc = _in_proj(x_flat, meta_tile, mix_norm_w[0][None, :], w_parts)

    dtb = _pad_lanes(dt_bias[0][None, :].astype(F32), MISC_WIDTH)
    aneg = _pad_lanes(-jnp.exp(a_log[0].astype(F32))[None, :], MISC_WIDTH)
    dskip = jnp.repeat(d_skip[0].astype(F32), SSD_HEAD_DIM)[None, :]
    wdec = jnp.zeros((MISC_WIDTH, GLA_K_WIDTH), F32).at[SSD_HEADS:SSD_HEADS + GLA_RANK].set(w_decay_up[0])
    params = (conv_w[0], conv_b[0][None, :], dtb, aneg, dskip, ssd_norm_w[0][None, :],
              wdec.astype(BF16), b_decay[0][None, :], gla_norm_w[0][None, :])
    mixed = _mixer(z, xbc, q, k, v, g, misc, params, batch, nchunks)

    wr_hi, wr_lo = _split2(w_router[0].T.astype(F32))
    n_tiles = n // TOKEN_TILE
    n_blocks = -(-(n * TOP_K + n_tiles * N_EXPERTS * (SEG_ALIGN - 1)) // MOE_BLOCK) + N_EXPERTS
    spare = n_blocks * MOE_BLOCK
    total_rows = spare + 2 * SORT_ROWS
    zero_rows = -(-total_rows // ((n_tiles + 1) * SEG_ALIGN)) * SEG_ALIGN
    h2, t_b, top_idx, gates, lrank, tile_cnt, buf0 = _post(
        x_flat, mixed, w_out[0].astype(BF16), ffn_norm_w[0][None, :],
        jnp.stack([wr_hi, wr_lo]), b_router[0][:, None], zero_rows)

    tile_cnt = tile_cnt.reshape(n_tiles, N_EXPERTS, LANES)[:, :, 0]
    seg_rows = (tile_cnt + SEG_ALIGN - 1) // SEG_ALIGN * SEG_ALIGN
    counts = jnp.sum(seg_rows, axis=0)
    padded = (counts + MOE_BLOCK - 1) // MOE_BLOCK * MOE_BLOCK
    pend = jnp.cumsum(padded)
    pstart = pend - padded
    dstart = pstart[None, :] + jnp.cumsum(seg_rows, axis=0) - seg_rows
    seg_end = jnp.cumsum(seg_rows, axis=1)
    offs = seg_end - seg_rows
    block_pos = jnp.arange(total_rows // MOE_BLOCK, dtype=jnp.int32) * MOE_BLOCK
    block_expert = jnp.minimum(
        jnp.sum((pend[None, :] <= block_pos[:, None]).astype(jnp.int32), axis=1), N_EXPERTS - 1)
    n_used = (pend[-1:] // MOE_BLOCK).astype(jnp.int32)
    grow = jnp.arange(SORT_GROUPS, dtype=jnp.int32)[None, :, None] * SEG_ALIGN
    inside = (offs[:, None, :] <= grow) & (grow < seg_end[:, None, :])
    group_dst = jnp.sum(jnp.where(inside, dstart[:, None, :] + grow - offs[:, None, :], 0), axis=2)
    parity = (jnp.arange(n_tiles, dtype=jnp.int32) % 2)[:, None]
    group_dst = jnp.where(jnp.any(inside, axis=2), group_dst, spare + parity * SORT_ROWS + grow[:, :, 0])
    group_dst = group_dst.astype(jnp.int32)[:, None, :]
    offs = offs.reshape(-1).astype(jnp.int32)

    buf = _dispatch(offs, group_dst, top_idx, lrank, t_b, buf0)
    y_buf = _experts(block_expert, n_used, buf, w_gate_up[0], b_gate_up[0][:, None, :],
                     w_down[0], b_down[0][:, None, :])
    out = _combine(offs, group_dst, top_idx, lrank, gates, h2, final_norm_w[None, :], y_buf)
    return out.reshape(batch, seq, D_MODEL)
```

```python
import functools

import jax
import jax.numpy as jnp
from jax import lax
from jax.experimental import pallas as pl
from jax.experimental.pallas import tpu as pltpu

F32 = jnp.float32
BF16 = jnp.bfloat16

LANES = 128
SUBLANES = 8

D_MODEL = 1024
N_META = 16
EPS = 1e-5
SSD_HEAD_DIM = 64
SSD_HEADS = 16
SSD_GROUPS = 2
SSD_STATE = 128
SSD_CONV = 4
SSD_WIDTH = SSD_HEADS * SSD_HEAD_DIM
XBC_WIDTH = SSD_WIDTH + 2 * SSD_GROUPS * SSD_STATE
GLA_HEADS = 4
GLA_KEY_DIM = 128
GLA_VAL_DIM = 256
GLA_K_WIDTH = GLA_HEADS * GLA_KEY_DIM
GLA_V_WIDTH = GLA_HEADS * GLA_VAL_DIM
GLA_RANK = 16
GLA_TAU = 16.0
GLA_SUB = 16
N_EXPERTS = 32
TOP_K = 4
D_FF = D_MODEL
SWIGLU_LIMIT = 7.0
SWIGLU_ALPHA = 1.702
MOE_BLOCK = 512

CHUNK = 128
FRONT_PAD = CHUNK - N_META
MISC_WIDTH = LANES
TOKEN_TILE = 256
HALF = D_MODEL // 2
NO_ROW = 1 << 12
SEG_ALIGN = SUBLANES
PICK_ROWS = SUBLANES
CONV_CARRY = SUBLANES
SORT_ROWS = -(-(TOP_K * TOKEN_TILE + N_EXPERTS * (SEG_ALIGN - 1)) // TOKEN_TILE) * TOKEN_TILE
SORT_GROUPS = SORT_ROWS // SEG_ALIGN
VMEM_LIMIT = 56 * 1024 * 1024


def _split2(x):
    hi = x.astype(BF16)
    lo = (x - hi.astype(F32)).astype(BF16)
    return hi, lo


def _split3(x):
    hi = x.astype(BF16)
    r = x - hi.astype(F32)
    mid = r.astype(BF16)
    lo = (r - mid.astype(F32)).astype(BF16)
    return hi, mid, lo


def _dot(a, b):
    return jnp.dot(a, b, preferred_element_type=F32)


def _dot_nt(a, b):
    return lax.dot_general(a, b, (((1,), (1,)), ((), ())), preferred_element_type=F32)


def _dot_tn(a, b):
    return lax.dot_general(a, b, (((0,), (0,)), ((), ())), preferred_element_type=F32)


def _silu(x):
    return x * jax.nn.sigmoid(x)


_IN_SECTIONS = (SSD_WIDTH, XBC_WIDTH, GLA_K_WIDTH, GLA_K_WIDTH, GLA_V_WIDTH, GLA_V_WIDTH, MISC_WIDTH)
_GATE_SECTIONS = (0, 5)


_SECTION_SOURCE = ((0, 0), (0, SSD_WIDTH),
                   (1, 0), (1, GLA_K_WIDTH), (1, 2 * GLA_K_WIDTH), (1, 2 * GLA_K_WIDTH + GLA_V_WIDTH),
                   (2, 0))


def _in_proj_kernel(x_ref, meta_ref, nw_ref, wa_ref, wb_ref, wc_ref,
                    z_ref, xbc_ref, q_ref, k_ref, v_ref, g_ref, misc_ref):
    h = jnp.where(pl.program_id(0) < pl.num_programs(0) - 1, x_ref[...], meta_ref[...])
    u = h * lax.rsqrt(jnp.mean(h * h, axis=-1, keepdims=True) + EPS) * nw_ref[...]
    ub = u.astype(BF16)
    outs = (z_ref, xbc_ref, q_ref, k_ref, v_ref, g_ref, misc_ref)
    w_refs = (wa_ref, wb_ref, wc_ref)
    for section, (o_ref, width) in enumerate(zip(outs, _IN_SECTIONS)):
        src, off = _SECTION_SOURCE[section]
        p = _dot(ub, w_refs[src][:, off:off + width])
        if section in _GATE_SECTIONS:
            p = _silu(p)
        o_ref[...] = p.astype(o_ref.dtype)


def _in_proj(x_flat, meta_tile, norm_w, w_parts):
    tm = TOKEN_TILE
    nx = x_flat.shape[0] // tm
    rows = (nx + 1) * tm
    row_spec = lambda w: pl.BlockSpec((tm, w), lambda i: (i, 0))
    out_dtypes = (BF16,) * 6 + (F32,)
    return pl.pallas_call(
        _in_proj_kernel,
        grid=(nx + 1,),
        in_specs=[
            pl.BlockSpec((tm, D_MODEL), lambda i: (jnp.minimum(i, nx - 1), 0)),
            pl.BlockSpec((tm, D_MODEL), lambda i: (0, 0)),
            pl.BlockSpec((1, D_MODEL), lambda i: (0, 0)),
            pl.BlockSpec((D_MODEL, SSD_WIDTH + XBC_WIDTH), lambda i: (0, 0)),
        ] + [pl.BlockSpec(w.shape, lambda i: (0, 0)) for w in w_parts[1:]],
        out_specs=[row_spec(w) for w in _IN_SECTIONS],
        out_shape=[jax.ShapeDtypeStruct((rows, w), dt) for w, dt in zip(_IN_SECTIONS, out_dtypes)],
        compiler_params=pltpu.CompilerParams(
            dimension_semantics=("arbitrary",), vmem_limit_bytes=VMEM_LIMIT),
        name="in_proj",
    )(x_flat, meta_tile, norm_w, *w_parts)


def _block_rows(x, rows, width):
    parts = []
    for r in rows:
        if r is None:
            parts.append(jnp.zeros((width, x.shape[1]), x.dtype))
        else:
            parts.append(jnp.broadcast_to(x[r:r + 1, :], (width, x.shape[1])))
    return jnp.concatenate(parts, axis=0)


N_SEQ_INPUTS = 7
N_MIXER_PARAMS = 9


def _mixer_kernel(*refs, nseq):
    seq_refs = [refs[s * N_SEQ_INPUTS:(s + 1) * N_SEQ_INPUTS] for s in range(nseq)]
    rest = refs[nseq * N_SEQ_INPUTS:]
    param_refs = rest[:N_MIXER_PARAMS]
    out_ref, xext_ref, sstate_ref, gstate_ref = rest[N_MIXER_PARAMS:]
    c = pl.program_id(1)

    @pl.when(c == 0)
    def _():
        xext_ref[:, 0:CONV_CARRY, :] = jnp.zeros((nseq, CONV_CARRY, XBC_WIDTH), BF16)
        sstate_ref[...] = jnp.zeros_like(sstate_ref)
        gstate_ref[...] = jnp.zeros_like(gstate_ref)

    for s in range(nseq):
        _mixer_chunk(c, *seq_refs[s], *param_refs, out_ref.at[s], xext_ref.at[s], sstate_ref.at[s],
                     gstate_ref.at[s])


def _mixer_chunk(c, z_ref, xbc_ref, q_ref, k_ref, v_ref, g_ref, misc_ref,
                 convw_ref, convb_ref, dtb_ref, aneg_ref, dskip_ref, ssdw_ref,
                 wdec_ref, bdec_ref, glaw_ref,
                 out_ref,
                 xext_ref, sstate_ref, gstate_ref):

    row = lax.broadcasted_iota(jnp.int32, (CHUNK, CHUNK), 0)
    col = lax.broadcasted_iota(jnp.int32, (CHUNK, CHUNK), 1)
    causal = row >= col
    tri = jnp.where(causal, 1.0, 0.0).astype(BF16)
    valid = jnp.logical_or(c > 0, row >= FRONT_PAD)

    xcur = xbc_ref[...]
    xext_ref[CONV_CARRY:CONV_CARRY + CHUNK, :] = xcur
    xall = xext_ref[...]
    srow = lax.broadcasted_iota(jnp.int32, (CHUNK, CHUNK + CONV_CARRY), 0)
    scol = lax.broadcasted_iota(jnp.int32, (CHUNK, CHUNK + CONV_CARRY), 1)
    conv = convb_ref[...] + convw_ref[SSD_CONV - 1:SSD_CONV, :] * xcur.astype(F32)
    for kk in range(SSD_CONV - 1):
        shift = jnp.where(scol == srow + CONV_CARRY - (SSD_CONV - 1) + kk, 1.0, 0.0).astype(BF16)
        conv = conv + convw_ref[kk:kk + 1, :] * _dot(shift, xall)
    xext_ref[0:CONV_CARRY, :] = xext_ref[CHUNK:CHUNK + CONV_CARRY, :]
    valid_w = jnp.logical_or(c > 0, lax.broadcasted_iota(jnp.int32, (CHUNK, XBC_WIDTH), 0) >= FRONT_PAD)
    act = jnp.where(valid_w, _silu(conv), 0.0)
    xs = act[:, :SSD_WIDTH]
    bmat = act[:, SSD_WIDTH:SSD_WIDTH + SSD_GROUPS * SSD_STATE].astype(BF16)
    cmat = act[:, SSD_WIDTH + SSD_GROUPS * SSD_STATE:].astype(BF16)

    misc = misc_ref[...]
    dt = jnp.where(valid, jax.nn.softplus(misc + dtb_ref[...]), 0.0)
    d_a = dt * aneg_ref[...]
    p0, p1, p2 = _split3(d_a)
    a_cs = _dot(tri, p0) + _dot(tri, p1) + _dot(tri, p2)
    a_cs_t = a_cs.T

    lo_half = col < SSD_HEAD_DIM
    hg = SSD_HEADS // SSD_GROUPS
    pairs_per_group = hg // 2
    y_parts = []
    for gi in range(SSD_GROUPS):
        b_g = bmat[:, gi * SSD_STATE:(gi + 1) * SSD_STATE]
        c_g = cmat[:, gi * SSD_STATE:(gi + 1) * SSD_STATE]
        c_g32 = c_g.astype(F32)
        cb = _dot_nt(c_g, b_g)
        w_parts = []
        cd_parts = []
        for pj in range(pairs_per_group):
            h0 = gi * hg + 2 * pj
            h1 = h0 + 1
            lanes = slice(h0 * SSD_HEAD_DIM, (h1 + 1) * SSD_HEAD_DIM)
            ab0 = jnp.broadcast_to(a_cs[:, h0:h0 + 1], (CHUNK, CHUNK))
            ab1 = jnp.broadcast_to(a_cs[:, h1:h1 + 1], (CHUNK, CHUNK))
            db0 = jnp.broadcast_to(dt[:, h0:h0 + 1], (CHUNK, CHUNK))
            db1 = jnp.broadcast_to(dt[:, h1:h1 + 1], (CHUNK, CHUNK))
            l0 = jnp.exp(jnp.where(causal, ab0 - a_cs_t[h0:h0 + 1, :], -jnp.inf))
            l1 = jnp.exp(jnp.where(causal, ab1 - a_cs_t[h1:h1 + 1, :], -jnp.inf))
            lhs = jnp.concatenate(
                [(cb * l0).astype(BF16), (c_g32 * jnp.exp(ab0)).astype(BF16),
                 (cb * l1).astype(BF16), (c_g32 * jnp.exp(ab1)).astype(BF16)], axis=1)
            xs_p = xs[:, lanes]
            dtx = xs_p * jnp.where(lo_half, db0, db1)
            st = sstate_ref[gi, :, pj * CHUNK:(pj + 1) * CHUNK]
            rhs = jnp.concatenate(
                [jnp.where(lo_half, dtx, 0.0).astype(BF16), jnp.where(lo_half, st, 0.0).astype(BF16),
                 jnp.where(lo_half, 0.0, dtx).astype(BF16), jnp.where(lo_half, 0.0, st).astype(BF16)],
                axis=0)
            y_p = _dot(lhs, rhs) + dskip_ref[:, lanes] * xs_p
            y_parts.append(y_p)
            ae = jnp.where(lo_half, ab0, ab1)
            a_last = ae[CHUNK - 1:CHUNK, :]
            w_parts.append((dtx * jnp.exp(a_last - ae)).astype(BF16))
            cd_parts.append(jnp.exp(a_last))
        upd = _dot_tn(b_g, jnp.concatenate(w_parts, axis=1))
        sstate_ref[gi] = jnp.concatenate(cd_parts, axis=1) * sstate_ref[gi] + upd
    y = jnp.concatenate(y_parts, axis=1)
    y = y * z_ref[...].astype(F32)
    gsz = SSD_WIDTH // SSD_GROUPS
    for gi in range(SSD_GROUPS):
        yg = y[:, gi * gsz:(gi + 1) * gsz]
        yn = yg * lax.rsqrt(jnp.mean(yg * yg, axis=-1, keepdims=True) + EPS)
        out_ref[:, gi * gsz:(gi + 1) * gsz] = (yn * ssdw_ref[:, gi * gsz:(gi + 1) * gsz]).astype(out_ref.dtype)

    m0, m1 = _split2(misc)
    pre = _dot(m0, wdec_ref[...]) + _dot(m1, wdec_ref[...]) + bdec_ref[...]
    log_a = jax.nn.log_sigmoid(pre) * (1.0 / GLA_TAU)
    g0, g1 = _split2(log_a)
    gcum = _dot(tri, g0) + _dot(tri, g1)
    roww = lax.broadcasted_iota(jnp.int32, (CHUNK, GLA_K_WIDTH), 0)
    qf = q_ref[...].astype(F32) * (GLA_KEY_DIM ** -0.5)
    kf = k_ref[...].astype(F32)
    nsub = CHUNK // GLA_SUB
    g_start = _block_rows(gcum, [None] + [GLA_SUB * i - 1 for i in range(1, nsub)], GLA_SUB)
    e0 = gcum - g_start
    q_lv = [(qf * jnp.exp(e0)).astype(BF16)]
    k_lv = [(kf * jnp.exp(-e0)).astype(BF16)]
    shifts = (4, 5, 6)
    for sh in shifts:
        b = 1 << sh
        bound = _block_rows(gcum, [2 * b * i + b - 1 for i in range(CHUNK // (2 * b))], 2 * b)
        second = ((roww >> sh) & 1) == 1
        x_l = jnp.exp(jnp.where(second, gcum - bound, bound - gcum))
        q_lv.append((qf * x_l).astype(BF16))
        k_lv.append((kf * x_l).astype(BF16))
    masks = [jnp.logical_and((row >> 4) == (col >> 4), causal)]
    for sh in shifts:
        same = (row >> (sh + 1)) == (col >> (sh + 1))
        m = jnp.logical_and(same, jnp.logical_and(((row >> sh) & 1) == 1, ((col >> sh) & 1) == 0))
        masks.append(m)
    g_last = gcum[CHUNK - 1:CHUNK, :]
    q_in = (qf * jnp.exp(gcum)).astype(BF16)
    k_end = (kf * jnp.exp(g_last - gcum)).astype(BF16)
    dec = jnp.exp(g_last)
    vb = v_ref[...]
    for hh in range(GLA_HEADS):
        kl = slice(hh * GLA_KEY_DIM, (hh + 1) * GLA_KEY_DIM)
        vl = slice(hh * GLA_VAL_DIM, (hh + 1) * GLA_VAL_DIM)
        scores = jnp.zeros((CHUNK, CHUNK), F32)
        for lv in range(len(masks)):
            scores = scores + jnp.where(masks[lv], _dot_nt(q_lv[lv][:, kl], k_lv[lv][:, kl]), 0.0)
        v_h = vb[:, vl]
        s_t = gstate_ref[hh]
        o = _dot(scores.astype(BF16), v_h) + _dot_nt(q_in[:, kl], s_t.astype(BF16))
        gstate_ref[hh] = dec[:, kl] * s_t + _dot_tn(v_h, k_end[:, kl])
        o = o * lax.rsqrt(jnp.mean(o * o, axis=-1, keepdims=True) + EPS) * glaw_ref[...]
        o = o * g_ref[:, vl].astype(F32)
        out_ref[:, SSD_WIDTH + hh * GLA_VAL_DIM:SSD_WIDTH + (hh + 1) * GLA_VAL_DIM] = o.astype(out_ref.dtype)


def _mixer(z, xbc, q, k, v, g, misc, params, batch, nchunks):
    widths = (SSD_WIDTH, XBC_WIDTH, GLA_K_WIDTH, GLA_K_WIDTH, GLA_V_WIDTH, GLA_V_WIDTH, MISC_WIDTH)
    assert len(params) == N_MIXER_PARAMS
    nseq = 4 if batch % 4 == 0 else (2 if batch % 2 == 0 else 1)
    meta_block = z.shape[0] // CHUNK - 1

    def row_spec(w, s):
        return pl.BlockSpec(
            (CHUNK, w),
            lambda bp, c: (jnp.where(c == 0, meta_block, (bp * nseq + s) * (nchunks - 1) + c - 1), 0))

    par_spec = lambda p: pl.BlockSpec(p.shape, lambda bp, c: (0,) * p.ndim)
    seq = (nchunks - 1) * CHUNK
    width = SSD_WIDTH + GLA_V_WIDTH
    out = pl.pallas_call(
        functools.partial(_mixer_kernel, nseq=nseq),
        grid=(batch // nseq, nchunks),
        in_specs=[row_spec(w, s) for s in range(nseq) for w in widths] + [par_spec(p) for p in params],
        out_specs=pl.BlockSpec((nseq, CHUNK, width), lambda bp, c: (bp, jnp.maximum(c - 1, 0), 0)),
        out_shape=jax.ShapeDtypeStruct((batch, seq, width), BF16),
        scratch_shapes=[
            pltpu.VMEM((nseq, CHUNK + CONV_CARRY, XBC_WIDTH), BF16),
            pltpu.VMEM((nseq, SSD_GROUPS, SSD_STATE, SSD_WIDTH // SSD_GROUPS), F32),
            pltpu.VMEM((nseq, GLA_HEADS, GLA_VAL_DIM, GLA_KEY_DIM), F32),
        ],
        compiler_params=pltpu.CompilerParams(
            dimension_semantics=("arbitrary", "arbitrary"), vmem_limit_bytes=VMEM_LIMIT),
        name="mixer",
    )(*([z, xbc, q, k, v, g, misc] * nseq), *params)
    return out.reshape(batch * seq, width)


def _post_kernel(x_ref, mixed_ref, wout_ref, fnw_ref, wrt_ref, br_ref,
                 h2_ref, t_ref, idx_ref, gate_ref, lrank_ref, cnt_ref, zero_ref, tprev_ref):
    @pl.when(pl.program_id(0) == 0)
    def _():
        tprev_ref[...] = jnp.zeros_like(tprev_ref)

    t_prev = tprev_ref[...]
    h2 = x_ref[...] + _dot(mixed_ref[...], wout_ref[...])
    h2_ref[...] = h2
    t = h2 * lax.rsqrt(jnp.mean(h2 * h2, axis=-1, keepdims=True) + EPS) * fnw_ref[...]
    t_ref[...] = t.astype(BF16)
    tprev_ref[...] = t

    t0, t1 = _split2(t_prev)
    w0 = wrt_ref[0]
    w1 = wrt_ref[1]
    logits = _dot_nt(w0, t0) + _dot_nt(w0, t1) + _dot_nt(w1, t0) + br_ref[...]
    e_iota = lax.broadcasted_iota(jnp.int32, logits.shape, 0)
    work = logits
    sel_any = jnp.zeros(logits.shape, F32)
    tops = []
    idxs = []
    onehots = []
    for _ in range(TOP_K):
        m = jnp.max(work, axis=0, keepdims=True)
        idx = jnp.min(jnp.where(work == m, e_iota, N_EXPERTS), axis=0, keepdims=True)
        hit = e_iota == idx
        tops.append(m)
        idxs.append(idx)
        onehots.append(hit)
        sel_any = sel_any + jnp.where(hit, 1.0, 0.0)
        work = jnp.where(hit, -jnp.inf, work)
    exps = [jnp.exp(tv - tops[0]) for tv in tops]
    denom = exps[0] + exps[1] + exps[2] + exps[3]
    tile = logits.shape[1]
    pad_rows = PICK_ROWS - TOP_K
    gate_ref[...] = jnp.concatenate([e / denom for e in exps] + [jnp.zeros((pad_rows, tile), F32)], axis=0)
    idx_ref[...] = jnp.concatenate(idxs + [jnp.zeros((pad_rows, tile), jnp.int32)], axis=0)

    r_i = lax.broadcasted_iota(jnp.int32, (tile, tile), 0)
    c_i = lax.broadcasted_iota(jnp.int32, (tile, tile), 1)
    upper = jnp.where(r_i < c_i, 1.0, 0.0).astype(BF16)
    before = _dot(sel_any.astype(BF16), upper)
    ranks = [jnp.sum(jnp.where(h, before, 0.0), axis=0, keepdims=True) for h in onehots]
    ranks.append(jnp.full((pad_rows, tile), float(NO_ROW), F32))
    lrank_ref[...] = jnp.concatenate(ranks, axis=0).astype(jnp.int32)
    cnt = jnp.sum(sel_any, axis=1, keepdims=True)
    cnt_ref[...] = jnp.broadcast_to(cnt, cnt_ref.shape).astype(jnp.int32)
    zero_ref[...] = jnp.zeros_like(zero_ref)


def _post(x_flat, mixed, w_out_b, ffn_norm_w, w_router_t, b_router_col, zero_rows):
    n = x_flat.shape[0]
    tt = TOKEN_TILE
    n_tiles = n // tt
    const = lambda shape: pl.BlockSpec(shape, lambda i: (0,) * len(shape))
    cur = lambda i: jnp.minimum(i, n_tiles - 1)
    prev = lambda i: jnp.maximum(i - 1, 0)
    return pl.pallas_call(
        _post_kernel,
        grid=(n_tiles + 1,),
        in_specs=[
            pl.BlockSpec((tt, D_MODEL), lambda i: (cur(i), 0)),
            pl.BlockSpec((tt, SSD_WIDTH + GLA_V_WIDTH), lambda i: (cur(i), 0)),
            const(w_out_b.shape),
            const((1, D_MODEL)),
            const(w_router_t.shape),
            const((N_EXPERTS, 1)),
        ],
        out_specs=[
            pl.BlockSpec((tt, D_MODEL), lambda i: (cur(i), 0)),
            pl.BlockSpec((tt, D_MODEL), lambda i: (cur(i), 0)),
            pl.BlockSpec((PICK_ROWS, tt), lambda i: (0, prev(i))),
            pl.BlockSpec((PICK_ROWS, tt), lambda i: (0, prev(i))),
            pl.BlockSpec((PICK_ROWS, tt), lambda i: (0, prev(i))),
            pl.BlockSpec((N_EXPERTS, LANES), lambda i: (prev(i), 0)),
            pl.BlockSpec((zero_rows, HALF), lambda i: (i, 0)),
        ],
        out_shape=[
            jax.ShapeDtypeStruct((n, D_MODEL), F32),
            jax.ShapeDtypeStruct((n, D_MODEL), BF16),
            jax.ShapeDtypeStruct((PICK_ROWS, n), jnp.int32),
            jax.ShapeDtypeStruct((PICK_ROWS, n), F32),
            jax.ShapeDtypeStruct((PICK_ROWS, n), jnp.int32),
            jax.ShapeDtypeStruct((n_tiles * N_EXPERTS, LANES), jnp.int32),
            jax.ShapeDtypeStruct(((n_tiles + 1) * zero_rows, HALF), jnp.uint32),
        ],
        scratch_shapes=[pltpu.VMEM((tt, D_MODEL), F32)],
        compiler_params=pltpu.CompilerParams(
            dimension_semantics=("arbitrary",), vmem_limit_bytes=VMEM_LIMIT),
        name="post",
    )(x_flat, mixed, w_out_b, ffn_norm_w, w_router_t, b_router_col)


def _pack_pairs(x):
    lo = pltpu.bitcast(x[:, :HALF], jnp.uint32) >> 16
    hi = pltpu.bitcast(x[:, HALF:], jnp.uint32) & jnp.uint32(0xFFFF0000)
    return lo | hi


def _unpack_pairs(w):
    lo = pltpu.bitcast(w << 16, F32).astype(BF16)
    hi = pltpu.bitcast(w & jnp.uint32(0xFFFF0000), F32).astype(BF16)
    return lo, hi


def _sorted_positions(idx, lrank, offs_ref, base):
    pos = lrank
    for e in range(N_EXPERTS):
        pos = pos + jnp.where(idx == e, offs_ref[base + e], 0)
    return pos


def _group_copies(dst_ref, make_copy):
    return [make_copy(g * SEG_ALIGN, pl.multiple_of(dst_ref[0, 0, g], SEG_ALIGN))
            for g in range(SORT_GROUPS)]


def _dispatch_kernel(offs_ref, idx_ref, lrank_ref, t_ref, dst_ref, dst_prev_ref, buf_in_ref, buf_ref,
                     sorted_ref, sem):
    del buf_in_ref
    i = pl.program_id(0)
    slot = i % 2
    base = i * N_EXPERTS
    tt = t_ref.shape[0]
    pos = _sorted_positions(idx_ref[...], lrank_ref[...], offs_ref, base)
    t = t_ref[...]
    for r0 in range(0, SORT_ROWS, tt):
        prow = lax.broadcasted_iota(jnp.int32, (tt, tt), 0) + r0
        perm = jnp.zeros((tt, tt), F32)
        for kk in range(TOP_K):
            perm = jnp.where(prow == pos[kk:kk + 1, :], 1.0, perm)
        sorted_ref[slot, r0:r0 + tt, :] = _pack_pairs(_dot(perm.astype(BF16), t))

    def copies(table_ref, which):
        def make_copy(local, glob):
            return pltpu.make_async_copy(
                sorted_ref.at[which, pl.ds(local, SEG_ALIGN), :],
                buf_ref.at[pl.ds(glob, SEG_ALIGN), :], sem.at[which])
        return _group_copies(table_ref, make_copy)

    for cp in copies(dst_ref, slot):
        cp.start()

    @pl.when(i > 0)
    def _():
        for cp in copies(dst_prev_ref, 1 - slot):
            cp.wait()

    @pl.when(i == pl.num_programs(0) - 1)
    def _():
        for cp in copies(dst_ref, slot):
            cp.wait()


def _dispatch(offs, group_dst, top_idx, lrank, t_b, buf0):
    n = t_b.shape[0]
    tt = TOKEN_TILE
    table = lambda index: pl.BlockSpec((1, 1, SORT_GROUPS), index, memory_space=pltpu.SMEM)
    grid_spec = pltpu.PrefetchScalarGridSpec(
        num_scalar_prefetch=1,
        grid=(n // tt,),
        in_specs=[
            pl.BlockSpec((PICK_ROWS, tt), lambda i, *_: (0, i)),
            pl.BlockSpec((PICK_ROWS, tt), lambda i, *_: (0, i)),
            pl.BlockSpec((tt, D_MODEL), lambda i, *_: (i, 0)),
            table(lambda i, *_: (i, 0, 0)),
            table(lambda i, *_: (jnp.maximum(i - 1, 0), 0, 0)),
            pl.BlockSpec(memory_space=pl.ANY),
        ],
        out_specs=pl.BlockSpec(memory_space=pl.ANY),
        scratch_shapes=[pltpu.VMEM((2, SORT_ROWS, HALF), jnp.uint32), pltpu.SemaphoreType.DMA((2,))],
    )
    return pl.pallas_call(
        _dispatch_kernel,
        grid_spec=grid_spec,
        out_shape=jax.ShapeDtypeStruct(buf0.shape, jnp.uint32),
        input_output_aliases={6: 0},
        compiler_params=pltpu.CompilerParams(
            dimension_semantics=("arbitrary",), vmem_limit_bytes=VMEM_LIMIT),
        name="dispatch",
    )(offs, top_idx, lrank, t_b, group_dst, group_dst, buf0)


def _expert_kernel(be_ref, nused_ref, xp_ref, wgu_ref, bgu_ref, wd_ref, bd_ref, y_ref,
                   wgu_b_ref, wd_b_ref):
    i = pl.program_id(0)
    used = i < nused_ref[0]

    @pl.when(jnp.logical_not(used))
    def _():
        y_ref[...] = jnp.zeros_like(y_ref)

    new_expert = jnp.logical_or(i == 0, be_ref[i] != be_ref[jnp.maximum(i - 1, 0)])

    @pl.when(jnp.logical_and(used, new_expert))
    def _():
        rows = CHUNK
        for r0 in range(0, D_MODEL, rows):
            wgu_b_ref[r0:r0 + rows, :] = wgu_ref[r0:r0 + rows, :].astype(BF16)
        for r0 in range(0, D_FF, rows):
            wd_b_ref[r0:r0 + rows, :] = wd_ref[r0:r0 + rows, :].astype(BF16)

    @pl.when(used)
    def _():
        x_lo, x_hi = _unpack_pairs(xp_ref[...])
        hgu = _dot(x_lo, wgu_b_ref[:HALF, :]) + _dot(x_hi, wgu_b_ref[HALF:, :]) + bgu_ref[...]
        gate = jnp.minimum(hgu[:, :D_FF], SWIGLU_LIMIT)
        up = jnp.clip(hgu[:, D_FF:], -SWIGLU_LIMIT, SWIGLU_LIMIT)
        act = gate * jax.nn.sigmoid(SWIGLU_ALPHA * gate)
        y = _dot(((up + 1.0) * act).astype(BF16), wd_b_ref[...]) + bd_ref[...]
        y_ref[...] = _pack_pairs(y.astype(BF16).astype(F32))


def _experts(block_expert, n_used, buf, w_gu, b_gu, w_d, b_d):
    n_blocks = block_expert.shape[0]
    rows = n_blocks * MOE_BLOCK
    row_map = lambda i, be, nu: (jnp.minimum(i, nu[0] - 1), 0)
    w_map = lambda i, be, nu: (be[jnp.minimum(i, nu[0] - 1)], 0, 0)
    grid_spec = pltpu.PrefetchScalarGridSpec(
        num_scalar_prefetch=2,
        grid=(n_blocks,),
        in_specs=[
            pl.BlockSpec((MOE_BLOCK, HALF), row_map),
            pl.BlockSpec((None, D_MODEL, 2 * D_FF), w_map),
            pl.BlockSpec((None, 1, 2 * D_FF), w_map),
            pl.BlockSpec((None, D_FF, D_MODEL), w_map),
            pl.BlockSpec((None, 1, D_MODEL), w_map),
        ],
        out_specs=pl.BlockSpec((MOE_BLOCK, HALF), lambda i, be, nu: (i, 0)),
        scratch_shapes=[pltpu.VMEM((D_MODEL, 2 * D_FF), BF16), pltpu.VMEM((D_FF, D_MODEL), BF16)],
    )
    return pl.pallas_call(
        _expert_kernel,
        grid_spec=grid_spec,
        out_shape=jax.ShapeDtypeStruct((rows, HALF), jnp.uint32),
        compiler_params=pltpu.CompilerParams(
            dimension_semantics=("arbitrary",), vmem_limit_bytes=VMEM_LIMIT),
        name="experts",
    )(block_expert, n_used, buf, w_gu, b_gu, w_d, b_d)


def _combine_kernel(offs_ref, idx_ref, lrank_ref, gate_ref, h2_ref, fw_ref, dst_ref, dst_next_ref,
                    y_hbm_ref, out_ref, ys_ref, sem):
    i = pl.program_id(0)
    slot = i % 2
    base = i * N_EXPERTS
    tt = h2_ref.shape[0]

    def copies(table_ref, which):
        def make_copy(local, glob):
            return pltpu.make_async_copy(
                y_hbm_ref.at[pl.ds(glob, SEG_ALIGN), :],
                ys_ref.at[which, pl.ds(local, SEG_ALIGN), :], sem.at[which])
        return _group_copies(table_ref, make_copy)

    @pl.when(i == 0)
    def _():
        for cp in copies(dst_ref, slot):
            cp.start()

    @pl.when(i < pl.num_programs(0) - 1)
    def _():
        for cp in copies(dst_next_ref, 1 - slot):
            cp.start()

    pos = _sorted_positions(idx_ref[...], lrank_ref[...], offs_ref, base).astype(F32)
    zpad = jnp.zeros((LANES - PICK_ROWS, tt), F32)
    pos_c = jnp.concatenate([pos, zpad], axis=0).T
    gate_c = jnp.concatenate([gate_ref[...], zpad], axis=0).T
    lane = lax.broadcasted_iota(jnp.int32, (tt, SORT_ROWS), 1).astype(F32)
    pg = jnp.zeros((tt, SORT_ROWS), F32)
    for kk in range(TOP_K):
        pg = jnp.where(lane == pos_c[:, kk:kk + 1], gate_c[:, kk:kk + 1], pg)
    p_hi, p_lo = _split2(pg)

    for cp in copies(dst_ref, slot):
        cp.wait()
    y_lo, y_hi = _unpack_pairs(ys_ref[slot])
    ffn = jnp.concatenate(
        [_dot(p_hi, y_lo) + _dot(p_lo, y_lo), _dot(p_hi, y_hi) + _dot(p_lo, y_hi)], axis=1)
    acc = h2_ref[...] + ffn
    out_ref[...] = acc * lax.rsqrt(jnp.mean(acc * acc, axis=-1, keepdims=True) + EPS) * fw_ref[...]


def _combine(offs, group_dst, top_idx, lrank, gates, h2, final_norm_w, y_buf):
    n = h2.shape[0]
    tt = TOKEN_TILE
    grid_spec = pltpu.PrefetchScalarGridSpec(
        num_scalar_prefetch=1,
        grid=(n // tt,),
        in_specs=[
            pl.BlockSpec((PICK_ROWS, tt), lambda i, *_: (0, i)),
            pl.BlockSpec((PICK_ROWS, tt), lambda i, *_: (0, i)),
            pl.BlockSpec((PICK_ROWS, tt), lambda i, *_: (0, i)),
            pl.BlockSpec((tt, D_MODEL), lambda i, *_: (i, 0)),
            pl.BlockSpec((1, D_MODEL), lambda i, *_: (0, 0)),
            pl.BlockSpec((1, 1, SORT_GROUPS), lambda i, *_: (i, 0, 0), memory_space=pltpu.SMEM),
            pl.BlockSpec((1, 1, SORT_GROUPS), lambda i, *_: (jnp.minimum(i + 1, n // tt - 1), 0, 0),
                         memory_space=pltpu.SMEM),
            pl.BlockSpec(memory_space=pl.ANY),
        ],
        out_specs=pl.BlockSpec((tt, D_MODEL), lambda i, *_: (i, 0)),
        scratch_shapes=[pltpu.VMEM((2, SORT_ROWS, HALF), jnp.uint32), pltpu.SemaphoreType.DMA((2,))],
    )
    return pl.pallas_call(
        _combine_kernel,
        grid_spec=grid_spec,
        out_shape=jax.ShapeDtypeStruct((n, D_MODEL), F32),
        compiler_params=pltpu.CompilerParams(
            dimension_semantics=("arbitrary",), vmem_limit_bytes=VMEM_LIMIT),
        name="combine",
    )(offs, top_idx, lrank, gates, h2, final_norm_w, group_dst, group_dst, y_buf)


def _pad_lanes(v, width):
    return jnp.pad(v, ((0, 0), (0, width - v.shape[1])))


def kernel(x, meta_tokens, mix_norm_w, w_in, conv_w, conv_b, dt_bias, a_log, d_skip, ssd_norm_w,
           w_decay_up, b_decay, gla_norm_w, w_out, ffn_norm_w, w_router, b_router, w_gate_up,
           b_gate_up, w_down, b_down, final_norm_w):
    batch, seq, d = x.shape
    assert d == D_MODEL and seq % TOKEN_TILE == 0
    assert mix_norm_w.shape[0] == 1, "single-layer block"
    nchunks = (FRONT_PAD + N_META + seq) // CHUNK

    n = batch * seq
    x_flat = x.reshape(n, D_MODEL)
    meta_tile = jnp.concatenate(
        [jnp.zeros((TOKEN_TILE - N_META, D_MODEL), x.dtype), meta_tokens.astype(x.dtype)], axis=0)

    wi = w_in[0]
    o_z, o_xbc = 0, SSD_WIDTH
    o_dt = o_xbc + XBC_WIDTH
    o_q = o_dt + SSD_HEADS
    o_k = o_q + GLA_K_WIDTH
    o_v = o_k + GLA_K_WIDTH
    o_g = o_v + GLA_V_WIDTH
    o_a = o_g + GLA_V_WIDTH
    w_misc = jnp.concatenate(
        [wi[:, o_dt:o_dt + SSD_HEADS], wi[:, o_a:o_a + GLA_RANK],
         jnp.zeros((D_MODEL, MISC_WIDTH - SSD_HEADS - GLA_RANK), wi.dtype)], axis=1)
    w_parts = (wi.astype(BF16), wi[:, o_q:o_a].astype(BF16), w_misc.astype(BF16))
    z, xbc, q, k, v, g, misc = _in_proj(x_flat, meta_tile, mix_norm_w[0][None, :], w_parts)

    dtb = _pad_lanes(dt_bias[0][None, :].astype(F32), MISC_WIDTH)
    aneg = _pad_lanes(-jnp.exp(a_log[0].astype(F32))[None, :], MISC_WIDTH)
    dskip = jnp.repeat(d_skip[0].astype(F32), SSD_HEAD_DIM)[None, :]
    wdec = jnp.zeros((MISC_WIDTH, GLA_K_WIDTH), F32).at[SSD_HEADS:SSD_HEADS + GLA_RANK].set(w_decay_up[0])
    params = (conv_w[0], conv_b[0][None, :], dtb, aneg, dskip, ssd_norm_w[0][None, :],
              wdec.astype(BF16), b_decay[0][None, :], gla_norm_w[0][None, :])
    mixed = _mixer(z, xbc, q, k, v, g, misc, params, batch, nchunks)

    wr_hi, wr_lo = _split2(w_router[0].T.astype(F32))
    n_tiles = n // TOKEN_TILE
    n_blocks = -(-(n * TOP_K + n_tiles * N_EXPERTS * (SEG_ALIGN - 1)) // MOE_BLOCK) + N_EXPERTS
    spare = n_blocks * MOE_BLOCK
    total_rows = spare + 2 * SORT_ROWS
    zero_rows = -(-total_rows // ((n_tiles + 1) * SEG_ALIGN)) * SEG_ALIGN
    h2, t_b, top_idx, gates, lrank, tile_cnt, buf0 = _post(
        x_flat, mixed, w_out[0].astype(BF16), ffn_norm_w[0][None, :],
        jnp.stack([wr_hi, wr_lo]), b_router[0][:, None], zero_rows)

    tile_cnt = tile_cnt.reshape(n_tiles, N_EXPERTS, LANES)[:, :, 0]
    seg_rows = (tile_cnt + SEG_ALIGN - 1) // SEG_ALIGN * SEG_ALIGN
    counts = jnp.sum(seg_rows, axis=0)
    padded = (counts + MOE_BLOCK - 1) // MOE_BLOCK * MOE_BLOCK
    pend = jnp.cumsum(padded)
    pstart = pend - padded
    dstart = pstart[None, :] + jnp.cumsum(seg_rows, axis=0) - seg_rows
    seg_end = jnp.cumsum(seg_rows, axis=1)
    offs = seg_end - seg_rows
    block_pos = jnp.arange(total_rows // MOE_BLOCK, dtype=jnp.int32) * MOE_BLOCK
    block_expert = jnp.minimum(
        jnp.sum((pend[None, :] <= block_pos[:, None]).astype(jnp.int32), axis=1), N_EXPERTS - 1)
    n_used = (pend[-1:] // MOE_BLOCK).astype(jnp.int32)
    grow = jnp.arange(SORT_GROUPS, dtype=jnp.int32)[None, :, None] * SEG_ALIGN
    inside = (offs[:, None, :] <= grow) & (grow < seg_end[:, None, :])
    group_dst = jnp.sum(jnp.where(inside, dstart[:, None, :] + grow - offs[:, None, :], 0), axis=2)
    parity = (jnp.arange(n_tiles, dtype=jnp.int32) % 2)[:, None]
    group_dst = jnp.where(jnp.any(inside, axis=2), group_dst, spare + parity * SORT_ROWS + grow[:, :, 0])
    group_dst = group_dst.astype(jnp.int32)[:, None, :]
    offs = offs.reshape(-1).astype(jnp.int32)

    buf = _dispatch(offs, group_dst, top_idx, lrank, t_b, buf0)
    y_buf = _experts(block_expert, n_used, buf, w_gate_up[0], b_gate_up[0][:, None, :],
                     w_down[0], b_down[0][:, None, :])
    out = _combine(offs, group_dst, top_idx, lrank, gates, h2, final_norm_w[None, :], y_buf)
    return out.reshape(batch, seq, D_MODEL)
```
